```python
import math
import jax, jax.numpy as jnp
from jax import lax
import numpy as np

D_MODEL = 2048
BATCH = 4
SEQ = 2048
DEPTH = 1

HEAD_DIM = 128
N_Q_HEADS = 8
N_KV_HEADS = 2
Q_PER_KV = N_Q_HEADS // N_KV_HEADS
ATTN_WIDTH = N_Q_HEADS * HEAD_DIM
KV_WIDTH = N_KV_HEADS * HEAD_DIM
FOURIER_GROUPS = 8
FOURIER_GROUP_DIM = 128
FOURIER_WIDTH = FOURIER_GROUPS * FOURIER_GROUP_DIM
MIX_WIDTH = ATTN_WIDTH + FOURIER_WIDTH
IN_WIDTH = ATTN_WIDTH + 2 * KV_WIDTH + FOURIER_WIDTH
Q_BLOCK = 128
GRID_W = 64
ROPE_THETA = 10000.0
ROPE_PAIRS = HEAD_DIM // 4
D_FF = 5632
CONV_WIDTH = 3
EPS = 1e-6

kernel_name = "hybrid_attn_fourier_convffn_encoder"


def rmsnorm(x, g):
    xf = x.astype(jnp.float32)
    xf = xf * lax.rsqrt(jnp.mean(xf * xf, axis=-1, keepdims=True) + EPS)
    return xf.astype(x.dtype) * g


def rope_tables(pos, dtype):
    inv_freq = ROPE_THETA ** (-jnp.arange(ROPE_PAIRS, dtype=jnp.float32) / ROPE_PAIRS)
    ang = pos.astype(jnp.float32)[:, None] * inv_freq[None, :]
    ang = jnp.concatenate([ang, ang], axis=-1)[:, None, :]
    return jnp.cos(ang).astype(dtype), jnp.sin(ang).astype(dtype)


def rotate(v, cos, sin):
    h = v.shape[-1] // 2
    rot = jnp.concatenate([-v[..., h:], v[..., :h]], axis=-1)
    return v * cos + rot * sin


def axial_rope(x, cos_r, sin_r, cos_c, sin_c):
    half = x.shape[-1] // 2
    return jnp.concatenate([rotate(x[..., :half], cos_r, sin_r),
                            rotate(x[..., half:], cos_c, sin_c)], axis=-1)


def attention_group(q, k, v):
    B, S = q.shape[0], q.shape[1]
    n_blk = S // Q_BLOCK
    scale = 1.0 / math.sqrt(HEAD_DIM)
    qb = q.reshape(B, n_blk, Q_BLOCK, N_KV_HEADS, Q_PER_KV, HEAD_DIM)
    qb = jnp.transpose(qb, (1, 0, 2, 3, 4, 5))

    def attend_block(qblk):
        s = jnp.einsum('bqkgd,bskd->bkgqs', qblk, k).astype(jnp.float32) * scale
        p = jax.nn.softmax(s, axis=-1).astype(v.dtype)
        return jnp.einsum('bkgqs,bskd->bqkgd', p, v)

    o = lax.map(attend_block, qb)
    o = jnp.transpose(o, (1, 0, 2, 3, 4, 5))
    return o.reshape(B, S, ATTN_WIDTH)


def fourier_group(f, w_fmix):
    B, S = f.shape[0], f.shape[1]
    fg = f.reshape(B, S, FOURIER_GROUPS, FOURIER_GROUP_DIM).astype(jnp.float32)
    spec = jnp.fft.fft2(fg, axes=(1, 3), norm='ortho').real.astype(f.dtype)
    out = jnp.einsum('bsgc,gcd->bsgd', spec, w_fmix)
    return out.reshape(B, S, FOURIER_WIDTH)


def depthwise_conv_centred(h, w, b):
    hp = jnp.pad(h, ((0, 0), (1, 1), (0, 0)))
    S = h.shape[1]
    return hp[:, 0:S] * w[0] + hp[:, 1:S + 1] * w[1] + hp[:, 2:S + 2] * w[2] + b


def setup_inputs(seed: int = 0) -> dict:
    key = jax.random.key(seed)
    ks = jax.random.split(key, 16)
    f32 = jnp.float32

    def nrm(k, shape, fan_in):
        return jax.random.normal(k, shape, f32) * (fan_in ** -0.5)

    def gain(k, shape):
        return 1.0 + 0.02 * jax.random.normal(k, shape, f32)

    return {
        "x": jax.random.normal(ks[0], (BATCH, SEQ, D_MODEL), f32),
        "norm1_g": gain(ks[1], (DEPTH, D_MODEL)),
        "w_in": nrm(ks[2], (DEPTH, D_MODEL, IN_WIDTH), D_MODEL),
        "q_norm_g": gain(ks[3], (DEPTH, HEAD_DIM)),
        "k_norm_g": gain(ks[4], (DEPTH, HEAD_DIM)),
        "w_fmix": nrm(ks[5], (DEPTH, FOURIER_GROUPS, FOURIER_GROUP_DIM, FOURIER_GROUP_DIM), FOURIER_GROUP_DIM),
        "attn_out_g": gain(ks[6], (DEPTH, ATTN_WIDTH)),
        "fourier_out_g": gain(ks[7], (DEPTH, FOURIER_WIDTH)),
        "w_out": nrm(ks[8], (DEPTH, MIX_WIDTH, D_MODEL), MIX_WIDTH),
        "norm2_g": gain(ks[9], (DEPTH, D_MODEL)),
        "w_up": nrm(ks[10], (DEPTH, D_MODEL, 2 * D_FF), D_MODEL),
        "conv_w": nrm(ks[11], (DEPTH, CONV_WIDTH, 2 * D_FF), CONV_WIDTH),
        "conv_b": 0.01 * jax.random.normal(ks[12], (DEPTH, 2 * D_FF), f32),
        "w_down": nrm(ks[13], (DEPTH, D_FF, D_MODEL), D_FF),
        "final_g": gain(ks[14], (D_MODEL,)),
    }


def reference(x, norm1_g, w_in, q_norm_g, k_norm_g, w_fmix, attn_out_g, fourier_out_g,
              w_out, norm2_g, w_up, conv_w, conv_b, w_down, final_g):
    B, S = x.shape[0], x.shape[1]
    ROWS = S // GRID_W
    row = jnp.repeat(jnp.arange(ROWS, dtype=jnp.int32), GRID_W)
    col = jnp.tile(jnp.arange(GRID_W, dtype=jnp.int32), ROWS)
    cos_r, sin_r = rope_tables(row, x.dtype)
    cos_c, sin_c = rope_tables(col, x.dtype)

    h = x
    for l in range(DEPTH):
        u = rmsnorm(h, norm1_g[l])
        proj = jnp.einsum('bsd,de->bse', u, w_in[l])
        q = proj[..., :ATTN_WIDTH].reshape(B, S, N_Q_HEADS, HEAD_DIM)
        k = proj[..., ATTN_WIDTH:ATTN_WIDTH + KV_WIDTH].reshape(B, S, N_KV_HEADS, HEAD_DIM)
        v = proj[..., ATTN_WIDTH + KV_WIDTH:ATTN_WIDTH + 2 * KV_WIDTH].reshape(B, S, N_KV_HEADS, HEAD_DIM)
        f = proj[..., ATTN_WIDTH + 2 * KV_WIDTH:]

        q = axial_rope(rmsnorm(q, q_norm_g[l]), cos_r, sin_r, cos_c, sin_c)
        k = axial_rope(rmsnorm(k, k_norm_g[l]), cos_r, sin_r, cos_c, sin_c)
        a_out = rmsnorm(attention_group(q, k, v), attn_out_g[l])
        f_out = rmsnorm(fourier_group(f, w_fmix[l]), fourier_out_g[l])
        mix = jnp.concatenate([a_out, f_out], axis=-1)
        h = h + jnp.einsum('bse,ed->bsd', mix, w_out[l])

        u2 = rmsnorm(h, norm2_g[l])
        up = jnp.einsum('bsd,df->bsf', u2, w_up[l])
        up = depthwise_conv_centred(up, conv_w[l], conv_b[l])
        gate, val = up[..., :D_FF], up[..., D_FF:]
        h = h + jnp.einsum('bsf,fd->bsd', jax.nn.silu(gate) * val, w_down[l])

    return rmsnorm(h, final_g)
```

```python
import functools
import math

import jax
import jax.numpy as jnp
from jax import lax
from jax.experimental import pallas as pl
from jax.experimental.pallas import tpu as pltpu

F32 = jnp.float32
BF16 = jnp.bfloat16

HEAD_DIM = 128
N_Q_HEADS = 8
N_KV_HEADS = 2
Q_PER_KV = N_Q_HEADS // N_KV_HEADS
ATTN_WIDTH = N_Q_HEADS * HEAD_DIM
KV_WIDTH = N_KV_HEADS * HEAD_DIM
FOURIER_GROUPS = 8
FOURIER_GROUP_DIM = 128
FOURIER_WIDTH = FOURIER_GROUPS * FOURIER_GROUP_DIM
GRID_W = 64
ROPE_THETA = 10000.0
ROPE_PAIRS = HEAD_DIM // 4
EPS = 1e-6

VMEM_LIMIT_BYTES = 56 * 1024 * 1024

IN_TM, IN_TN = 1024, 512
ATTN_TQ = 256
FOURIER_TM = 512
OUT_TM, OUT_TN = 512, 512
UP_TN = 256
DOWN_TM, DOWN_TK = 1024, 512


def _params(*sem):
    return pltpu.CompilerParams(dimension_semantics=sem, vmem_limit_bytes=VMEM_LIMIT_BYTES)


def _rms(x):
    return x * lax.rsqrt(jnp.mean(x * x, axis=-1, keepdims=True) + EPS)


def _in_proj_kernel(x_ref, g_ref, w_ref, o_ref, u_scr):
    @pl.when(pl.program_id(1) == 0)
    def _():
        u_scr[...] = (_rms(x_ref[...]) * g_ref[...]).astype(BF16)

    o_ref[...] = jnp.dot(u_scr[...], w_ref[...].astype(BF16),
                         preferred_element_type=F32).astype(o_ref.dtype)


def _in_proj(x2, g, w):
    t, d = x2.shape
    n = w.shape[1]
    return pl.pallas_call(
        _in_proj_kernel,
        grid=(t // IN_TM, n // IN_TN),
        in_specs=[
            pl.BlockSpec((IN_TM, d), lambda m, j: (m, 0)),
            pl.BlockSpec((1, d), lambda m, j: (0, 0)),
            pl.BlockSpec((d, IN_TN), lambda m, j: (0, j)),
        ],
        out_specs=pl.BlockSpec((IN_TM, IN_TN), lambda m, j: (m, j)),
        out_shape=jax.ShapeDtypeStruct((t, n), BF16),
        scratch_shapes=[pltpu.VMEM((IN_TM, d), BF16)],
        compiler_params=_params("parallel", "arbitrary"),
        name="in_proj",
    )(x2, g, w)


def _rope(x, cos, sin_lo, sin_hi):
    return (x * cos
            + pltpu.roll(x, HEAD_DIM - ROPE_PAIRS, axis=1) * sin_lo
            + pltpu.roll(x, ROPE_PAIRS, axis=1) * sin_hi)


def _attn_kernel(q_ref, k_ref, v_ref, cq_ref, slq_ref, shq_ref, ck_ref, slk_ref, shk_ref,
                 qg_ref, kg_ref, o_ref, k_scr):
    @pl.when(pl.program_id(2) == 0)
    def _():
        k = _rms(k_ref[...].astype(F32)) * kg_ref[...]
        k_scr[...] = _rope(k, ck_ref[...], slk_ref[...], shk_ref[...]).astype(BF16)

    scale = 1.0 / math.sqrt(HEAD_DIM)
    cos, sin_lo, sin_hi = cq_ref[...], slq_ref[...], shq_ref[...]
    kk = k_scr[...]
    vv = v_ref[...]
    for h in range(Q_PER_KV):
        sl = slice(h * HEAD_DIM, (h + 1) * HEAD_DIM)
        q = _rms(q_ref[:, sl].astype(F32)) * qg_ref[...]
        q = (_rope(q, cos, sin_lo, sin_hi) * scale).astype(BF16)
        s = lax.dot_general(q, kk, (((1,), (1,)), ((), ())), preferred_element_type=F32)
        p = jnp.exp(s - jnp.max(s, axis=-1, keepdims=True))
        l = jnp.sum(p, axis=-1, keepdims=True)
        o = jnp.dot(p.astype(BF16), vv, preferred_element_type=F32)
        o_ref[:, sl] = (o / l).astype(o_ref.dtype)


def _attention(proj, tabs, qg, kg, batch, seq):
    t = proj.shape[0]
    nq = seq // ATTN_TQ
    gw = Q_PER_KV * HEAD_DIM
    k_blk0 = ATTN_WIDTH // HEAD_DIM
    v_blk0 = (ATTN_WIDTH + KV_WIDTH) // HEAD_DIM
    cos, sin_lo, sin_hi = tabs
    q_tab = pl.BlockSpec((ATTN_TQ, HEAD_DIM), lambda b, g, i: (i, 0))
    k_tab = pl.BlockSpec((seq, HEAD_DIM), lambda b, g, i: (0, 0))
    gain = pl.BlockSpec((1, HEAD_DIM), lambda b, g, i: (0, 0))
    return pl.pallas_call(
        _attn_kernel,
        grid=(batch, N_KV_HEADS, nq),
        in_specs=[
            pl.BlockSpec((ATTN_TQ, gw), lambda b, g, i: (b * nq + i, g)),
            pl.BlockSpec((seq, HEAD_DIM), lambda b, g, i: (b, k_blk0 + g)),
            pl.BlockSpec((seq, HEAD_DIM), lambda b, g, i: (b, v_blk0 + g)),
            q_tab, q_tab, q_tab, k_tab, k_tab, k_tab, gain, gain,
        ],
        out_specs=pl.BlockSpec((ATTN_TQ, gw), lambda b, g, i: (b * nq + i, g)),
        out_shape=jax.ShapeDtypeStruct((t, ATTN_WIDTH), BF16),
        scratch_shapes=[pltpu.VMEM((seq, HEAD_DIM), BF16)],
        compiler_params=_params("parallel", "parallel", "arbitrary"),
        name="attention",
    )(proj, proj, proj, cos, sin_lo, sin_hi, cos, sin_lo, sin_hi, qg, kg)


def _fourier_fold_kernel(w_ref, cc_ref, sc_ref, o_ref):
    for g in range(FOURIER_GROUPS):
        w = w_ref[g]
        a = jnp.dot(cc_ref[...], w, preferred_element_type=F32, precision=lax.Precision.HIGHEST)
        b = jnp.dot(sc_ref[...], w, preferred_element_type=F32, precision=lax.Precision.HIGHEST)
        o_ref[g, :, :FOURIER_GROUP_DIM] = a.astype(o_ref.dtype)
        o_ref[g, :, FOURIER_GROUP_DIM:] = b.astype(o_ref.dtype)


def _fourier_fold(w_fmix, cc, sc):
    return pl.pallas_call(
        _fourier_fold_kernel,
        out_shape=jax.ShapeDtypeStruct((FOURIER_GROUPS, FOURIER_GROUP_DIM, 2 * FOURIER_GROUP_DIM), BF16),
        name="fourier_fold",
    )(w_fmix, cc, sc)


def _fourier_kernel(flo_ref, fhi_ref, ab_ref, c_ref, ms_ref, o_ref, za_scr, zb_scr):
    @pl.when(pl.program_id(1) == 0)
    def _():
        half = FOURIER_GROUPS // 2
        for g in range(FOURIER_GROUPS):
            src = flo_ref if g < half else fhi_ref
            lo = (g % half) * FOURIER_GROUP_DIM
            z = jnp.dot(src[:, lo:lo + FOURIER_GROUP_DIM], ab_ref[g], preferred_element_type=F32)
            dst = slice(g * FOURIER_GROUP_DIM, (g + 1) * FOURIER_GROUP_DIM)
            za_scr[:, dst] = z[:, :FOURIER_GROUP_DIM].astype(BF16)
            zb_scr[:, dst] = z[:, FOURIER_GROUP_DIM:].astype(BF16)

    acc = jnp.dot(c_ref[...], za_scr[...], preferred_element_type=F32)
    acc = acc + jnp.dot(ms_ref[...], zb_scr[...], preferred_element_type=F32)
    o_ref[...] = acc.astype(o_ref.dtype)


def _fourier(proj, ab, dft_c, dft_ms, batch, seq):
    t = proj.shape[0]
    nm = seq // FOURIER_TM
    half_w = FOURIER_WIDTH // 2
    f_blk0 = (ATTN_WIDTH + 2 * KV_WIDTH) // half_w
    return pl.pallas_call(
        _fourier_kernel,
        grid=(batch, nm),
        in_specs=[
            pl.BlockSpec((seq, half_w), lambda b, m: (b, f_blk0)),
            pl.BlockSpec((seq, half_w), lambda b, m: (b, f_blk0 + 1)),
            pl.BlockSpec(ab.shape, lambda b, m: (0, 0, 0)),
            pl.BlockSpec((FOURIER_TM, seq), lambda b, m: (m, 0)),
            pl.BlockSpec((FOURIER_TM, seq), lambda b, m: (m, 0)),
        ],
        out_specs=pl.BlockSpec((FOURIER_TM, FOURIER_WIDTH), lambda b, m: (b * nm + m, 0)),
        out_shape=jax.ShapeDtypeStruct((t, FOURIER_WIDTH), BF16),
        scratch_shapes=[pltpu.VMEM((seq, FOURIER_WIDTH), BF16), pltpu.VMEM((seq, FOURIER_WIDTH), BF16)],
        compiler_params=_params("parallel", "arbitrary"),
        name="fourier",
    )(proj, proj, ab, dft_c, dft_ms)


def _out_proj_kernel(a_ref, f_ref, x_ref, ag_ref, fg_ref, w_ref, g2_ref, h_ref, u_ref, mix_scr):
    j = pl.program_id(1)

    @pl.when(j == 0)
    def _():
        mix_scr[:, :ATTN_WIDTH] = (_rms(a_ref[...].astype(F32)) * ag_ref[...]).astype(BF16)
        mix_scr[:, ATTN_WIDTH:] = (_rms(f_ref[...].astype(F32)) * fg_ref[...]).astype(BF16)

    col = pl.multiple_of(j * OUT_TN, OUT_TN)
    h_ref[:, pl.ds(col, OUT_TN)] = x_ref[:, pl.ds(col, OUT_TN)] + jnp.dot(
        mix_scr[...], w_ref[...].astype(BF16), preferred_element_type=F32)

    @pl.when(j == pl.num_programs(1) - 1)
    def _():
        u_ref[...] = (_rms(h_ref[...]) * g2_ref[...]).astype(u_ref.dtype)


def _out_proj(a, f, x2, ag, fg, w, g2):
    t, d = x2.shape
    row = lambda width: pl.BlockSpec((OUT_TM, width), lambda m, j: (m, 0))
    vec = lambda width: pl.BlockSpec((1, width), lambda m, j: (0, 0))
    return pl.pallas_call(
        _out_proj_kernel,
        grid=(t // OUT_TM, d // OUT_TN),
        in_specs=[
            row(ATTN_WIDTH), row(FOURIER_WIDTH), row(d), vec(ATTN_WIDTH), vec(FOURIER_WIDTH),
            pl.BlockSpec((w.shape[0], OUT_TN), lambda m, j: (0, j)),
            vec(d),
        ],
        out_specs=[row(d), row(d)],
        out_shape=[jax.ShapeDtypeStruct((t, d), F32), jax.ShapeDtypeStruct((t, d), BF16)],
        scratch_shapes=[pltpu.VMEM((OUT_TM, w.shape[0]), BF16)],
        compiler_params=_params("parallel", "arbitrary"),
        name="out_proj",
    )(a, f, x2, ag, fg, w, g2)


def _conv3(up, cw, cb):
    rows = up.shape[0]
    row = lax.broadcasted_iota(jnp.int32, up.shape, 0)
    prev = jnp.where(row == 0, 0.0, pltpu.roll(up, 1, axis=0))
    nxt = jnp.where(row == rows - 1, 0.0, pltpu.roll(up, rows - 1, axis=0))
    return prev * cw[0:1, :] + up * cw[1:2, :] + nxt * cw[2:3, :] + cb


def _up_gate_kernel(u_ref, wg_ref, wv_ref, cwg_ref, cwv_ref, cbg_ref, cbv_ref, o_ref):
    u = u_ref[...]
    gate = jnp.dot(u, wg_ref[...].astype(BF16), preferred_element_type=F32)
    gate = _conv3(gate, cwg_ref[...], cbg_ref[...])
    val = jnp.dot(u, wv_ref[...].astype(BF16), preferred_element_type=F32)
    val = _conv3(val, cwv_ref[...], cbv_ref[...])
    o_ref[...] = (gate * jax.nn.sigmoid(gate) * val).astype(o_ref.dtype)


def _up_gate(u2, w_up, conv_w, conv_b, batch, seq):
    t, d = u2.shape
    d_ff = w_up.shape[1] // 2
    nj = d_ff // UP_TN
    gate_col = lambda b, j: (0, j)
    val_col = lambda b, j: (0, nj + j)
    return pl.pallas_call(
        _up_gate_kernel,
        grid=(batch, nj),
        in_specs=[
            pl.BlockSpec((seq, d), lambda b, j: (b, 0)),
            pl.BlockSpec((d, UP_TN), gate_col),
            pl.BlockSpec((d, UP_TN), val_col),
            pl.BlockSpec((conv_w.shape[0], UP_TN), gate_col),
            pl.BlockSpec((conv_w.shape[0], UP_TN), val_col),
            pl.BlockSpec((1, UP_TN), gate_col),
            pl.BlockSpec((1, UP_TN), val_col),
        ],
        out_specs=pl.BlockSpec((seq, UP_TN), lambda b, j: (b, j)),
        out_shape=jax.ShapeDtypeStruct((t, d_ff), BF16),
        compiler_params=_params("parallel", "arbitrary"),
        name="up_gate",
    )(u2, w_up, w_up, conv_w, conv_w, conv_b, conv_b)


def _down_kernel(a_ref, w_ref, h_ref, g_ref, y_ref):
    k = pl.program_id(1)

    @pl.when(k == 0)
    def _():
        y_ref[...] = h_ref[...]

    y_ref[...] += jnp.dot(a_ref[...], w_ref[...].astype(BF16), preferred_element_type=F32)

    @pl.when(k == pl.num_programs(1) - 1)
    def _():
        y_ref[...] = _rms(y_ref[...]) * g_ref[...]


def _down(act, w, h1, g):
    t, d_ff = act.shape
    d = w.shape[1]
    return pl.pallas_call(
        _down_kernel,
        grid=(t // DOWN_TM, d_ff // DOWN_TK),
        in_specs=[
            pl.BlockSpec((DOWN_TM, DOWN_TK), lambda m, k: (m, k)),
            pl.BlockSpec((DOWN_TK, d), lambda m, k: (k, 0)),
            pl.BlockSpec((DOWN_TM, d), lambda m, k: (m, 0)),
            pl.BlockSpec((1, d), lambda m, k: (0, 0)),
        ],
        out_specs=pl.BlockSpec((DOWN_TM, d), lambda m, k: (m, 0)),
        out_shape=jax.ShapeDtypeStruct((t, d), F32),
        compiler_params=_params("parallel", "arbitrary"),
        name="down",
    )(act, w, h1, g)


def _rope_tables(seq):
    t = jnp.arange(seq, dtype=jnp.int32)
    inv_freq = ROPE_THETA ** (-jnp.arange(ROPE_PAIRS, dtype=F32) / ROPE_PAIRS)

    def axis_tables(pos):
        ang = pos.astype(F32)[:, None] * inv_freq[None, :]
        ang = jnp.concatenate([ang, ang], axis=-1)
        return jnp.cos(ang), jnp.sin(ang)

    cos_r, sin_r = axis_tables(t // GRID_W)
    cos_c, sin_c = axis_tables(t % GRID_W)
    cos = jnp.concatenate([cos_r, cos_c], axis=-1)
    sin = jnp.concatenate([sin_r, sin_c], axis=-1)
    first_half = (jnp.arange(HEAD_DIM) % (2 * ROPE_PAIRS)) < ROPE_PAIRS
    sin_lo = jnp.where(first_half[None, :], -sin, 0.0)
    sin_hi = jnp.where(first_half[None, :], 0.0, sin)
    return cos, sin_lo, sin_hi


def _dft_tables(n, scale):
    j = jnp.arange(n, dtype=jnp.int32)
    ang = ((j[:, None] * j[None, :]) % n).astype(F32) * (2.0 * math.pi / n)
    return jnp.cos(ang) * scale, jnp.sin(ang) * scale


def kernel(x, norm1_g, w_in, q_norm_g, k_norm_g, w_fmix, attn_out_g, fourier_out_g, w_out,
           norm2_g, w_up, conv_w, conv_b, w_down, final_g):
    batch, seq, d = x.shape
    depth = w_in.shape[0]
    rope = _rope_tables(seq)
    dft_c, dft_s = _dft_tables(seq, 1.0)
    dft_c, dft_ms = dft_c.astype(BF16), (-dft_s).astype(BF16)
    cc, sc = _dft_tables(FOURIER_GROUP_DIM, 1.0 / math.sqrt(seq * FOURIER_GROUP_DIM))

    assert depth == 1
    h = x.reshape(batch * seq, d)
    for l in range(depth):
        proj = _in_proj(h, norm1_g[l][None], w_in[l])
        attn = _attention(proj, rope, q_norm_g[l][None], k_norm_g[l][None], batch, seq)
        ab = _fourier_fold(w_fmix[l], cc, sc)
        four = _fourier(proj, ab, dft_c, dft_ms, batch, seq)
        h1, u2 = _out_proj(attn, four, h, attn_out_g[l][None], fourier_out_g[l][None],
                           w_out[l], norm2_g[l][None])
        act = _up_gate(u2, w_up[l], conv_w[l], conv_b[l][None], batch, seq)
        h = _down(act, w_down[l], h1, final_g[None])
    return h.reshape(batch, seq, d)
```

```python
import functools
import math

import jax
import jax.numpy as jnp
from jax import lax
from jax.experimental import pallas as pl
from jax.experimental.pallas import tpu as pltpu

F32 = jnp.float32
BF16 = jnp.bfloat16

HEAD_DIM = 128
N_Q_HEADS = 8
N_KV_HEADS = 2
Q_PER_KV = N_Q_HEADS // N_KV_HEADS
ATTN_WIDTH = N_Q_HEADS * HEAD_DIM
KV_WIDTH = N_KV_HEADS * HEAD_DIM
FOURIER_GROUPS = 8
FOURIER_GROUP_DIM = 128
FOURIER_WIDTH = FOURIER_GROUPS * FOURIER_GROUP_DIM
GRID_W = 64
ROPE_THETA = 10000.0
ROPE_PAIRS = HEAD_DIM // 4
EPS = 1e-6

VMEM_LIMIT_BYTES = 56 * 1024 * 1024

IN_TM, IN_TN = 1024, 512
ATTN_TQ = 256
FOURIER_TM = 512
DFT_SUB, DFT_COARSE_PER_STEP = 64, 8
OUT_TM, OUT_TN = 1024, 512
UP_TN = 256
DOWN_TM, DOWN_TK = 1024, 512


def _params(*sem):
    return pltpu.CompilerParams(dimension_semantics=sem, vmem_limit_bytes=VMEM_LIMIT_BYTES)


def _rms(x):
    return x * lax.rsqrt(jnp.mean(x * x, axis=-1, keepdims=True) + EPS)


def _in_proj_kernel(x_ref, g_ref, w_ref, o_ref, u_scr):
    @pl.when(pl.program_id(1) == 0)
    def _():
        u_scr[...] = (_rms(x_ref[...]) * g_ref[...]).astype(BF16)

    o_ref[...] = jnp.dot(u_scr[...], w_ref[...].astype(BF16),
                         preferred_element_type=F32).astype(o_ref.dtype)


def _in_proj(x2, g, w):
    t, d = x2.shape
    n = w.shape[1]
    return pl.pallas_call(
        _in_proj_kernel,
        grid=(t // IN_TM, n // IN_TN),
        in_specs=[
            pl.BlockSpec((IN_TM, d), lambda m, j: (m, 0)),
            pl.BlockSpec((1, d), lambda m, j: (0, 0)),
            pl.BlockSpec((d, IN_TN), lambda m, j: (0, j)),
        ],
        out_specs=pl.BlockSpec((IN_TM, IN_TN), lambda m, j: (m, j)),
        out_shape=jax.ShapeDtypeStruct((t, n), BF16),
        scratch_shapes=[pltpu.VMEM((IN_TM, d), BF16)],
        compiler_params=_params("parallel", "arbitrary"),
        name="in_proj",
    )(x2, g, w)


def _rope(x, cos, sin_lo, sin_hi):
    return (x * cos
            + pltpu.roll(x, HEAD_DIM - ROPE_PAIRS, axis=1) * sin_lo
            + pltpu.roll(x, ROPE_PAIRS, axis=1) * sin_hi)


def _attn_kernel(q_ref, k_ref, v_ref, cq_ref, slq_ref, shq_ref, ck_ref, slk_ref, shk_ref,
                 qg_ref, kg_ref, o_ref, k_scr):
    @pl.when(pl.program_id(2) == 0)
    def _():
        k = _rms(k_ref[...].astype(F32)) * kg_ref[...]
        k_scr[...] = _rope(k, ck_ref[...], slk_ref[...], shk_ref[...]).astype(BF16)

    scale = 1.0 / math.sqrt(HEAD_DIM)
    cos, sin_lo, sin_hi = cq_ref[...], slq_ref[...], shq_ref[...]
    kk = k_scr[...]
    vv = v_ref[...]
    for h in range(Q_PER_KV):
        sl = slice(h * HEAD_DIM, (h + 1) * HEAD_DIM)
        q = _rms(q_ref[:, sl].astype(F32)) * qg_ref[...]
        q = (_rope(q, cos, sin_lo, sin_hi) * scale).astype(BF16)
        s = lax.dot_general(q, kk, (((1,), (1,)), ((), ())), preferred_element_type=F32)
        p = jnp.exp(s - jnp.max(s, axis=-1, keepdims=True))
        l = jnp.sum(p, axis=-1, keepdims=True)
        o = jnp.dot(p.astype(BF16), vv, preferred_element_type=F32)
        o_ref[:, sl] = (o / l).astype(o_ref.dtype)


def _attention(proj, tabs, qg, kg, batch, seq):
    t = proj.shape[0]
    nq = seq // ATTN_TQ
    gw = Q_PER_KV * HEAD_DIM
    k_blk0 = ATTN_WIDTH // HEAD_DIM
    v_blk0 = (ATTN_WIDTH + KV_WIDTH) // HEAD_DIM
    cos, sin_lo, sin_hi = tabs
    q_tab = pl.BlockSpec((ATTN_TQ, HEAD_DIM), lambda b, g, i: (i, 0))
    k_tab = pl.BlockSpec((seq, HEAD_DIM), lambda b, g, i: (0, 0))
    gain = pl.BlockSpec((1, HEAD_DIM), lambda b, g, i: (0, 0))
    return pl.pallas_call(
        _attn_kernel,
        grid=(batch, N_KV_HEADS, nq),
        in_specs=[
            pl.BlockSpec((ATTN_TQ, gw), lambda b, g, i: (b * nq + i, g)),
            pl.BlockSpec((seq, HEAD_DIM), lambda b, g, i: (b, k_blk0 + g)),
            pl.BlockSpec((seq, HEAD_DIM), lambda b, g, i: (b, v_blk0 + g)),
            q_tab, q_tab, q_tab, k_tab, k_tab, k_tab, gain, gain,
        ],
        out_specs=pl.BlockSpec((ATTN_TQ, gw), lambda b, g, i: (b * nq + i, g)),
        out_shape=jax.ShapeDtypeStruct((t, ATTN_WIDTH), BF16),
        scratch_shapes=[pltpu.VMEM((seq, HEAD_DIM), BF16)],
        compiler_params=_params("parallel", "parallel", "arbitrary"),
        name="attention",
    )(proj, proj, proj, cos, sin_lo, sin_hi, cos, sin_lo, sin_hi, qg, kg)


def _fourier_fold_kernel(w_ref, cc_ref, sc_ref, o_ref):
    for g in range(FOURIER_GROUPS):
        w = w_ref[g]
        a = jnp.dot(cc_ref[...], w, preferred_element_type=F32, precision=lax.Precision.HIGHEST)
        b = jnp.dot(sc_ref[...], w, preferred_element_type=F32, precision=lax.Precision.HIGHEST)
        o_ref[g, :, :FOURIER_GROUP_DIM] = a.astype(o_ref.dtype)
        o_ref[g, :, FOURIER_GROUP_DIM:] = b.astype(o_ref.dtype)


def _fourier_fold(w_fmix, cc, sc):
    return pl.pallas_call(
        _fourier_fold_kernel,
        out_shape=jax.ShapeDtypeStruct((FOURIER_GROUPS, FOURIER_GROUP_DIM, 2 * FOURIER_GROUP_DIM), BF16),
        name="fourier_fold",
    )(w_fmix, cc, sc)


def _fourier_kernel(flo_ref, fhi_ref, ab_ref, c_ref, ms_ref, o_ref, za_scr, zb_scr):
    @pl.when(pl.program_id(1) == 0)
    def _():
        half = FOURIER_GROUPS // 2
        for g in range(FOURIER_GROUPS):
            src = flo_ref if g < half else fhi_ref
            lo = (g % half) * FOURIER_GROUP_DIM
            z = jnp.dot(src[:, lo:lo + FOURIER_GROUP_DIM], ab_ref[g], preferred_element_type=F32)
            dst = slice(g * FOURIER_GROUP_DIM, (g + 1) * FOURIER_GROUP_DIM)
            za_scr[:, dst] = z[:, :FOURIER_GROUP_DIM].astype(BF16)
            zb_scr[:, dst] = z[:, FOURIER_GROUP_DIM:].astype(BF16)

    acc = jnp.dot(c_ref[...], za_scr[...], preferred_element_type=F32)
    acc = acc + jnp.dot(ms_ref[...], zb_scr[...], preferred_element_type=F32)
    o_ref[...] = acc.astype(o_ref.dtype)


def _fourier(proj, ab, dft_c, dft_ms, batch, seq):
    t = proj.shape[0]
    nm = seq // FOURIER_TM
    half_w = FOURIER_WIDTH // 2
    f_blk0 = (ATTN_WIDTH + 2 * KV_WIDTH) // half_w
    return pl.pallas_call(
        _fourier_kernel,
        grid=(batch, nm),
        in_specs=[
            pl.BlockSpec((seq, half_w), lambda b, m: (b, f_blk0)),
            pl.BlockSpec((seq, half_w), lambda b, m: (b, f_blk0 + 1)),
            pl.BlockSpec(ab.shape, lambda b, m: (0, 0, 0)),
            pl.BlockSpec((FOURIER_TM, seq), lambda b, m: (m, 0)),
            pl.BlockSpec((FOURIER_TM, seq), lambda b, m: (m, 0)),
        ],
        out_specs=pl.BlockSpec((FOURIER_TM, FOURIER_WIDTH), lambda b, m: (b * nm + m, 0)),
        out_shape=jax.ShapeDtypeStruct((t, FOURIER_WIDTH), BF16),
        scratch_shapes=[pltpu.VMEM((seq, FOURIER_WIDTH), BF16), pltpu.VMEM((seq, FOURIER_WIDTH), BF16)],
        compiler_params=_params("parallel", "arbitrary"),
        name="fourier",
    )(proj, proj, ab, dft_c, dft_ms)


def _out_proj_kernel(a_ref, f_ref, x_ref, ag_ref, fg_ref, w_ref, g2_ref, h_ref, u_ref, mix_scr, h_scr):
    j = pl.program_id(1)

    @pl.when(j == 0)
    def _():
        mix_scr[:, :ATTN_WIDTH] = (_rms(a_ref[...].astype(F32)) * ag_ref[...]).astype(BF16)
        mix_scr[:, ATTN_WIDTH:] = (_rms(f_ref[...].astype(F32)) * fg_ref[...]).astype(BF16)

    h = x_ref[...] + jnp.dot(mix_scr[...], w_ref[...].astype(BF16), preferred_element_type=F32)
    h_ref[...] = h
    h_scr[:, pl.ds(pl.multiple_of(j * OUT_TN, OUT_TN), OUT_TN)] = h

    @pl.when(j == pl.num_programs(1) - 1)
    def _():
        u_ref[...] = (_rms(h_scr[...]) * g2_ref[...]).astype(u_ref.dtype)


def _out_proj(a, f, x2, ag, fg, w, g2):
    t, d = x2.shape
    row = lambda width: pl.BlockSpec((OUT_TM, width), lambda m, j: (m, 0))
    vec = lambda width: pl.BlockSpec((1, width), lambda m, j: (0, 0))
    tile = pl.BlockSpec((OUT_TM, OUT_TN), lambda m, j: (m, j))
    return pl.pallas_call(
        _out_proj_kernel,
        grid=(t // OUT_TM, d // OUT_TN),
        in_specs=[
            row(ATTN_WIDTH), row(FOURIER_WIDTH), tile, vec(ATTN_WIDTH), vec(FOURIER_WIDTH),
            pl.BlockSpec((w.shape[0], OUT_TN), lambda m, j: (0, j)),
            vec(d),
        ],
        out_specs=[tile, row(d)],
        out_shape=[jax.ShapeDtypeStruct((t, d), F32), jax.ShapeDtypeStruct((t, d), BF16)],
        scratch_shapes=[pltpu.VMEM((OUT_TM, w.shape[0]), BF16), pltpu.VMEM((OUT_TM, d), F32)],
        compiler_params=_params("parallel", "arbitrary"),
        name="out_proj",
    )(a, f, x2, ag, fg, w, g2)


def _conv3(up, cw, cb):
    rows = up.shape[0]
    row = lax.broadcasted_iota(jnp.int32, up.shape, 0)
    prev = jnp.where(row == 0, 0.0, pltpu.roll(up, 1, axis=0))
    nxt = jnp.where(row == rows - 1, 0.0, pltpu.roll(up, rows - 1, axis=0))
    return prev * cw[0:1, :] + up * cw[1:2, :] + nxt * cw[2:3, :] + cb


def _up_gate_kernel(u_ref, wg_ref, wv_ref, cwg_ref, cwv_ref, cbg_ref, cbv_ref, o_ref):
    u = u_ref[...]
    gate = jnp.dot(u, wg_ref[...].astype(BF16), preferred_element_type=F32)
    gate = _conv3(gate, cwg_ref[...], cbg_ref[...])
    val = jnp.dot(u, wv_ref[...].astype(BF16), preferred_element_type=F32)
    val = _conv3(val, cwv_ref[...], cbv_ref[...])
    o_ref[...] = (gate * jax.nn.sigmoid(gate) * val).astype(o_ref.dtype)


def _up_gate(u2, w_up, conv_w, conv_b, batch, seq):
    t, d = u2.shape
    d_ff = w_up.shape[1] // 2
    nj = d_ff // UP_TN
    gate_col = lambda b, j: (0, j)
    val_col = lambda b, j: (0, nj + j)
    return pl.pallas_call(
        _up_gate_kernel,
        grid=(batch, nj),
        in_specs=[
            pl.BlockSpec((seq, d), lambda b, j: (b, 0)),
            pl.BlockSpec((d, UP_TN), gate_col),
            pl.BlockSpec((d, UP_TN), val_col),
            pl.BlockSpec((conv_w.shape[0], UP_TN), gate_col),
            pl.BlockSpec((conv_w.shape[0], UP_TN), val_col),
            pl.BlockSpec((1, UP_TN), gate_col),
            pl.BlockSpec((1, UP_TN), val_col),
        ],
        out_specs=pl.BlockSpec((seq, UP_TN), lambda b, j: (b, j)),
        out_shape=jax.ShapeDtypeStruct((t, d_ff), BF16),
        compiler_params=_params("parallel", "arbitrary"),
        name="up_gate",
    )(u2, w_up, w_up, conv_w, conv_w, conv_b, conv_b)


def _down_kernel(a_ref, w_ref, h_ref, g_ref, y_ref):
    k = pl.program_id(1)

    @pl.when(k == 0)
    def _():
        y_ref[...] = h_ref[...]

    y_ref[...] += jnp.dot(a_ref[...], w_ref[...].astype(BF16), preferred_element_type=F32)

    @pl.when(k == pl.num_programs(1) - 1)
    def _():
        y_ref[...] = _rms(y_ref[...]) * g_ref[...]


def _down(act, w, h1, g):
    t, d_ff = act.shape
    d = w.shape[1]
    return pl.pallas_call(
        _down_kernel,
        grid=(t // DOWN_TM, d_ff // DOWN_TK),
        in_specs=[
            pl.BlockSpec((DOWN_TM, DOWN_TK), lambda m, k: (m, k)),
            pl.BlockSpec((DOWN_TK, d), lambda m, k: (k, 0)),
            pl.BlockSpec((DOWN_TM, d), lambda m, k: (m, 0)),
            pl.BlockSpec((1, d), lambda m, k: (0, 0)),
        ],
        out_specs=pl.BlockSpec((DOWN_TM, d), lambda m, k: (m, 0)),
        out_shape=jax.ShapeDtypeStruct((t, d), F32),
        compiler_params=_params("parallel", "arbitrary"),
        name="down",
    )(act, w, h1, g)


def _rope_tables(seq):
    t = jnp.arange(seq, dtype=jnp.int32)
    inv_freq = ROPE_THETA ** (-jnp.arange(ROPE_PAIRS, dtype=F32) / ROPE_PAIRS)

    def axis_tables(pos):
        ang = pos.astype(F32)[:, None] * inv_freq[None, :]
        ang = jnp.concatenate([ang, ang], axis=-1)
        return jnp.cos(ang), jnp.sin(ang)

    cos_r, sin_r = axis_tables(t // GRID_W)
    cos_c, sin_c = axis_tables(t % GRID_W)
    cos = jnp.concatenate([cos_r, cos_c], axis=-1)
    sin = jnp.concatenate([sin_r, sin_c], axis=-1)
    first_half = (jnp.arange(HEAD_DIM) % (2 * ROPE_PAIRS)) < ROPE_PAIRS
    sin_lo = jnp.where(first_half[None, :], -sin, 0.0)
    sin_hi = jnp.where(first_half[None, :], 0.0, sin)
    return cos, sin_lo, sin_hi


def _dft_cos_sin(rows, n_cols, n, scale=1.0):
    k = jnp.arange(n_cols, dtype=jnp.int32)
    ang = ((rows[:, None] * k[None, :]) % n).astype(F32) * (2.0 * math.pi / n)
    return jnp.cos(ang) * scale, jnp.sin(ang) * scale


def _dft_tables(n, scale):
    return _dft_cos_sin(jnp.arange(n, dtype=jnp.int32), n, n, scale)


def _seq_dft_kernel(ca_ref, sa_ref, cr_ref, sr_ref, c_ref, ms_ref):
    cr, sr = cr_ref[...], sr_ref[...]
    for i in range(DFT_COARSE_PER_STEP):
        ca, sa = ca_ref[i:i + 1, :], sa_ref[i:i + 1, :]
        rows = slice(i * DFT_SUB, (i + 1) * DFT_SUB)
        c_ref[rows, :] = (ca * cr - sa * sr).astype(c_ref.dtype)
        ms_ref[rows, :] = (-(sa * cr + ca * sr)).astype(ms_ref.dtype)


def _seq_dft_tables(n):
    coarse = jnp.arange(n // DFT_SUB, dtype=jnp.int32) * DFT_SUB
    ca, sa = _dft_cos_sin(coarse, n, n)
    cr, sr = _dft_cos_sin(jnp.arange(DFT_SUB, dtype=jnp.int32), n, n)
    steps = n // (DFT_SUB * DFT_COARSE_PER_STEP)
    coarse_spec = pl.BlockSpec((DFT_COARSE_PER_STEP, n), lambda i: (i, 0))
    fine_spec = pl.BlockSpec((DFT_SUB, n), lambda i: (0, 0))
    out_spec = pl.BlockSpec((DFT_SUB * DFT_COARSE_PER_STEP, n), lambda i: (i, 0))
    return pl.pallas_call(
        _seq_dft_kernel,
        grid=(steps,),
        in_specs=[coarse_spec, coarse_spec, fine_spec, fine_spec],
        out_specs=[out_spec, out_spec],
        out_shape=[jax.ShapeDtypeStruct((n, n), BF16)] * 2,
        compiler_params=_params("parallel"),
        name="seq_dft_tables",
    )(ca, sa, cr, sr)


def kernel(x, norm1_g, w_in, q_norm_g, k_norm_g, w_fmix, attn_out_g, fourier_out_g, w_out,
           norm2_g, w_up, conv_w, conv_b, w_down, final_g):
    batch, seq, d = x.shape
    depth = w_in.shape[0]
    rope = _rope_tables(seq)
    dft_c, dft_ms = _seq_dft_tables(seq)
    cc, sc = _dft_tables(FOURIER_GROUP_DIM, 1.0 / math.sqrt(seq * FOURIER_GROUP_DIM))

    assert depth == 1
    h = x.reshape(batch * seq, d)
    for l in range(depth):
        proj = _in_proj(h, norm1_g[l][None], w_in[l])
        attn = _attention(proj, rope, q_norm_g[l][None], k_norm_g[l][None], batch, seq)
        ab = _fourier_fold(w_fmix[l], cc, sc)
        four = _fourier(proj, ab, dft_c, dft_ms, batch, seq)
        h1, u2 = _out_proj(attn, four, h, attn_out_g[l][None], fourier_out_g[l][None],
                           w_out[l], norm2_g[l][None])
        act = _up_gate(u2, w_up[l], conv_w[l], conv_b[l][None], batch, seq)
        h = _down(act, w_down[l], h1, final_g[None])
    return h.reshape(batch, seq, d)
```

```python
import functools
import math

import jax
import jax.numpy as jnp
from jax import lax
from jax.experimental import pallas as pl
from jax.experimental.pallas import tpu as pltpu

F32 = jnp.float32
BF16 = jnp.bfloat16

HEAD_DIM = 128
N_Q_HEADS = 8
N_KV_HEADS = 2
Q_PER_KV = N_Q_HEADS // N_KV_HEADS
ATTN_WIDTH = N_Q_HEADS * HEAD_DIM
KV_WIDTH = N_KV_HEADS * HEAD_DIM
FOURIER_GROUPS = 8
FOURIER_GROUP_DIM = 128
FOURIER_WIDTH = FOURIER_GROUPS * FOURIER_GROUP_DIM
GRID_W = 64
ROPE_THETA = 10000.0
ROPE_PAIRS = HEAD_DIM // 4
EPS = 1e-6

VMEM_LIMIT_BYTES = 56 * 1024 * 1024

IN_TM, IN_TN = 1024, 512
ATTN_TQ = 256
FOURIER_TM = 512
DFT_SUB, DFT_COARSE_PER_STEP = 64, 8
OUT_TM, OUT_TN = 1024, 512
FFN_TN = 256
FFN_ROW_CHUNKS = 8
FFN_PAD = 8
FFN_OUT_ROWS = 256
FFN_VMEM_LIMIT_BYTES = 60 * 1024 * 1024


def _params(*sem):
    return pltpu.CompilerParams(dimension_semantics=sem, vmem_limit_bytes=VMEM_LIMIT_BYTES)


def _rms(x):
    return x * lax.rsqrt(jnp.mean(x * x, axis=-1, keepdims=True) + EPS)


def _in_proj_kernel(x_ref, g_ref, w_ref, o_ref, u_scr):
    @pl.when(pl.program_id(1) == 0)
    def _():
        u_scr[...] = (_rms(x_ref[...]) * g_ref[...]).astype(BF16)

    o_ref[...] = jnp.dot(u_scr[...], w_ref[...].astype(BF16),
                         preferred_element_type=F32).astype(o_ref.dtype)


def _in_proj(x2, g, w):
    t, d = x2.shape
    n = w.shape[1]
    return pl.pallas_call(
        _in_proj_kernel,
        grid=(t // IN_TM, n // IN_TN),
        in_specs=[
            pl.BlockSpec((IN_TM, d), lambda m, j: (m, 0)),
            pl.BlockSpec((1, d), lambda m, j: (0, 0)),
            pl.BlockSpec((d, IN_TN), lambda m, j: (0, j)),
        ],
        out_specs=pl.BlockSpec((IN_TM, IN_TN), lambda m, j: (m, j)),
        out_shape=jax.ShapeDtypeStruct((t, n), BF16),
        scratch_shapes=[pltpu.VMEM((IN_TM, d), BF16)],
        compiler_params=_params("parallel", "arbitrary"),
        name="in_proj",
    )(x2, g, w)


def _rope(x, cos, sin_lo, sin_hi):
    return (x * cos
            + pltpu.roll(x, HEAD_DIM - ROPE_PAIRS, axis=1) * sin_lo
            + pltpu.roll(x, ROPE_PAIRS, axis=1) * sin_hi)


def _attn_kernel(q_ref, k_ref, v_ref, cq_ref, slq_ref, shq_ref, ck_ref, slk_ref, shk_ref,
                 qg_ref, kg_ref, o_ref, k_scr):
    @pl.when(pl.program_id(2) == 0)
    def _():
        k = _rms(k_ref[...].astype(F32)) * kg_ref[...]
        k_scr[...] = _rope(k, ck_ref[...], slk_ref[...], shk_ref[...]).astype(BF16)

    scale = 1.0 / math.sqrt(HEAD_DIM)
    cos, sin_lo, sin_hi = cq_ref[...], slq_ref[...], shq_ref[...]
    kk = k_scr[...]
    vv = v_ref[...]
    for h in range(Q_PER_KV):
        sl = slice(h * HEAD_DIM, (h + 1) * HEAD_DIM)
        q = _rms(q_ref[:, sl].astype(F32)) * qg_ref[...]
        q = (_rope(q, cos, sin_lo, sin_hi) * scale).astype(BF16)
        s = lax.dot_general(q, kk, (((1,), (1,)), ((), ())), preferred_element_type=F32)
        p = jnp.exp(s - jnp.max(s, axis=-1, keepdims=True))
        l = jnp.sum(p, axis=-1, keepdims=True)
        o = jnp.dot(p.astype(BF16), vv, preferred_element_type=F32)
        o_ref[:, sl] = (o / l).astype(o_ref.dtype)


def _attention(proj, tabs, qg, kg, batch, seq):
    t = proj.shape[0]
    nq = seq // ATTN_TQ
    gw = Q_PER_KV * HEAD_DIM
    k_blk0 = ATTN_WIDTH // HEAD_DIM
    v_blk0 = (ATTN_WIDTH + KV_WIDTH) // HEAD_DIM
    cos, sin_lo, sin_hi = tabs
    q_tab = pl.BlockSpec((ATTN_TQ, HEAD_DIM), lambda b, g, i: (i, 0))
    k_tab = pl.BlockSpec((seq, HEAD_DIM), lambda b, g, i: (0, 0))
    gain = pl.BlockSpec((1, HEAD_DIM), lambda b, g, i: (0, 0))
    return pl.pallas_call(
        _attn_kernel,
        grid=(batch, N_KV_HEADS, nq),
        in_specs=[
            pl.BlockSpec((ATTN_TQ, gw), lambda b, g, i: (b * nq + i, g)),
            pl.BlockSpec((seq, HEAD_DIM), lambda b, g, i: (b, k_blk0 + g)),
            pl.BlockSpec((seq, HEAD_DIM), lambda b, g, i: (b, v_blk0 + g)),
            q_tab, q_tab, q_tab, k_tab, k_tab, k_tab, gain, gain,
        ],
        out_specs=pl.BlockSpec((ATTN_TQ, gw), lambda b, g, i: (b * nq + i, g)),
        out_shape=jax.ShapeDtypeStruct((t, ATTN_WIDTH), BF16),
        scratch_shapes=[pltpu.VMEM((seq, HEAD_DIM), BF16)],
        compiler_params=_params("parallel", "parallel", "arbitrary"),
        name="attention",
    )(proj, proj, proj, cos, sin_lo, sin_hi, cos, sin_lo, sin_hi, qg, kg)


def _fourier_fold_kernel(w_ref, cc_ref, sc_ref, o_ref):
    for g in range(FOURIER_GROUPS):
        w = w_ref[g]
        a = jnp.dot(cc_ref[...], w, preferred_element_type=F32, precision=lax.Precision.HIGHEST)
        b = jnp.dot(sc_ref[...], w, preferred_element_type=F32, precision=lax.Precision.HIGHEST)
        o_ref[g, :, :FOURIER_GROUP_DIM] = a.astype(o_ref.dtype)
        o_ref[g, :, FOURIER_GROUP_DIM:] = b.astype(o_ref.dtype)


def _fourier_fold(w_fmix, cc, sc):
    return pl.pallas_call(
        _fourier_fold_kernel,
        out_shape=jax.ShapeDtypeStruct((FOURIER_GROUPS, FOURIER_GROUP_DIM, 2 * FOURIER_GROUP_DIM), BF16),
        name="fourier_fold",
    )(w_fmix, cc, sc)


def _fourier_kernel(flo_ref, fhi_ref, ab_ref, c_ref, ms_ref, o_ref, za_scr, zb_scr):
    @pl.when(pl.program_id(1) == 0)
    def _():
        half = FOURIER_GROUPS // 2
        for g in range(FOURIER_GROUPS):
            src = flo_ref if g < half else fhi_ref
            lo = (g % half) * FOURIER_GROUP_DIM
            z = jnp.dot(src[:, lo:lo + FOURIER_GROUP_DIM], ab_ref[g], preferred_element_type=F32)
            dst = slice(g * FOURIER_GROUP_DIM, (g + 1) * FOURIER_GROUP_DIM)
            za_scr[:, dst] = z[:, :FOURIER_GROUP_DIM].astype(BF16)
            zb_scr[:, dst] = z[:, FOURIER_GROUP_DIM:].astype(BF16)

    acc = jnp.dot(c_ref[...], za_scr[...], preferred_element_type=F32)
    acc = acc + jnp.dot(ms_ref[...], zb_scr[...], preferred_element_type=F32)
    o_ref[...] = acc.astype(o_ref.dtype)


def _fourier(proj, ab, dft_c, dft_ms, batch, seq):
    t = proj.shape[0]
    nm = seq // FOURIER_TM
    half_w = FOURIER_WIDTH // 2
    f_blk0 = (ATTN_WIDTH + 2 * KV_WIDTH) // half_w
    return pl.pallas_call(
        _fourier_kernel,
        grid=(batch, nm),
        in_specs=[
            pl.BlockSpec((seq, half_w), lambda b, m: (b, f_blk0)),
            pl.BlockSpec((seq, half_w), lambda b, m: (b, f_blk0 + 1)),
            pl.BlockSpec(ab.shape, lambda b, m: (0, 0, 0)),
            pl.BlockSpec((FOURIER_TM, seq), lambda b, m: (m, 0)),
            pl.BlockSpec((FOURIER_TM, seq), lambda b, m: (m, 0)),
        ],
        out_specs=pl.BlockSpec((FOURIER_TM, FOURIER_WIDTH), lambda b, m: (b * nm + m, 0)),
        out_shape=jax.ShapeDtypeStruct((t, FOURIER_WIDTH), BF16),
        scratch_shapes=[pltpu.VMEM((seq, FOURIER_WIDTH), BF16), pltpu.VMEM((seq, FOURIER_WIDTH), BF16)],
        compiler_params=_params("parallel", "arbitrary"),
        name="fourier",
    )(proj, proj, ab, dft_c, dft_ms)


def _out_proj_kernel(a_ref, f_ref, x_ref, ag_ref, fg_ref, w_ref, g2_ref, h_ref, u_ref, mix_scr, h_scr):
    j = pl.program_id(1)

    @pl.when(j == 0)
    def _():
        mix_scr[:, :ATTN_WIDTH] = (_rms(a_ref[...].astype(F32)) * ag_ref[...]).astype(BF16)
        mix_scr[:, ATTN_WIDTH:] = (_rms(f_ref[...].astype(F32)) * fg_ref[...]).astype(BF16)

    h = x_ref[...] + jnp.dot(mix_scr[...], w_ref[...].astype(BF16), preferred_element_type=F32)
    h_ref[...] = h
    h_scr[:, pl.ds(pl.multiple_of(j * OUT_TN, OUT_TN), OUT_TN)] = h

    @pl.when(j == pl.num_programs(1) - 1)
    def _():
        u_ref[...] = (_rms(h_scr[...]) * g2_ref[...]).astype(u_ref.dtype)


def _out_proj(a, f, x2, ag, fg, w, g2):
    t, d = x2.shape
    row = lambda width: pl.BlockSpec((OUT_TM, width), lambda m, j: (m, 0))
    vec = lambda width: pl.BlockSpec((1, width), lambda m, j: (0, 0))
    tile = pl.BlockSpec((OUT_TM, OUT_TN), lambda m, j: (m, j))
    return pl.pallas_call(
        _out_proj_kernel,
        grid=(t // OUT_TM, d // OUT_TN),
        in_specs=[
            row(ATTN_WIDTH), row(FOURIER_WIDTH), tile, vec(ATTN_WIDTH), vec(FOURIER_WIDTH),
            pl.BlockSpec((w.shape[0], OUT_TN), lambda m, j: (0, j)),
            vec(d),
        ],
        out_specs=[tile, row(d)],
        out_shape=[jax.ShapeDtypeStruct((t, d), F32), jax.ShapeDtypeStruct((t, d), BF16)],
        scratch_shapes=[pltpu.VMEM((OUT_TM, w.shape[0]), BF16), pltpu.VMEM((OUT_TM, d), F32)],
        compiler_params=_params("parallel", "arbitrary"),
        name="out_proj",
    )(a, f, x2, ag, fg, w, g2)


def _ffn_kernel(u_ref, wg_ref, wv_ref, cwg_ref, cwv_ref, cbg_ref, cbv_ref, wd_ref, h_ref, g_ref,
                y_ref, acc_ref, gate_scr, val_scr, *, n_ff):
    j = pl.program_id(1)
    seq = u_ref.shape[0]
    rc = seq // FFN_ROW_CHUNKS

    @pl.when((pl.program_id(0) == 0) & (j == 0))
    def _():
        acc_ref[...] = jnp.zeros_like(acc_ref)
        gate_scr[...] = jnp.zeros_like(gate_scr)
        val_scr[...] = jnp.zeros_like(val_scr)

    @pl.when(j < n_ff)
    def _():
        wg = wg_ref[...].astype(BF16)
        wv = wv_ref[...].astype(BF16)
        wd = wd_ref[...].astype(BF16)
        cwg, cwv, cbg, cbv = cwg_ref[...], cwv_ref[...], cbg_ref[...], cbv_ref[...]

        def conv3(scr, cw, cb, r0):
            prev = scr[pl.ds(FFN_PAD - 1 + r0, rc), :]
            cur = scr[pl.ds(FFN_PAD + r0, rc), :]
            nxt = scr[pl.ds(FFN_PAD + 1 + r0, rc), :]
            return prev * cw[0:1, :] + cur * cw[1:2, :] + nxt * cw[2:3, :] + cb

        def gate_and_down(c):
            r0 = c * rc
            gate = conv3(gate_scr, cwg, cbg, r0)
            val = conv3(val_scr, cwv, cbv, r0)
            act = (gate * jax.nn.sigmoid(gate) * val).astype(BF16)
            acc_ref[pl.ds(r0, rc), :] += jnp.dot(act, wd, preferred_element_type=F32)

        for c in range(FFN_ROW_CHUNKS):
            r0 = c * rc
            u = u_ref[pl.ds(r0, rc), :]
            gate_scr[pl.ds(FFN_PAD + r0, rc), :] = jnp.dot(u, wg, preferred_element_type=F32)
            val_scr[pl.ds(FFN_PAD + r0, rc), :] = jnp.dot(u, wv, preferred_element_type=F32)
            if c >= 1:
                gate_and_down(c - 1)
        gate_and_down(FFN_ROW_CHUNKS - 1)

    @pl.when(j >= n_ff)
    def _():
        rows = pl.ds(pl.multiple_of((j - n_ff) * FFN_OUT_ROWS, FFN_OUT_ROWS), FFN_OUT_ROWS)
        y_ref[...] = _rms(acc_ref[rows, :] + h_ref[...]) * g_ref[...]
        acc_ref[rows, :] = jnp.zeros((FFN_OUT_ROWS, acc_ref.shape[1]), F32)


def _ffn(u2, w_up, conv_w, conv_b, w_down, h1, g, batch, seq):
    t, d = u2.shape
    d_ff = w_down.shape[0]
    n_ff = d_ff // FFN_TN
    n_out = seq // FFN_OUT_ROWS
    taps = conv_w.shape[0]
    ff = lambda j: jnp.minimum(j, n_ff - 1)
    gate_col = lambda b, j: (0, ff(j))
    val_col = lambda b, j: (0, n_ff + ff(j))
    out_row = lambda b, j: (b * n_out + jnp.maximum(j - n_ff, 0), 0)
    return pl.pallas_call(
        functools.partial(_ffn_kernel, n_ff=n_ff),
        grid=(batch, n_ff + n_out),
        in_specs=[
            pl.BlockSpec((seq, d), lambda b, j: (b, 0), pipeline_mode=pl.Buffered(1)),
            pl.BlockSpec((d, FFN_TN), gate_col),
            pl.BlockSpec((d, FFN_TN), val_col),
            pl.BlockSpec((taps, FFN_TN), gate_col),
            pl.BlockSpec((taps, FFN_TN), val_col),
            pl.BlockSpec((1, FFN_TN), gate_col),
            pl.BlockSpec((1, FFN_TN), val_col),
            pl.BlockSpec((FFN_TN, d), lambda b, j: (ff(j), 0)),
            pl.BlockSpec((FFN_OUT_ROWS, d), out_row),
            pl.BlockSpec((1, d), lambda b, j: (0, 0)),
        ],
        out_specs=pl.BlockSpec((FFN_OUT_ROWS, d), out_row),
        out_shape=jax.ShapeDtypeStruct((t, d), F32),
        scratch_shapes=[
            pltpu.VMEM((seq, d), F32),
            pltpu.VMEM((seq + 2 * FFN_PAD, FFN_TN), F32),
            pltpu.VMEM((seq + 2 * FFN_PAD, FFN_TN), F32),
        ],
        compiler_params=pltpu.CompilerParams(dimension_semantics=("arbitrary", "arbitrary"),
                                             vmem_limit_bytes=FFN_VMEM_LIMIT_BYTES),
        name="ffn",
    )(u2, w_up, w_up, conv_w, conv_w, conv_b, conv_b, w_down, h1, g)


def _rope_tables(seq):
    t = jnp.arange(seq, dtype=jnp.int32)
    inv_freq = ROPE_THETA ** (-jnp.arange(ROPE_PAIRS, dtype=F32) / ROPE_PAIRS)

    def axis_tables(pos):
        ang = pos.astype(F32)[:, None] * inv_freq[None, :]
        ang = jnp.concatenate([ang, ang], axis=-1)
        return jnp.cos(ang), jnp.sin(ang)

    cos_r, sin_r = axis_tables(t // GRID_W)
    cos_c, sin_c = axis_tables(t % GRID_W)
    cos = jnp.concatenate([cos_r, cos_c], axis=-1)
    sin = jnp.concatenate([sin_r, sin_c], axis=-1)
    first_half = (jnp.arange(HEAD_DIM) % (2 * ROPE_PAIRS)) < ROPE_PAIRS
    sin_lo = jnp.where(first_half[None, :], -sin, 0.0)
    sin_hi = jnp.where(first_half[None, :], 0.0, sin)
    return cos, sin_lo, sin_hi


def _dft_cos_sin(rows, n_cols, n, scale=1.0):
    k = jnp.arange(n_cols, dtype=jnp.int32)
    ang = ((rows[:, None] * k[None, :]) % n).astype(F32) * (2.0 * math.pi / n)
    return jnp.cos(ang) * scale, jnp.sin(ang) * scale


def _dft_tables(n, scale):
    return _dft_cos_sin(jnp.arange(n, dtype=jnp.int32), n, n, scale)


def _seq_dft_kernel(ca_ref, sa_ref, cr_ref, sr_ref, c_ref, ms_ref):
    cr, sr = cr_ref[...], sr_ref[...]
    for i in range(DFT_COARSE_PER_STEP):
        ca, sa = ca_ref[i:i + 1, :], sa_ref[i:i + 1, :]
        rows = slice(i * DFT_SUB, (i + 1) * DFT_SUB)
        c_ref[rows, :] = (ca * cr - sa * sr).astype(c_ref.dtype)
        ms_ref[rows, :] = (-(sa * cr + ca * sr)).astype(ms_ref.dtype)


def _seq_dft_tables(n):
    coarse = jnp.arange(n // DFT_SUB, dtype=jnp.int32) * DFT_SUB
    ca, sa = _dft_cos_sin(coarse, n, n)
    cr, sr = _dft_cos_sin(jnp.arange(DFT_SUB, dtype=jnp.int32), n, n)
    steps = n // (DFT_SUB * DFT_COARSE_PER_STEP)
    coarse_spec = pl.BlockSpec((DFT_COARSE_PER_STEP, n), lambda i: (i, 0))
    fine_spec = pl.BlockSpec((DFT_SUB, n), lambda i: (0, 0))
    out_spec = pl.BlockSpec((DFT_SUB * DFT_COARSE_PER_STEP, n), lambda i: (i, 0))
    return pl.pallas_call(
        _seq_dft_kernel,
        grid=(steps,),
        in_specs=[coarse_spec, coarse_spec, fine_spec, fine_spec],
        out_specs=[out_spec, out_spec],
        out_shape=[jax.ShapeDtypeStruct((n, n), BF16)] * 2,
        compiler_params=_params("parallel"),
        name="seq_dft_tables",
    )(ca, sa, cr, sr)


def kernel(x, norm1_g, w_in, q_norm_g, k_norm_g, w_fmix, attn_out_g, fourier_out_g, w_out,
           norm2_g, w_up, conv_w, conv_b, w_down, final_g):
    batch, seq, d = x.shape
    depth = w_in.shape[0]
    rope = _rope_tables(seq)
    dft_c, dft_ms = _seq_dft_tables(seq)
    cc, sc = _dft_tables(FOURIER_GROUP_DIM, 1.0 / math.sqrt(seq * FOURIER_GROUP_DIM))

    assert depth == 1
    h = x.reshape(batch * seq, d)
    for l in range(depth):
        proj = _in_proj(h, norm1_g[l][None], w_in[l])
        attn = _attention(proj, rope, q_norm_g[l][None], k_norm_g[l][None], batch, seq)
        ab = _fourier_fold(w_fmix[l], cc, sc)
        four = _fourier(proj, ab, dft_c, dft_ms, batch, seq)
        h1, u2 = _out_proj(attn, four, h, attn_out_g[l][None], fourier_out_g[l][None],
                           w_out[l], norm2_g[l][None])
        h = _ffn(u2, w_up[l], conv_w[l], conv_b[l][None], w_down[l], h1, final_g[None], batch, seq)
    return h.reshape(batch, seq, d)
```

```python
import functools
import math

import jax
import jax.numpy as jnp
from jax import lax
from jax.experimental import pallas as pl
from jax.experimental.pallas import tpu as pltpu

F32 = jnp.float32
BF16 = jnp.bfloat16

HEAD_DIM = 128
N_Q_HEADS = 8
N_KV_HEADS = 2
Q_PER_KV = N_Q_HEADS // N_KV_HEADS
ATTN_WIDTH = N_Q_HEADS * HEAD_DIM
KV_WIDTH = N_KV_HEADS * HEAD_DIM
FOURIER_GROUPS = 8
FOURIER_GROUP_DIM = 128
FOURIER_WIDTH = FOURIER_GROUPS * FOURIER_GROUP_DIM
GRID_W = 64
ROPE_THETA = 10000.0
ROPE_PAIRS = HEAD_DIM // 4
EPS = 1e-6

VMEM_LIMIT_BYTES = 56 * 1024 * 1024

IN_TM, IN_TN = 1024, 512
ATTN_TQ = 256
FOURIER_TM = 512
DFT_SUB, DFT_COARSE_PER_STEP = 64, 8
OUT_TM, OUT_TN = 1024, 512
FFN_TN = 256
FFN_ROW_CHUNKS = 2
FFN_PAD = 8
FFN_OUT_ROWS = 256
FFN_VMEM_LIMIT_BYTES = 60 * 1024 * 1024


def _params(*sem):
    return pltpu.CompilerParams(dimension_semantics=sem, vmem_limit_bytes=VMEM_LIMIT_BYTES)


def _rms(x):
    return x * lax.rsqrt(jnp.mean(x * x, axis=-1, keepdims=True) + EPS)


def _in_proj_kernel(x_ref, g_ref, w_ref, o_ref, u_scr):
    @pl.when(pl.program_id(1) == 0)
    def _():
        u_scr[...] = (_rms(x_ref[...]) * g_ref[...]).astype(BF16)

    o_ref[...] = jnp.dot(u_scr[...], w_ref[...].astype(BF16),
                         preferred_element_type=F32).astype(o_ref.dtype)


def _in_proj(x2, g, w):
    t, d = x2.shape
    n = w.shape[1]
    return pl.pallas_call(
        _in_proj_kernel,
        grid=(t // IN_TM, n // IN_TN),
        in_specs=[
            pl.BlockSpec((IN_TM, d), lambda m, j: (m, 0)),
            pl.BlockSpec((1, d), lambda m, j: (0, 0)),
            pl.BlockSpec((d, IN_TN), lambda m, j: (0, j)),
        ],
        out_specs=pl.BlockSpec((IN_TM, IN_TN), lambda m, j: (m, j)),
        out_shape=jax.ShapeDtypeStruct((t, n), BF16),
        scratch_shapes=[pltpu.VMEM((IN_TM, d), BF16)],
        compiler_params=_params("parallel", "arbitrary"),
        name="in_proj",
    )(x2, g, w)


def _rope(x, cos, sin_lo, sin_hi):
    return (x * cos
            + pltpu.roll(x, HEAD_DIM - ROPE_PAIRS, axis=1) * sin_lo
            + pltpu.roll(x, ROPE_PAIRS, axis=1) * sin_hi)


def _attn_kernel(q_ref, k_ref, v_ref, cq_ref, slq_ref, shq_ref, ck_ref, slk_ref, shk_ref,
                 qg_ref, kg_ref, o_ref, k_scr):
    @pl.when(pl.program_id(2) == 0)
    def _():
        k = _rms(k_ref[...].astype(F32)) * kg_ref[...]
        k_scr[...] = _rope(k, ck_ref[...], slk_ref[...], shk_ref[...]).astype(BF16)

    scale = 1.0 / math.sqrt(HEAD_DIM)
    cos, sin_lo, sin_hi = cq_ref[...], slq_ref[...], shq_ref[...]
    kk = k_scr[...]
    vv = v_ref[...]
    for h in range(Q_PER_KV):
        sl = slice(h * HEAD_DIM, (h + 1) * HEAD_DIM)
        q = _rms(q_ref[:, sl].astype(F32)) * qg_ref[...]
        q = (_rope(q, cos, sin_lo, sin_hi) * scale).astype(BF16)
        s = lax.dot_general(q, kk, (((1,), (1,)), ((), ())), preferred_element_type=F32)
        p = jnp.exp(s - jnp.max(s, axis=-1, keepdims=True))
        l = jnp.sum(p, axis=-1, keepdims=True)
        o = jnp.dot(p.astype(BF16), vv, preferred_element_type=F32)
        o_ref[:, sl] = (o / l).astype(o_ref.dtype)


def _attention(proj, tabs, qg, kg, batch, seq):
    t = proj.shape[0]
    nq = seq // ATTN_TQ
    gw = Q_PER_KV * HEAD_DIM
    k_blk0 = ATTN_WIDTH // HEAD_DIM
    v_blk0 = (ATTN_WIDTH + KV_WIDTH) // HEAD_DIM
    cos, sin_lo, sin_hi = tabs
    q_tab = pl.BlockSpec((ATTN_TQ, HEAD_DIM), lambda b, g, i: (i, 0))
    k_tab = pl.BlockSpec((seq, HEAD_DIM), lambda b, g, i: (0, 0))
    gain = pl.BlockSpec((1, HEAD_DIM), lambda b, g, i: (0, 0))
    return pl.pallas_call(
        _attn_kernel,
        grid=(batch, N_KV_HEADS, nq),
        in_specs=[
            pl.BlockSpec((ATTN_TQ, gw), lambda b, g, i: (b * nq + i, g)),
            pl.BlockSpec((seq, HEAD_DIM), lambda b, g, i: (b, k_blk0 + g)),
            pl.BlockSpec((seq, HEAD_DIM), lambda b, g, i: (b, v_blk0 + g)),
            q_tab, q_tab, q_tab, k_tab, k_tab, k_tab, gain, gain,
        ],
        out_specs=pl.BlockSpec((ATTN_TQ, gw), lambda b, g, i: (b * nq + i, g)),
        out_shape=jax.ShapeDtypeStruct((t, ATTN_WIDTH), BF16),
        scratch_shapes=[pltpu.VMEM((seq, HEAD_DIM), BF16)],
        compiler_params=_params("parallel", "parallel", "arbitrary"),
        name="attention",
    )(proj, proj, proj, cos, sin_lo, sin_hi, cos, sin_lo, sin_hi, qg, kg)


def _fourier_fold_kernel(w_ref, cc_ref, sc_ref, o_ref):
    for g in range(FOURIER_GROUPS):
        w = w_ref[g]
        a = jnp.dot(cc_ref[...], w, preferred_element_type=F32, precision=lax.Precision.HIGHEST)
        b = jnp.dot(sc_ref[...], w, preferred_element_type=F32, precision=lax.Precision.HIGHEST)
        o_ref[g, :, :FOURIER_GROUP_DIM] = a.astype(o_ref.dtype)
        o_ref[g, :, FOURIER_GROUP_DIM:] = b.astype(o_ref.dtype)


def _fourier_fold(w_fmix, cc, sc):
    return pl.pallas_call(
        _fourier_fold_kernel,
        out_shape=jax.ShapeDtypeStruct((FOURIER_GROUPS, FOURIER_GROUP_DIM, 2 * FOURIER_GROUP_DIM), BF16),
        name="fourier_fold",
    )(w_fmix, cc, sc)


def _fourier_kernel(flo_ref, fhi_ref, ab_ref, c_ref, ms_ref, o_ref, za_scr, zb_scr):
    @pl.when(pl.program_id(1) == 0)
    def _():
        half = FOURIER_GROUPS // 2
        for g in range(FOURIER_GROUPS):
            src = flo_ref if g < half else fhi_ref
            lo = (g % half) * FOURIER_GROUP_DIM
            z = jnp.dot(src[:, lo:lo + FOURIER_GROUP_DIM], ab_ref[g], preferred_element_type=F32)
            dst = slice(g * FOURIER_GROUP_DIM, (g + 1) * FOURIER_GROUP_DIM)
            za_scr[:, dst] = z[:, :FOURIER_GROUP_DIM].astype(BF16)
            zb_scr[:, dst] = z[:, FOURIER_GROUP_DIM:].astype(BF16)

    acc = jnp.dot(c_ref[...], za_scr[...], preferred_element_type=F32)
    acc = acc + jnp.dot(ms_ref[...], zb_scr[...], preferred_element_type=F32)
    o_ref[...] = acc.astype(o_ref.dtype)


def _fourier(proj, ab, dft_c, dft_ms, batch, seq):
    t = proj.shape[0]
    nm = seq // FOURIER_TM
    half_w = FOURIER_WIDTH // 2
    f_blk0 = (ATTN_WIDTH + 2 * KV_WIDTH) // half_w
    return pl.pallas_call(
        _fourier_kernel,
        grid=(batch, nm),
        in_specs=[
            pl.BlockSpec((seq, half_w), lambda b, m: (b, f_blk0)),
            pl.BlockSpec((seq, half_w), lambda b, m: (b, f_blk0 + 1)),
            pl.BlockSpec(ab.shape, lambda b, m: (0, 0, 0)),
            pl.BlockSpec((FOURIER_TM, seq), lambda b, m: (m, 0)),
            pl.BlockSpec((FOURIER_TM, seq), lambda b, m: (m, 0)),
        ],
        out_specs=pl.BlockSpec((FOURIER_TM, FOURIER_WIDTH), lambda b, m: (b * nm + m, 0)),
        out_shape=jax.ShapeDtypeStruct((t, FOURIER_WIDTH), BF16),
        scratch_shapes=[pltpu.VMEM((seq, FOURIER_WIDTH), BF16), pltpu.VMEM((seq, FOURIER_WIDTH), BF16)],
        compiler_params=_params("parallel", "arbitrary"),
        name="fourier",
    )(proj, proj, ab, dft_c, dft_ms)


def _out_proj_kernel(a_ref, f_ref, x_ref, ag_ref, fg_ref, w_ref, g2_ref, h_ref, u_ref, mix_scr, h_scr):
    j = pl.program_id(1)

    @pl.when(j == 0)
    def _():
        mix_scr[:, :ATTN_WIDTH] = (_rms(a_ref[...].astype(F32)) * ag_ref[...]).astype(BF16)
        mix_scr[:, ATTN_WIDTH:] = (_rms(f_ref[...].astype(F32)) * fg_ref[...]).astype(BF16)

    h = x_ref[...] + jnp.dot(mix_scr[...], w_ref[...].astype(BF16), preferred_element_type=F32)
    h_ref[...] = h
    h_scr[:, pl.ds(pl.multiple_of(j * OUT_TN, OUT_TN), OUT_TN)] = h

    @pl.when(j == pl.num_programs(1) - 1)
    def _():
        u_ref[...] = (_rms(h_scr[...]) * g2_ref[...]).astype(u_ref.dtype)


def _out_proj(a, f, x2, ag, fg, w, g2):
    t, d = x2.shape
    row = lambda width: pl.BlockSpec((OUT_TM, width), lambda m, j: (m, 0))
    vec = lambda width: pl.BlockSpec((1, width), lambda m, j: (0, 0))
    tile = pl.BlockSpec((OUT_TM, OUT_TN), lambda m, j: (m, j))
    return pl.pallas_call(
        _out_proj_kernel,
        grid=(t // OUT_TM, d // OUT_TN),
        in_specs=[
            row(ATTN_WIDTH), row(FOURIER_WIDTH), tile, vec(ATTN_WIDTH), vec(FOURIER_WIDTH),
            pl.BlockSpec((w.shape[0], OUT_TN), lambda m, j: (0, j)),
            vec(d),
        ],
        out_specs=[tile, row(d)],
        out_shape=[jax.ShapeDtypeStruct((t, d), F32), jax.ShapeDtypeStruct((t, d), BF16)],
        scratch_shapes=[pltpu.VMEM((OUT_TM, w.shape[0]), BF16), pltpu.VMEM((OUT_TM, d), F32)],
        compiler_params=_params("parallel", "arbitrary"),
        name="out_proj",
    )(a, f, x2, ag, fg, w, g2)


def _ffn_kernel(u_ref, wg_ref, wv_ref, cwg_ref, cwv_ref, cbg_ref, cbv_ref, wd_ref, h_ref, g_ref,
                y_ref, acc_ref, gate_scr, val_scr, *, n_ff):
    j = pl.program_id(1)
    seq = u_ref.shape[0]
    rc = seq // FFN_ROW_CHUNKS

    @pl.when((pl.program_id(0) == 0) & (j == 0))
    def _():
        acc_ref[...] = jnp.zeros_like(acc_ref)
        gate_scr[...] = jnp.zeros_like(gate_scr)
        val_scr[...] = jnp.zeros_like(val_scr)

    @pl.when(j < n_ff)
    def _():
        wg = wg_ref[...].astype(BF16)
        wv = wv_ref[...].astype(BF16)
        wd = wd_ref[...].astype(BF16)
        cwg, cwv, cbg, cbv = cwg_ref[...], cwv_ref[...], cbg_ref[...], cbv_ref[...]

        def conv3(scr, cw, cb, r0):
            prev = scr[pl.ds(FFN_PAD - 1 + r0, rc), :]
            cur = scr[pl.ds(FFN_PAD + r0, rc), :]
            nxt = scr[pl.ds(FFN_PAD + 1 + r0, rc), :]
            return prev * cw[0:1, :] + cur * cw[1:2, :] + nxt * cw[2:3, :] + cb

        def gate_and_down(c):
            r0 = c * rc
            gate = conv3(gate_scr, cwg, cbg, r0)
            val = conv3(val_scr, cwv, cbv, r0)
            act = (gate * jax.nn.sigmoid(gate) * val).astype(BF16)
            acc_ref[pl.ds(r0, rc), :] += jnp.dot(act, wd, preferred_element_type=F32)

        for c in range(FFN_ROW_CHUNKS):
            r0 = c * rc
            u = u_ref[pl.ds(r0, rc), :]
            gate_scr[pl.ds(FFN_PAD + r0, rc), :] = jnp.dot(u, wg, preferred_element_type=F32)
            val_scr[pl.ds(FFN_PAD + r0, rc), :] = jnp.dot(u, wv, preferred_element_type=F32)
            if c >= 1:
                gate_and_down(c - 1)
        gate_and_down(FFN_ROW_CHUNKS - 1)

    @pl.when(j >= n_ff)
    def _():
        rows = pl.ds(pl.multiple_of((j - n_ff) * FFN_OUT_ROWS, FFN_OUT_ROWS), FFN_OUT_ROWS)
        y_ref[...] = _rms(acc_ref[rows, :] + h_ref[...]) * g_ref[...]
        acc_ref[rows, :] = jnp.zeros((FFN_OUT_ROWS, acc_ref.shape[1]), F32)


def _ffn(u2, w_up, conv_w, conv_b, w_down, h1, g, batch, seq):
    t, d = u2.shape
    d_ff = w_down.shape[0]
    n_ff = d_ff // FFN_TN
    n_out = seq // FFN_OUT_ROWS
    taps = conv_w.shape[0]
    ff = lambda j: jnp.minimum(j, n_ff - 1)
    gate_col = lambda b, j: (0, ff(j))
    val_col = lambda b, j: (0, n_ff + ff(j))
    out_row = lambda b, j: (b * n_out + jnp.maximum(j - n_ff, 0), 0)
    return pl.pallas_call(
        functools.partial(_ffn_kernel, n_ff=n_ff),
        grid=(batch, n_ff + n_out),
        in_specs=[
            pl.BlockSpec((seq, d), lambda b, j: (b, 0), pipeline_mode=pl.Buffered(1)),
            pl.BlockSpec((d, FFN_TN), gate_col),
            pl.BlockSpec((d, FFN_TN), val_col),
            pl.BlockSpec((taps, FFN_TN), gate_col),
            pl.BlockSpec((taps, FFN_TN), val_col),
            pl.BlockSpec((1, FFN_TN), gate_col),
            pl.BlockSpec((1, FFN_TN), val_col),
            pl.BlockSpec((FFN_TN, d), lambda b, j: (ff(j), 0)),
            pl.BlockSpec((FFN_OUT_ROWS, d), out_row),
            pl.BlockSpec((1, d), lambda b, j: (0, 0)),
        ],
        out_specs=pl.BlockSpec((FFN_OUT_ROWS, d), out_row),
        out_shape=jax.ShapeDtypeStruct((t, d), F32),
        scratch_shapes=[
            pltpu.VMEM((seq, d), F32),
            pltpu.VMEM((seq + 2 * FFN_PAD, FFN_TN), F32),
            pltpu.VMEM((seq + 2 * FFN_PAD, FFN_TN), F32),
        ],
        compiler_params=pltpu.CompilerParams(dimension_semantics=("arbitrary", "arbitrary"),
                                             vmem_limit_bytes=FFN_VMEM_LIMIT_BYTES),
        name="ffn",
    )(u2, w_up, w_up, conv_w, conv_w, conv_b, conv_b, w_down, h1, g)


def _rope_tables(seq):
    t = jnp.arange(seq, dtype=jnp.int32)
    inv_freq = ROPE_THETA ** (-jnp.arange(ROPE_PAIRS, dtype=F32) / ROPE_PAIRS)

    def axis_tables(pos):
        ang = pos.astype(F32)[:, None] * inv_freq[None, :]
        ang = jnp.concatenate([ang, ang], axis=-1)
        return jnp.cos(ang), jnp.sin(ang)

    cos_r, sin_r = axis_tables(t // GRID_W)
    cos_c, sin_c = axis_tables(t % GRID_W)
    cos = jnp.concatenate([cos_r, cos_c], axis=-1)
    sin = jnp.concatenate([sin_r, sin_c], axis=-1)
    first_half = (jnp.arange(HEAD_DIM) % (2 * ROPE_PAIRS)) < ROPE_PAIRS
    sin_lo = jnp.where(first_half[None, :], -sin, 0.0)
    sin_hi = jnp.where(first_half[None, :], 0.0, sin)
    return cos, sin_lo, sin_hi


def _dft_cos_sin(rows, n_cols, n, scale=1.0):
    k = jnp.arange(n_cols, dtype=jnp.int32)
    ang = ((rows[:, None] * k[None, :]) % n).astype(F32) * (2.0 * math.pi / n)
    return jnp.cos(ang) * scale, jnp.sin(ang) * scale


def _dft_tables(n, scale):
    return _dft_cos_sin(jnp.arange(n, dtype=jnp.int32), n, n, scale)


def _seq_dft_kernel(ca_ref, sa_ref, cr_ref, sr_ref, c_ref, ms_ref):
    cr, sr = cr_ref[...], sr_ref[...]
    for i in range(DFT_COARSE_PER_STEP):
        ca, sa = ca_ref[i:i + 1, :], sa_ref[i:i + 1, :]
        rows = slice(i * DFT_SUB, (i + 1) * DFT_SUB)
        c_ref[rows, :] = (ca * cr - sa * sr).astype(c_ref.dtype)
        ms_ref[rows, :] = (-(sa * cr + ca * sr)).astype(ms_ref.dtype)


def _seq_dft_tables(n):
    coarse = jnp.arange(n // DFT_SUB, dtype=jnp.int32) * DFT_SUB
    ca, sa = _dft_cos_sin(coarse, n, n)
    cr, sr = _dft_cos_sin(jnp.arange(DFT_SUB, dtype=jnp.int32), n, n)
    steps = n // (DFT_SUB * DFT_COARSE_PER_STEP)
    coarse_spec = pl.BlockSpec((DFT_COARSE_PER_STEP, n), lambda i: (i, 0))
    fine_spec = pl.BlockSpec((DFT_SUB, n), lambda i: (0, 0))
    out_spec = pl.BlockSpec((DFT_SUB * DFT_COARSE_PER_STEP, n), lambda i: (i, 0))
    return pl.pallas_call(
        _seq_dft_kernel,
        grid=(steps,),
        in_specs=[coarse_spec, coarse_spec, fine_spec, fine_spec],
        out_specs=[out_spec, out_spec],
        out_shape=[jax.ShapeDtypeStruct((n, n), BF16)] * 2,
        compiler_params=_params("parallel"),
        name="seq_dft_tables",
    )(ca, sa, cr, sr)


def kernel(x, norm1_g, w_in, q_norm_g, k_norm_g, w_fmix, attn_out_g, fourier_out_g, w_out,
           norm2_g, w_up, conv_w, conv_b, w_down, final_g):
    batch, seq, d = x.shape
    depth = w_in.shape[0]
    rope = _rope_tables(seq)
    dft_c, dft_ms = _seq_dft_tables(seq)
    cc, sc = _dft_tables(FOURIER_GROUP_DIM, 1.0 / math.sqrt(seq * FOURIER_GROUP_DIM))

    assert depth == 1
    h = x.reshape(batch * seq, d)
    for l in range(depth):
        proj = _in_proj(h, norm1_g[l][None], w_in[l])
        attn = _attention(proj, rope, q_norm_g[l][None], k_norm_g[l][None], batch, seq)
        ab = _fourier_fold(w_fmix[l], cc, sc)
        four = _fourier(proj, ab, dft_c, dft_ms, batch, seq)
        h1, u2 = _out_proj(attn, four, h, attn_out_g[l][None], fourier_out_g[l][None],
                           w_out[l], norm2_g[l][None])
        h = _ffn(u2, w_up[l], conv_w[l], conv_b[l][None], w_down[l], h1, final_g[None], batch, seq)
    return h.reshape(batch, seq, d)
```

```python
import functools
import math

import jax
import jax.numpy as jnp
from jax import lax
from jax.experimental import pallas as pl
from jax.experimental.pallas import tpu as pltpu

F32 = jnp.float32
BF16 = jnp.bfloat16

HEAD_DIM = 128
N_Q_HEADS = 8
N_KV_HEADS = 2
Q_PER_KV = N_Q_HEADS // N_KV_HEADS
ATTN_WIDTH = N_Q_HEADS * HEAD_DIM
KV_WIDTH = N_KV_HEADS * HEAD_DIM
FOURIER_GROUPS = 8
FOURIER_GROUP_DIM = 128
FOURIER_WIDTH = FOURIER_GROUPS * FOURIER_GROUP_DIM
GRID_W = 64
ROPE_THETA = 10000.0
ROPE_PAIRS = HEAD_DIM // 4
EPS = 1e-6

VMEM_LIMIT_BYTES = 56 * 1024 * 1024

IN_TM, IN_TN = 1024, 512
ATTN_TQ = 256
FOURIER_TM = 512
DFT_SUB, DFT_COARSE_PER_STEP = 64, 8
OUT_TM, OUT_TN = 1024, 512
FFN_TN = 256
FFN_ROW_CHUNKS = 2
FFN_PAD = 8
FFN_OUT_ROWS = 256
FFN_VMEM_LIMIT_BYTES = 60 * 1024 * 1024


def _params(*sem):
    return pltpu.CompilerParams(dimension_semantics=sem, vmem_limit_bytes=VMEM_LIMIT_BYTES)


def _rms(x):
    return x * lax.rsqrt(jnp.mean(x * x, axis=-1, keepdims=True) + EPS)


def _weight_tile_once(n_tiles):
    return lambda m, j: (0, jnp.where(m == 0, j, n_tiles - 1))


def _in_proj_kernel(x_ref, g_ref, w_ref, o_ref, u_scr, w_scr):
    m, j = pl.program_id(0), pl.program_id(1)

    @pl.when(j == 0)
    def _():
        u_scr[...] = (_rms(x_ref[...]) * g_ref[...]).astype(BF16)

    @pl.when(m == 0)
    def _():
        w_scr[j] = w_ref[...].astype(BF16)

    o_ref[...] = jnp.dot(u_scr[...], w_scr[j], preferred_element_type=F32).astype(o_ref.dtype)


def _in_proj(x2, g, w):
    t, d = x2.shape
    n = w.shape[1]
    n_tiles = n // IN_TN
    return pl.pallas_call(
        _in_proj_kernel,
        grid=(t // IN_TM, n_tiles),
        in_specs=[
            pl.BlockSpec((IN_TM, d), lambda m, j: (m, 0)),
            pl.BlockSpec((1, d), lambda m, j: (0, 0)),
            pl.BlockSpec((d, IN_TN), _weight_tile_once(n_tiles)),
        ],
        out_specs=pl.BlockSpec((IN_TM, IN_TN), lambda m, j: (m, j)),
        out_shape=jax.ShapeDtypeStruct((t, n), BF16),
        scratch_shapes=[pltpu.VMEM((IN_TM, d), BF16), pltpu.VMEM((n_tiles, d, IN_TN), BF16)],
        compiler_params=_params("arbitrary", "arbitrary"),
        name="in_proj",
    )(x2, g, w)


def _rope(x, cos, sin_lo, sin_hi):
    return (x * cos
            + pltpu.roll(x, HEAD_DIM - ROPE_PAIRS, axis=1) * sin_lo
            + pltpu.roll(x, ROPE_PAIRS, axis=1) * sin_hi)


def _attn_kernel(q_ref, k_ref, v_ref, cq_ref, slq_ref, shq_ref, ck_ref, slk_ref, shk_ref,
                 qg_ref, kg_ref, o_ref, k_scr):
    @pl.when(pl.program_id(2) == 0)
    def _():
        k = _rms(k_ref[...].astype(F32)) * kg_ref[...]
        k_scr[...] = _rope(k, ck_ref[...], slk_ref[...], shk_ref[...]).astype(BF16)

    scale = 1.0 / math.sqrt(HEAD_DIM)
    cos, sin_lo, sin_hi = cq_ref[...], slq_ref[...], shq_ref[...]
    kk = k_scr[...]
    vv = v_ref[...]
    for h in range(Q_PER_KV):
        sl = slice(h * HEAD_DIM, (h + 1) * HEAD_DIM)
        q = _rms(q_ref[:, sl].astype(F32)) * qg_ref[...]
        q = (_rope(q, cos, sin_lo, sin_hi) * scale).astype(BF16)
        s = lax.dot_general(q, kk, (((1,), (1,)), ((), ())), preferred_element_type=F32)
        p = jnp.exp(s - jnp.max(s, axis=-1, keepdims=True))
        l = jnp.sum(p, axis=-1, keepdims=True)
        o = jnp.dot(p.astype(BF16), vv, preferred_element_type=F32)
        o_ref[:, sl] = (o / l).astype(o_ref.dtype)


def _attention(proj, tabs, qg, kg, batch, seq):
    t = proj.shape[0]
    nq = seq // ATTN_TQ
    gw = Q_PER_KV * HEAD_DIM
    k_blk0 = ATTN_WIDTH // HEAD_DIM
    v_blk0 = (ATTN_WIDTH + KV_WIDTH) // HEAD_DIM
    cos, sin_lo, sin_hi = tabs
    q_tab = pl.BlockSpec((ATTN_TQ, HEAD_DIM), lambda b, g, i: (i, 0))
    k_tab = pl.BlockSpec((seq, HEAD_DIM), lambda b, g, i: (0, 0))
    gain = pl.BlockSpec((1, HEAD_DIM), lambda b, g, i: (0, 0))
    return pl.pallas_call(
        _attn_kernel,
        grid=(batch, N_KV_HEADS, nq),
        in_specs=[
            pl.BlockSpec((ATTN_TQ, gw), lambda b, g, i: (b * nq + i, g)),
            pl.BlockSpec((seq, HEAD_DIM), lambda b, g, i: (b, k_blk0 + g)),
            pl.BlockSpec((seq, HEAD_DIM), lambda b, g, i: (b, v_blk0 + g)),
            q_tab, q_tab, q_tab, k_tab, k_tab, k_tab, gain, gain,
        ],
        out_specs=pl.BlockSpec((ATTN_TQ, gw), lambda b, g, i: (b * nq + i, g)),
        out_shape=jax.ShapeDtypeStruct((t, ATTN_WIDTH), BF16),
        scratch_shapes=[pltpu.VMEM((seq, HEAD_DIM), BF16)],
        compiler_params=_params("parallel", "parallel", "arbitrary"),
        name="attention",
    )(proj, proj, proj, cos, sin_lo, sin_hi, cos, sin_lo, sin_hi, qg, kg)


def _fourier_fold_kernel(w_ref, cc_ref, sc_ref, o_ref):
    for g in range(FOURIER_GROUPS):
        w = w_ref[g]
        a = jnp.dot(cc_ref[...], w, preferred_element_type=F32, precision=lax.Precision.HIGHEST)
        b = jnp.dot(sc_ref[...], w, preferred_element_type=F32, precision=lax.Precision.HIGHEST)
        o_ref[g, :, :FOURIER_GROUP_DIM] = a.astype(o_ref.dtype)
        o_ref[g, :, FOURIER_GROUP_DIM:] = b.astype(o_ref.dtype)


def _fourier_fold(w_fmix, cc, sc):
    return pl.pallas_call(
        _fourier_fold_kernel,
        out_shape=jax.ShapeDtypeStruct((FOURIER_GROUPS, FOURIER_GROUP_DIM, 2 * FOURIER_GROUP_DIM), BF16),
        name="fourier_fold",
    )(w_fmix, cc, sc)


def _fourier_kernel(flo_ref, fhi_ref, ab_ref, c_ref, ms_ref, o_ref, za_scr, zb_scr):
    @pl.when(pl.program_id(1) == 0)
    def _():
        half = FOURIER_GROUPS // 2
        for g in range(FOURIER_GROUPS):
            src = flo_ref if g < half else fhi_ref
            lo = (g % half) * FOURIER_GROUP_DIM
            z = jnp.dot(src[:, lo:lo + FOURIER_GROUP_DIM], ab_ref[g], preferred_element_type=F32)
            dst = slice(g * FOURIER_GROUP_DIM, (g + 1) * FOURIER_GROUP_DIM)
            za_scr[:, dst] = z[:, :FOURIER_GROUP_DIM].astype(BF16)
            zb_scr[:, dst] = z[:, FOURIER_GROUP_DIM:].astype(BF16)

    acc = jnp.dot(c_ref[...], za_scr[...], preferred_element_type=F32)
    acc = acc + jnp.dot(ms_ref[...], zb_scr[...], preferred_element_type=F32)
    o_ref[...] = acc.astype(o_ref.dtype)


def _fourier(proj, ab, dft_c, dft_ms, batch, seq):
    t = proj.shape[0]
    nm = seq // FOURIER_TM
    half_w = FOURIER_WIDTH // 2
    f_blk0 = (ATTN_WIDTH + 2 * KV_WIDTH) // half_w
    return pl.pallas_call(
        _fourier_kernel,
        grid=(batch, nm),
        in_specs=[
            pl.BlockSpec((seq, half_w), lambda b, m: (b, f_blk0)),
            pl.BlockSpec((seq, half_w), lambda b, m: (b, f_blk0 + 1)),
            pl.BlockSpec(ab.shape, lambda b, m: (0, 0, 0)),
            pl.BlockSpec((FOURIER_TM, seq), lambda b, m: (m, 0)),
            pl.BlockSpec((FOURIER_TM, seq), lambda b, m: (m, 0)),
        ],
        out_specs=pl.BlockSpec((FOURIER_TM, FOURIER_WIDTH), lambda b, m: (b * nm + m, 0)),
        out_shape=jax.ShapeDtypeStruct((t, FOURIER_WIDTH), BF16),
        scratch_shapes=[pltpu.VMEM((seq, FOURIER_WIDTH), BF16), pltpu.VMEM((seq, FOURIER_WIDTH), BF16)],
        compiler_params=_params("parallel", "arbitrary"),
        name="fourier",
    )(proj, proj, ab, dft_c, dft_ms)


def _out_proj_kernel(a_ref, f_ref, x_ref, ag_ref, fg_ref, w_ref, g2_ref, h_ref, u_ref,
                     mix_scr, h_scr, w_scr):
    m, j = pl.program_id(0), pl.program_id(1)

    @pl.when(j == 0)
    def _():
        mix_scr[:, :ATTN_WIDTH] = (_rms(a_ref[...].astype(F32)) * ag_ref[...]).astype(BF16)
        mix_scr[:, ATTN_WIDTH:] = (_rms(f_ref[...].astype(F32)) * fg_ref[...]).astype(BF16)

    @pl.when(m == 0)
    def _():
        w_scr[j] = w_ref[...].astype(BF16)

    h = x_ref[...] + jnp.dot(mix_scr[...], w_scr[j], preferred_element_type=F32)
    h_ref[...] = h
    h_scr[:, pl.ds(pl.multiple_of(j * OUT_TN, OUT_TN), OUT_TN)] = h

    @pl.when(j == pl.num_programs(1) - 1)
    def _():
        u_ref[...] = (_rms(h_scr[...]) * g2_ref[...]).astype(u_ref.dtype)


def _out_proj(a, f, x2, ag, fg, w, g2):
    t, d = x2.shape
    row = lambda width: pl.BlockSpec((OUT_TM, width), lambda m, j: (m, 0))
    vec = lambda width: pl.BlockSpec((1, width), lambda m, j: (0, 0))
    tile = pl.BlockSpec((OUT_TM, OUT_TN), lambda m, j: (m, j))
    n_tiles = d // OUT_TN
    return pl.pallas_call(
        _out_proj_kernel,
        grid=(t // OUT_TM, n_tiles),
        in_specs=[
            row(ATTN_WIDTH), row(FOURIER_WIDTH), tile, vec(ATTN_WIDTH), vec(FOURIER_WIDTH),
            pl.BlockSpec((w.shape[0], OUT_TN), _weight_tile_once(n_tiles)),
            vec(d),
        ],
        out_specs=[tile, row(d)],
        out_shape=[jax.ShapeDtypeStruct((t, d), F32), jax.ShapeDtypeStruct((t, d), BF16)],
        scratch_shapes=[pltpu.VMEM((OUT_TM, w.shape[0]), BF16), pltpu.VMEM((OUT_TM, d), F32),
                        pltpu.VMEM((n_tiles, w.shape[0], OUT_TN), BF16)],
        compiler_params=_params("arbitrary", "arbitrary"),
        name="out_proj",
    )(a, f, x2, ag, fg, w, g2)


def _ffn_kernel(u_ref, wg_ref, wv_ref, cwg_ref, cwv_ref, cbg_ref, cbv_ref, wd_ref, h_ref, g_ref,
                y_ref, acc_ref, gate_scr, val_scr, *, n_ff):
    j = pl.program_id(1)
    seq = u_ref.shape[0]
    rc = seq // FFN_ROW_CHUNKS

    @pl.when((pl.program_id(0) == 0) & (j == 0))
    def _():
        acc_ref[...] = jnp.zeros_like(acc_ref)
        gate_scr[...] = jnp.zeros_like(gate_scr)
        val_scr[...] = jnp.zeros_like(val_scr)

    @pl.when(j < n_ff)
    def _():
        wg = wg_ref[...].astype(BF16)
        wv = wv_ref[...].astype(BF16)
        wd = wd_ref[...].astype(BF16)
        cwg, cwv, cbg, cbv = cwg_ref[...], cwv_ref[...], cbg_ref[...], cbv_ref[...]

        def conv3(scr, cw, cb, r0):
            prev = scr[pl.ds(FFN_PAD - 1 + r0, rc), :]
            cur = scr[pl.ds(FFN_PAD + r0, rc), :]
            nxt = scr[pl.ds(FFN_PAD + 1 + r0, rc), :]
            return prev * cw[0:1, :] + cur * cw[1:2, :] + nxt * cw[2:3, :] + cb

        def gate_and_down(c):
            r0 = c * rc
            gate = conv3(gate_scr, cwg, cbg, r0)
            val = conv3(val_scr, cwv, cbv, r0)
            act = (gate * jax.nn.sigmoid(gate) * val).astype(BF16)
            acc_ref[pl.ds(r0, rc), :] += jnp.dot(act, wd, preferred_element_type=F32)

        for c in range(FFN_ROW_CHUNKS):
            r0 = c * rc
            u = u_ref[pl.ds(r0, rc), :]
            gate_scr[pl.ds(FFN_PAD + r0, rc), :] = jnp.dot(u, wg, preferred_element_type=F32)
            val_scr[pl.ds(FFN_PAD + r0, rc), :] = jnp.dot(u, wv, preferred_element_type=F32)
            if c >= 1:
                gate_and_down(c - 1)
        gate_and_down(FFN_ROW_CHUNKS - 1)

    @pl.when(j >= n_ff)
    def _():
        rows = pl.ds(pl.multiple_of((j - n_ff) * FFN_OUT_ROWS, FFN_OUT_ROWS), FFN_OUT_ROWS)
        y_ref[...] = _rms(acc_ref[rows, :] + h_ref[...]) * g_ref[...]
        acc_ref[rows, :] = jnp.zeros((FFN_OUT_ROWS, acc_ref.shape[1]), F32)


def _ffn(u2, w_up, conv_w, conv_b, w_down, h1, g, batch, seq):
    t, d = u2.shape
    d_ff = w_down.shape[0]
    n_ff = d_ff // FFN_TN
    n_out = seq // FFN_OUT_ROWS
    taps = conv_w.shape[0]
    ff = lambda j: jnp.minimum(j, n_ff - 1)
    gate_col = lambda b, j: (0, ff(j))
    val_col = lambda b, j: (0, n_ff + ff(j))
    out_row = lambda b, j: (b * n_out + jnp.maximum(j - n_ff, 0), 0)
    return pl.pallas_call(
        functools.partial(_ffn_kernel, n_ff=n_ff),
        grid=(batch, n_ff + n_out),
        in_specs=[
            pl.BlockSpec((seq, d), lambda b, j: (b, 0), pipeline_mode=pl.Buffered(1)),
            pl.BlockSpec((d, FFN_TN), gate_col),
            pl.BlockSpec((d, FFN_TN), val_col),
            pl.BlockSpec((taps, FFN_TN), gate_col),
            pl.BlockSpec((taps, FFN_TN), val_col),
            pl.BlockSpec((1, FFN_TN), gate_col),
            pl.BlockSpec((1, FFN_TN), val_col),
            pl.BlockSpec((FFN_TN, d), lambda b, j: (ff(j), 0)),
            pl.BlockSpec((FFN_OUT_ROWS, d), out_row),
            pl.BlockSpec((1, d), lambda b, j: (0, 0)),
        ],
        out_specs=pl.BlockSpec((FFN_OUT_ROWS, d), out_row),
        out_shape=jax.ShapeDtypeStruct((t, d), F32),
        scratch_shapes=[
            pltpu.VMEM((seq, d), F32),
            pltpu.VMEM((seq + 2 * FFN_PAD, FFN_TN), F32),
            pltpu.VMEM((seq + 2 * FFN_PAD, FFN_TN), F32),
        ],
        compiler_params=pltpu.CompilerParams(dimension_semantics=("arbitrary", "arbitrary"),
                                             vmem_limit_bytes=FFN_VMEM_LIMIT_BYTES),
        name="ffn",
    )(u2, w_up, w_up, conv_w, conv_w, conv_b, conv_b, w_down, h1, g)


def _rope_tables(seq):
    t = jnp.arange(seq, dtype=jnp.int32)
    inv_freq = ROPE_THETA ** (-jnp.arange(ROPE_PAIRS, dtype=F32) / ROPE_PAIRS)

    def axis_tables(pos):
        ang = pos.astype(F32)[:, None] * inv_freq[None, :]
        ang = jnp.concatenate([ang, ang], axis=-1)
        return jnp.cos(ang), jnp.sin(ang)

    cos_r, sin_r = axis_tables(t // GRID_W)
    cos_c, sin_c = axis_tables(t % GRID_W)
    cos = jnp.concatenate([cos_r, cos_c], axis=-1)
    sin = jnp.concatenate([sin_r, sin_c], axis=-1)
    first_half = (jnp.arange(HEAD_DIM) % (2 * ROPE_PAIRS)) < ROPE_PAIRS
    sin_lo = jnp.where(first_half[None, :], -sin, 0.0)
    sin_hi = jnp.where(first_half[None, :], 0.0, sin)
    return cos, sin_lo, sin_hi


def _dft_cos_sin(rows, n_cols, n, scale=1.0):
    k = jnp.arange(n_cols, dtype=jnp.int32)
    ang = ((rows[:, None] * k[None, :]) % n).astype(F32) * (2.0 * math.pi / n)
    return jnp.cos(ang) * scale, jnp.sin(ang) * scale


def _dft_tables(n, scale):
    return _dft_cos_sin(jnp.arange(n, dtype=jnp.int32), n, n, scale)


def _seq_dft_kernel(ca_ref, sa_ref, cr_ref, sr_ref, c_ref, ms_ref):
    cr, sr = cr_ref[...], sr_ref[...]
    for i in range(DFT_COARSE_PER_STEP):
        ca, sa = ca_ref[i:i + 1, :], sa_ref[i:i + 1, :]
        rows = slice(i * DFT_SUB, (i + 1) * DFT_SUB)
        c_ref[rows, :] = (ca * cr - sa * sr).astype(c_ref.dtype)
        ms_ref[rows, :] = (-(sa * cr + ca * sr)).astype(ms_ref.dtype)


def _seq_dft_tables(n):
    coarse = jnp.arange(n // DFT_SUB, dtype=jnp.int32) * DFT_SUB
    ca, sa = _dft_cos_sin(coarse, n, n)
    cr, sr = _dft_cos_sin(jnp.arange(DFT_SUB, dtype=jnp.int32), n, n)
    steps = n // (DFT_SUB * DFT_COARSE_PER_STEP)
    coarse_spec = pl.BlockSpec((DFT_COARSE_PER_STEP, n), lambda i: (i, 0))
    fine_spec = pl.BlockSpec((DFT_SUB, n), lambda i: (0, 0))
    out_spec = pl.BlockSpec((DFT_SUB * DFT_COARSE_PER_STEP, n), lambda i: (i, 0))
    return pl.pallas_call(
        _seq_dft_kernel,
        grid=(steps,),
        in_specs=[coarse_spec, coarse_spec, fine_spec, fine_spec],
        out_specs=[out_spec, out_spec],
        out_shape=[jax.ShapeDtypeStruct((n, n), BF16)] * 2,
        compiler_params=_params("parallel"),
        name="seq_dft_tables",
    )(ca, sa, cr, sr)


def kernel(x, norm1_g, w_in, q_norm_g, k_norm_g, w_fmix, attn_out_g, fourier_out_g, w_out,
           norm2_g, w_up, conv_w, conv_b, w_down, final_g):
    batch, seq, d = x.shape
    depth = w_in.shape[0]
    rope = _rope_tables(seq)
    dft_c, dft_ms = _seq_dft_tables(seq)
    cc, sc = _dft_tables(FOURIER_GROUP_DIM, 1.0 / math.sqrt(seq * FOURIER_GROUP_DIM))

    assert depth == 1
    h = x.reshape(batch * seq, d)
    for l in range(depth):
        proj = _in_proj(h, norm1_g[l][None], w_in[l])
        attn = _attention(proj, rope, q_norm_g[l][None], k_norm_g[l][None], batch, seq)
        ab = _fourier_fold(w_fmix[l], cc, sc)
        four = _fourier(proj, ab, dft_c, dft_ms, batch, seq)
        h1, u2 = _out_proj(attn, four, h, attn_out_g[l][None], fourier_out_g[l][None],
                           w_out[l], norm2_g[l][None])
        h = _ffn(u2, w_up[l], conv_w[l], conv_b[l][None], w_down[l], h1, final_g[None], batch, seq)
    return h.reshape(batch, seq, d)
```

```python
import functools
import math

import jax
import jax.numpy as jnp
from jax import lax
from jax.experimental import pallas as pl
from jax.experimental.pallas import tpu as pltpu

F32 = jnp.float32
BF16 = jnp.bfloat16

HEAD_DIM = 128
N_Q_HEADS = 8
N_KV_HEADS = 2
Q_PER_KV = N_Q_HEADS // N_KV_HEADS
ATTN_WIDTH = N_Q_HEADS * HEAD_DIM
KV_WIDTH = N_KV_HEADS * HEAD_DIM
FOURIER_GROUPS = 8
FOURIER_GROUP_DIM = 128
FOURIER_WIDTH = FOURIER_GROUPS * FOURIER_GROUP_DIM
GRID_W = 64
ROPE_THETA = 10000.0
ROPE_PAIRS = HEAD_DIM // 4
EPS = 1e-6

VMEM_LIMIT_BYTES = 56 * 1024 * 1024

IN_TM, IN_TN = 1024, 512
ATTN_TQ = 256
FOURIER_TM = 512
DFT_SUB, DFT_COARSE_PER_STEP = 64, 8
OUT_TM, OUT_TN = 1024, 512
FFN_TN = 256
FFN_ROW_CHUNKS = 2
FFN_PAD = 8
FFN_OUT_ROWS = 256
FFN_VMEM_LIMIT_BYTES = 60 * 1024 * 1024


def _params(*sem):
    return pltpu.CompilerParams(dimension_semantics=sem, vmem_limit_bytes=VMEM_LIMIT_BYTES)


def _rms(x):
    return x * lax.rsqrt(jnp.mean(x * x, axis=-1, keepdims=True) + EPS)


def _weight_tile_once(n_tiles):
    return lambda m, j: (0, jnp.where(m == 0, j, n_tiles - 1))


def _in_proj_kernel(x_ref, g_ref, w_ref, o_ref, u_scr, w_scr):
    m, j = pl.program_id(0), pl.program_id(1)

    @pl.when(j == 0)
    def _():
        u_scr[...] = (_rms(x_ref[...]) * g_ref[...]).astype(BF16)

    @pl.when(m == 0)
    def _():
        w_scr[j] = w_ref[...].astype(BF16)

    o_ref[...] = jnp.dot(u_scr[...], w_scr[j], preferred_element_type=F32).astype(o_ref.dtype)


def _in_proj(x2, g, w):
    t, d = x2.shape
    n = w.shape[1]
    n_tiles = n // IN_TN
    return pl.pallas_call(
        _in_proj_kernel,
        grid=(t // IN_TM, n_tiles),
        in_specs=[
            pl.BlockSpec((IN_TM, d), lambda m, j: (m, 0)),
            pl.BlockSpec((1, d), lambda m, j: (0, 0)),
            pl.BlockSpec((d, IN_TN), _weight_tile_once(n_tiles)),
        ],
        out_specs=pl.BlockSpec((IN_TM, IN_TN), lambda m, j: (m, j)),
        out_shape=jax.ShapeDtypeStruct((t, n), BF16),
        scratch_shapes=[pltpu.VMEM((IN_TM, d), BF16), pltpu.VMEM((n_tiles, d, IN_TN), BF16)],
        compiler_params=_params("arbitrary", "arbitrary"),
        name="in_proj",
    )(x2, g, w)


def _rope(x, cos, sin_lo, sin_hi):
    return (x * cos
            + pltpu.roll(x, HEAD_DIM - ROPE_PAIRS, axis=1) * sin_lo
            + pltpu.roll(x, ROPE_PAIRS, axis=1) * sin_hi)


def _attn_kernel(q_ref, k_ref, v_ref, cq_ref, slq_ref, shq_ref, ck_ref, slk_ref, shk_ref,
                 qg_ref, kg_ref, o_ref, k_scr, v_scr):
    @pl.when(pl.program_id(2) == 0)
    def _():
        k = _rms(k_ref[...].astype(F32)) * kg_ref[...]
        k_scr[...] = _rope(k, ck_ref[...], slk_ref[...], shk_ref[...]).astype(BF16)
        v_scr[:, :HEAD_DIM] = v_ref[...]
        v_scr[:, HEAD_DIM:] = jnp.ones((v_ref.shape[0], HEAD_DIM), BF16)

    scale = math.log2(math.e) / math.sqrt(HEAD_DIM)
    cos, sin_lo, sin_hi = cq_ref[...], slq_ref[...], shq_ref[...]
    kk = k_scr[...]
    vv = v_scr[...]
    for h in range(Q_PER_KV):
        sl = slice(h * HEAD_DIM, (h + 1) * HEAD_DIM)
        q = _rms(q_ref[:, sl].astype(F32)) * qg_ref[...]
        q = (_rope(q, cos, sin_lo, sin_hi) * scale).astype(BF16)
        s = lax.dot_general(q, kk, (((1,), (1,)), ((), ())), preferred_element_type=F32)
        p = jnp.exp2(s - jnp.max(s, axis=-1, keepdims=True))
        o = jnp.dot(p.astype(BF16), vv, preferred_element_type=F32)
        o_ref[:, sl] = (o[:, :HEAD_DIM] / o[:, HEAD_DIM:]).astype(o_ref.dtype)


def _attention(proj, tabs, qg, kg, batch, seq):
    t = proj.shape[0]
    nq = seq // ATTN_TQ
    gw = Q_PER_KV * HEAD_DIM
    k_blk0 = ATTN_WIDTH // HEAD_DIM
    v_blk0 = (ATTN_WIDTH + KV_WIDTH) // HEAD_DIM
    cos, sin_lo, sin_hi = tabs
    q_tab = pl.BlockSpec((ATTN_TQ, HEAD_DIM), lambda b, g, i: (i, 0))
    k_tab = pl.BlockSpec((seq, HEAD_DIM), lambda b, g, i: (0, 0))
    gain = pl.BlockSpec((1, HEAD_DIM), lambda b, g, i: (0, 0))
    return pl.pallas_call(
        _attn_kernel,
        grid=(batch, N_KV_HEADS, nq),
        in_specs=[
            pl.BlockSpec((ATTN_TQ, gw), lambda b, g, i: (b * nq + i, g)),
            pl.BlockSpec((seq, HEAD_DIM), lambda b, g, i: (b, k_blk0 + g)),
            pl.BlockSpec((seq, HEAD_DIM), lambda b, g, i: (b, v_blk0 + g)),
            q_tab, q_tab, q_tab, k_tab, k_tab, k_tab, gain, gain,
        ],
        out_specs=pl.BlockSpec((ATTN_TQ, gw), lambda b, g, i: (b * nq + i, g)),
        out_shape=jax.ShapeDtypeStruct((t, ATTN_WIDTH), BF16),
        scratch_shapes=[pltpu.VMEM((seq, HEAD_DIM), BF16), pltpu.VMEM((seq, 2 * HEAD_DIM), BF16)],
        compiler_params=_params("parallel", "parallel", "arbitrary"),
        name="attention",
    )(proj, proj, proj, cos, sin_lo, sin_hi, cos, sin_lo, sin_hi, qg, kg)


def _fourier_fold_kernel(w_ref, cc_ref, sc_ref, o_ref):
    for g in range(FOURIER_GROUPS):
        w = w_ref[g]
        a = jnp.dot(cc_ref[...], w, preferred_element_type=F32, precision=lax.Precision.HIGHEST)
        b = jnp.dot(sc_ref[...], w, preferred_element_type=F32, precision=lax.Precision.HIGHEST)
        o_ref[g, :, :FOURIER_GROUP_DIM] = a.astype(o_ref.dtype)
        o_ref[g, :, FOURIER_GROUP_DIM:] = b.astype(o_ref.dtype)


def _fourier_fold(w_fmix, cc, sc):
    return pl.pallas_call(
        _fourier_fold_kernel,
        out_shape=jax.ShapeDtypeStruct((FOURIER_GROUPS, FOURIER_GROUP_DIM, 2 * FOURIER_GROUP_DIM), BF16),
        name="fourier_fold",
    )(w_fmix, cc, sc)


def _fourier_kernel(flo_ref, fhi_ref, ab_ref, c_ref, ms_ref, o_ref, za_scr, zb_scr):
    @pl.when(pl.program_id(1) == 0)
    def _():
        half = FOURIER_GROUPS // 2
        for g in range(FOURIER_GROUPS):
            src = flo_ref if g < half else fhi_ref
            lo = (g % half) * FOURIER_GROUP_DIM
            z = jnp.dot(src[:, lo:lo + FOURIER_GROUP_DIM], ab_ref[g], preferred_element_type=F32)
            dst = slice(g * FOURIER_GROUP_DIM, (g + 1) * FOURIER_GROUP_DIM)
            za_scr[:, dst] = z[:, :FOURIER_GROUP_DIM].astype(BF16)
            zb_scr[:, dst] = z[:, FOURIER_GROUP_DIM:].astype(BF16)

    acc = jnp.dot(c_ref[...], za_scr[...], preferred_element_type=F32)
    acc = acc + jnp.dot(ms_ref[...], zb_scr[...], preferred_element_type=F32)
    o_ref[...] = acc.astype(o_ref.dtype)


def _fourier(proj, ab, dft_c, dft_ms, batch, seq):
    t = proj.shape[0]
    nm = seq // FOURIER_TM
    half_w = FOURIER_WIDTH // 2
    f_blk0 = (ATTN_WIDTH + 2 * KV_WIDTH) // half_w
    return pl.pallas_call(
        _fourier_kernel,
        grid=(batch, nm),
        in_specs=[
            pl.BlockSpec((seq, half_w), lambda b, m: (b, f_blk0)),
            pl.BlockSpec((seq, half_w), lambda b, m: (b, f_blk0 + 1)),
            pl.BlockSpec(ab.shape, lambda b, m: (0, 0, 0)),
            pl.BlockSpec((FOURIER_TM, seq), lambda b, m: (m, 0)),
            pl.BlockSpec((FOURIER_TM, seq), lambda b, m: (m, 0)),
        ],
        out_specs=pl.BlockSpec((FOURIER_TM, FOURIER_WIDTH), lambda b, m: (b * nm + m, 0)),
        out_shape=jax.ShapeDtypeStruct((t, FOURIER_WIDTH), BF16),
        scratch_shapes=[pltpu.VMEM((seq, FOURIER_WIDTH), BF16), pltpu.VMEM((seq, FOURIER_WIDTH), BF16)],
        compiler_params=_params("parallel", "arbitrary"),
        name="fourier",
    )(proj, proj, ab, dft_c, dft_ms)


def _out_proj_kernel(a_ref, f_ref, x_ref, ag_ref, fg_ref, w_ref, g2_ref, h_ref, u_ref,
                     mix_scr, h_scr, w_scr):
    m, j = pl.program_id(0), pl.program_id(1)

    @pl.when(j == 0)
    def _():
        mix_scr[:, :ATTN_WIDTH] = (_rms(a_ref[...].astype(F32)) * ag_ref[...]).astype(BF16)
        mix_scr[:, ATTN_WIDTH:] = (_rms(f_ref[...].astype(F32)) * fg_ref[...]).astype(BF16)

    @pl.when(m == 0)
    def _():
        w_scr[j] = w_ref[...].astype(BF16)

    h = x_ref[...] + jnp.dot(mix_scr[...], w_scr[j], preferred_element_type=F32)
    h_ref[...] = h
    h_scr[:, pl.ds(pl.multiple_of(j * OUT_TN, OUT_TN), OUT_TN)] = h

    @pl.when(j == pl.num_programs(1) - 1)
    def _():
        u_ref[...] = (_rms(h_scr[...]) * g2_ref[...]).astype(u_ref.dtype)


def _out_proj(a, f, x2, ag, fg, w, g2):
    t, d = x2.shape
    row = lambda width: pl.BlockSpec((OUT_TM, width), lambda m, j: (m, 0))
    vec = lambda width: pl.BlockSpec((1, width), lambda m, j: (0, 0))
    tile = pl.BlockSpec((OUT_TM, OUT_TN), lambda m, j: (m, j))
    n_tiles = d // OUT_TN
    return pl.pallas_call(
        _out_proj_kernel,
        grid=(t // OUT_TM, n_tiles),
        in_specs=[
            row(ATTN_WIDTH), row(FOURIER_WIDTH), tile, vec(ATTN_WIDTH), vec(FOURIER_WIDTH),
            pl.BlockSpec((w.shape[0], OUT_TN), _weight_tile_once(n_tiles)),
            vec(d),
        ],
        out_specs=[tile, row(d)],
        out_shape=[jax.ShapeDtypeStruct((t, d), F32), jax.ShapeDtypeStruct((t, d), BF16)],
        scratch_shapes=[pltpu.VMEM((OUT_TM, w.shape[0]), BF16), pltpu.VMEM((OUT_TM, d), F32),
                        pltpu.VMEM((n_tiles, w.shape[0], OUT_TN), BF16)],
        compiler_params=_params("arbitrary", "arbitrary"),
        name="out_proj",
    )(a, f, x2, ag, fg, w, g2)


def _ffn_kernel(u_ref, wg_ref, wv_ref, cwg_ref, cwv_ref, cbg_ref, cbv_ref, wd_ref, h_ref, g_ref,
                y_ref, acc_ref, gate_scr, val_scr, *, n_ff):
    j = pl.program_id(1)
    seq = u_ref.shape[0]
    rc = seq // FFN_ROW_CHUNKS

    @pl.when((pl.program_id(0) == 0) & (j == 0))
    def _():
        acc_ref[...] = jnp.zeros_like(acc_ref)
        gate_scr[...] = jnp.zeros_like(gate_scr)
        val_scr[...] = jnp.zeros_like(val_scr)

    @pl.when(j < n_ff)
    def _():
        wg = wg_ref[...].astype(BF16)
        wv = wv_ref[...].astype(BF16)
        wd = wd_ref[...].astype(BF16)
        cwg, cwv, cbg, cbv = cwg_ref[...], cwv_ref[...], cbg_ref[...], cbv_ref[...]

        def conv3(scr, cw, cb, r0):
            prev = scr[pl.ds(FFN_PAD - 1 + r0, rc), :]
            cur = scr[pl.ds(FFN_PAD + r0, rc), :]
            nxt = scr[pl.ds(FFN_PAD + 1 + r0, rc), :]
            return prev * cw[0:1, :] + cur * cw[1:2, :] + nxt * cw[2:3, :] + cb

        def gate_and_down(c):
            r0 = c * rc
            gate = conv3(gate_scr, cwg, cbg, r0)
            val = conv3(val_scr, cwv, cbv, r0)
            act = (gate * jax.nn.sigmoid(gate) * val).astype(BF16)
            acc_ref[pl.ds(r0, rc), :] += jnp.dot(act, wd, preferred_element_type=F32)

        for c in range(FFN_ROW_CHUNKS):
            r0 = c * rc
            u = u_ref[pl.ds(r0, rc), :]
            gate_scr[pl.ds(FFN_PAD + r0, rc), :] = jnp.dot(u, wg, preferred_element_type=F32)
            val_scr[pl.ds(FFN_PAD + r0, rc), :] = jnp.dot(u, wv, preferred_element_type=F32)
            if c >= 1:
                gate_and_down(c - 1)
        gate_and_down(FFN_ROW_CHUNKS - 1)

    @pl.when(j >= n_ff)
    def _():
        rows = pl.ds(pl.multiple_of((j - n_ff) * FFN_OUT_ROWS, FFN_OUT_ROWS), FFN_OUT_ROWS)
        y_ref[...] = _rms(acc_ref[rows, :] + h_ref[...]) * g_ref[...]
        acc_ref[rows, :] = jnp.zeros((FFN_OUT_ROWS, acc_ref.shape[1]), F32)


def _ffn(u2, w_up, conv_w, conv_b, w_down, h1, g, batch, seq):
    t, d = u2.shape
    d_ff = w_down.shape[0]
    n_ff = d_ff // FFN_TN
    n_out = seq // FFN_OUT_ROWS
    taps = conv_w.shape[0]
    ff = lambda j: jnp.minimum(j, n_ff - 1)
    gate_col = lambda b, j: (0, ff(j))
    val_col = lambda b, j: (0, n_ff + ff(j))
    out_row = lambda b, j: (b * n_out + jnp.maximum(j - n_ff, 0), 0)
    return pl.pallas_call(
        functools.partial(_ffn_kernel, n_ff=n_ff),
        grid=(batch, n_ff + n_out),
        in_specs=[
            pl.BlockSpec((seq, d), lambda b, j: (b, 0), pipeline_mode=pl.Buffered(1)),
            pl.BlockSpec((d, FFN_TN), gate_col),
            pl.BlockSpec((d, FFN_TN), val_col),
            pl.BlockSpec((taps, FFN_TN), gate_col),
            pl.BlockSpec((taps, FFN_TN), val_col),
            pl.BlockSpec((1, FFN_TN), gate_col),
            pl.BlockSpec((1, FFN_TN), val_col),
            pl.BlockSpec((FFN_TN, d), lambda b, j: (ff(j), 0)),
            pl.BlockSpec((FFN_OUT_ROWS, d), out_row),
            pl.BlockSpec((1, d), lambda b, j: (0, 0)),
        ],
        out_specs=pl.BlockSpec((FFN_OUT_ROWS, d), out_row),
        out_shape=jax.ShapeDtypeStruct((t, d), F32),
        scratch_shapes=[
            pltpu.VMEM((seq, d), F32),
            pltpu.VMEM((seq + 2 * FFN_PAD, FFN_TN), F32),
            pltpu.VMEM((seq + 2 * FFN_PAD, FFN_TN), F32),
        ],
        compiler_params=pltpu.CompilerParams(dimension_semantics=("arbitrary", "arbitrary"),
                                             vmem_limit_bytes=FFN_VMEM_LIMIT_BYTES),
        name="ffn",
    )(u2, w_up, w_up, conv_w, conv_w, conv_b, conv_b, w_down, h1, g)


def _rope_tables(seq):
    t = jnp.arange(seq, dtype=jnp.int32)
    inv_freq = ROPE_THETA ** (-jnp.arange(ROPE_PAIRS, dtype=F32) / ROPE_PAIRS)

    def axis_tables(pos):
        ang = pos.astype(F32)[:, None] * inv_freq[None, :]
        ang = jnp.concatenate([ang, ang], axis=-1)
        return jnp.cos(ang), jnp.sin(ang)

    cos_r, sin_r = axis_tables(t // GRID_W)
    cos_c, sin_c = axis_tables(t % GRID_W)
    cos = jnp.concatenate([cos_r, cos_c], axis=-1)
    sin = jnp.concatenate([sin_r, sin_c], axis=-1)
    first_half = (jnp.arange(HEAD_DIM) % (2 * ROPE_PAIRS)) < ROPE_PAIRS
    sin_lo = jnp.where(first_half[None, :], -sin, 0.0)
    sin_hi = jnp.where(first_half[None, :], 0.0, sin)
    return cos, sin_lo, sin_hi


def _dft_cos_sin(rows, n_cols, n, scale=1.0):
    k = jnp.arange(n_cols, dtype=jnp.int32)
    ang = ((rows[:, None] * k[None, :]) % n).astype(F32) * (2.0 * math.pi / n)
    return jnp.cos(ang) * scale, jnp.sin(ang) * scale


def _dft_tables(n, scale):
    return _dft_cos_sin(jnp.arange(n, dtype=jnp.int32), n, n, scale)


def _seq_dft_kernel(ca_ref, sa_ref, cr_ref, sr_ref, c_ref, ms_ref):
    cr, sr = cr_ref[...], sr_ref[...]
    for i in range(DFT_COARSE_PER_STEP):
        ca, sa = ca_ref[i:i + 1, :], sa_ref[i:i + 1, :]
        rows = slice(i * DFT_SUB, (i + 1) * DFT_SUB)
        c_ref[rows, :] = (ca * cr - sa * sr).astype(c_ref.dtype)
        ms_ref[rows, :] = (-(sa * cr + ca * sr)).astype(ms_ref.dtype)


def _seq_dft_tables(n):
    coarse = jnp.arange(n // DFT_SUB, dtype=jnp.int32) * DFT_SUB
    ca, sa = _dft_cos_sin(coarse, n, n)
    cr, sr = _dft_cos_sin(jnp.arange(DFT_SUB, dtype=jnp.int32), n, n)
    steps = n // (DFT_SUB * DFT_COARSE_PER_STEP)
    coarse_spec = pl.BlockSpec((DFT_COARSE_PER_STEP, n), lambda i: (i, 0))
    fine_spec = pl.BlockSpec((DFT_SUB, n), lambda i: (0, 0))
    out_spec = pl.BlockSpec((DFT_SUB * DFT_COARSE_PER_STEP, n), lambda i: (i, 0))
    return pl.pallas_call(
        _seq_dft_kernel,
        grid=(steps,),
        in_specs=[coarse_spec, coarse_spec, fine_spec, fine_spec],
        out_specs=[out_spec, out_spec],
        out_shape=[jax.ShapeDtypeStruct((n, n), BF16)] * 2,
        compiler_params=_params("parallel"),
        name="seq_dft_tables",
    )(ca, sa, cr, sr)


def kernel(x, norm1_g, w_in, q_norm_g, k_norm_g, w_fmix, attn_out_g, fourier_out_g, w_out,
           norm2_g, w_up, conv_w, conv_b, w_down, final_g):
    batch, seq, d = x.shape
    depth = w_in.shape[0]
    rope = _rope_tables(seq)
    dft_c, dft_ms = _seq_dft_tables(seq)
    cc, sc = _dft_tables(FOURIER_GROUP_DIM, 1.0 / math.sqrt(seq * FOURIER_GROUP_DIM))

    assert depth == 1
    h = x.reshape(batch * seq, d)
    for l in range(depth):
        proj = _in_proj(h, norm1_g[l][None], w_in[l])
        attn = _attention(proj, rope, q_norm_g[l][None], k_norm_g[l][None], batch, seq)
        ab = _fourier_fold(w_fmix[l], cc, sc)
        four = _fourier(proj, ab, dft_c, dft_ms, batch, seq)
        h1, u2 = _out_proj(attn, four, h, attn_out_g[l][None], fourier_out_g[l][None],
                           w_out[l], norm2_g[l][None])
        h = _ffn(u2, w_up[l], conv_w[l], conv_b[l][None], w_down[l], h1, final_g[None], batch, seq)
    return h.reshape(batch, seq, d)
```

```python
import functools
import math

import jax
import jax.numpy as jnp
import numpy as np
from jax import lax
from jax.experimental import pallas as pl
from jax.experimental.pallas import tpu as pltpu

F32 = jnp.float32
BF16 = jnp.bfloat16

HEAD_DIM = 128
N_Q_HEADS = 8
N_KV_HEADS = 2
Q_PER_KV = N_Q_HEADS // N_KV_HEADS
ATTN_WIDTH = N_Q_HEADS * HEAD_DIM
KV_WIDTH = N_KV_HEADS * HEAD_DIM
FOURIER_GROUPS = 8
FOURIER_GROUP_DIM = 128
FOURIER_WIDTH = FOURIER_GROUPS * FOURIER_GROUP_DIM
GRID_W = 64
ROPE_THETA = 10000.0
ROPE_PAIRS = HEAD_DIM // 4
EPS = 1e-6

VMEM_LIMIT_BYTES = 56 * 1024 * 1024

IN_TM, IN_TN = 1024, 512
ATTN_TQ = 256
FOURIER_TM = 512
DFT_SUB, DFT_COARSE_PER_STEP = 64, 8
OUT_TM, OUT_TN = 1024, 512
FFN_TN = 256
FFN_ROW_CHUNKS = 2
FFN_PAD = 8
FFN_OUT_ROWS = 256
FFN_VMEM_LIMIT_BYTES = 60 * 1024 * 1024


def _params(*sem):
    return pltpu.CompilerParams(dimension_semantics=sem, vmem_limit_bytes=VMEM_LIMIT_BYTES)


def _rms(x):
    return x * lax.rsqrt(jnp.mean(x * x, axis=-1, keepdims=True) + EPS)


def _weight_tile_once(n_tiles):
    return lambda m, j: (0, jnp.where(m == 0, j, n_tiles - 1))


def _in_proj_kernel(x_ref, g_ref, w_ref, o_ref, u_scr, w_scr):
    m, j = pl.program_id(0), pl.program_id(1)

    @pl.when(j == 0)
    def _():
        u_scr[...] = (_rms(x_ref[...]) * g_ref[...]).astype(BF16)

    @pl.when(m == 0)
    def _():
        w_scr[j] = w_ref[...].astype(BF16)

    o_ref[...] = jnp.dot(u_scr[...], w_scr[j], preferred_element_type=F32).astype(o_ref.dtype)


def _in_proj(x2, g, w):
    t, d = x2.shape
    n = w.shape[1]
    n_tiles = n // IN_TN
    return pl.pallas_call(
        _in_proj_kernel,
        grid=(t // IN_TM, n_tiles),
        in_specs=[
            pl.BlockSpec((IN_TM, d), lambda m, j: (m, 0)),
            pl.BlockSpec((1, d), lambda m, j: (0, 0)),
            pl.BlockSpec((d, IN_TN), _weight_tile_once(n_tiles)),
        ],
        out_specs=pl.BlockSpec((IN_TM, IN_TN), lambda m, j: (m, j)),
        out_shape=jax.ShapeDtypeStruct((t, n), BF16),
        scratch_shapes=[pltpu.VMEM((IN_TM, d), BF16), pltpu.VMEM((n_tiles, d, IN_TN), BF16)],
        compiler_params=_params("arbitrary", "arbitrary"),
        name="in_proj",
    )(x2, g, w)


def _rope(x, cos, sin_lo, sin_hi):
    return (x * cos
            + pltpu.roll(x, HEAD_DIM - ROPE_PAIRS, axis=1) * sin_lo
            + pltpu.roll(x, ROPE_PAIRS, axis=1) * sin_hi)


def _attn_kernel(q_ref, k_ref, v_ref, cq_ref, slq_ref, shq_ref, ck_ref, slk_ref, shk_ref,
                 qg_ref, kg_ref, o_ref, k_scr, v_scr):
    @pl.when(pl.program_id(2) == 0)
    def _():
        k = _rms(k_ref[...].astype(F32)) * kg_ref[...]
        k_scr[...] = _rope(k, ck_ref[...], slk_ref[...], shk_ref[...]).astype(BF16)
        v_scr[:, :HEAD_DIM] = v_ref[...]
        v_scr[:, HEAD_DIM:] = jnp.ones((v_ref.shape[0], HEAD_DIM), BF16)

    scale = math.log2(math.e) / math.sqrt(HEAD_DIM)
    cos, sin_lo, sin_hi = cq_ref[...], slq_ref[...], shq_ref[...]
    kk = k_scr[...]
    vv = v_scr[...]
    for h in range(Q_PER_KV):
        sl = slice(h * HEAD_DIM, (h + 1) * HEAD_DIM)
        q = _rms(q_ref[:, sl].astype(F32)) * qg_ref[...]
        q = (_rope(q, cos, sin_lo, sin_hi) * scale).astype(BF16)
        s = lax.dot_general(q, kk, (((1,), (1,)), ((), ())), preferred_element_type=F32)
        p = jnp.exp2(s - jnp.max(s, axis=-1, keepdims=True))
        o = jnp.dot(p.astype(BF16), vv, preferred_element_type=F32)
        o_ref[:, sl] = (o[:, :HEAD_DIM] / o[:, HEAD_DIM:]).astype(o_ref.dtype)


def _attention(proj, tabs, qg, kg, batch, seq):
    t = proj.shape[0]
    nq = seq // ATTN_TQ
    gw = Q_PER_KV * HEAD_DIM
    k_blk0 = ATTN_WIDTH // HEAD_DIM
    v_blk0 = (ATTN_WIDTH + KV_WIDTH) // HEAD_DIM
    cos, sin_lo, sin_hi = tabs
    q_tab = pl.BlockSpec((ATTN_TQ, HEAD_DIM), lambda b, g, i: (i, 0))
    k_tab = pl.BlockSpec((seq, HEAD_DIM), lambda b, g, i: (0, 0))
    gain = pl.BlockSpec((1, HEAD_DIM), lambda b, g, i: (0, 0))
    return pl.pallas_call(
        _attn_kernel,
        grid=(batch, N_KV_HEADS, nq),
        in_specs=[
            pl.BlockSpec((ATTN_TQ, gw), lambda b, g, i: (b * nq + i, g)),
            pl.BlockSpec((seq, HEAD_DIM), lambda b, g, i: (b, k_blk0 + g)),
            pl.BlockSpec((seq, HEAD_DIM), lambda b, g, i: (b, v_blk0 + g)),
            q_tab, q_tab, q_tab, k_tab, k_tab, k_tab, gain, gain,
        ],
        out_specs=pl.BlockSpec((ATTN_TQ, gw), lambda b, g, i: (b * nq + i, g)),
        out_shape=jax.ShapeDtypeStruct((t, ATTN_WIDTH), BF16),
        scratch_shapes=[pltpu.VMEM((seq, HEAD_DIM), BF16), pltpu.VMEM((seq, 2 * HEAD_DIM), BF16)],
        compiler_params=_params("parallel", "parallel", "arbitrary"),
        name="attention",
    )(proj, proj, proj, cos, sin_lo, sin_hi, cos, sin_lo, sin_hi, qg, kg)


def _fourier_fold_kernel(w_ref, cc_ref, sc_ref, o_ref):
    for g in range(FOURIER_GROUPS):
        w = w_ref[g]
        a = jnp.dot(cc_ref[...], w, preferred_element_type=F32, precision=lax.Precision.HIGHEST)
        b = jnp.dot(sc_ref[...], w, preferred_element_type=F32, precision=lax.Precision.HIGHEST)
        o_ref[g, :, :FOURIER_GROUP_DIM] = a.astype(o_ref.dtype)
        o_ref[g, :, FOURIER_GROUP_DIM:] = b.astype(o_ref.dtype)


def _fourier_fold(w_fmix, cc, sc):
    return pl.pallas_call(
        _fourier_fold_kernel,
        out_shape=jax.ShapeDtypeStruct((FOURIER_GROUPS, FOURIER_GROUP_DIM, 2 * FOURIER_GROUP_DIM), BF16),
        name="fourier_fold",
    )(w_fmix, cc, sc)


def _fourier_kernel(flo_ref, fhi_ref, ab_ref, c_ref, ms_ref, o_ref, za_scr, zb_scr):
    @pl.when(pl.program_id(1) == 0)
    def _():
        half = FOURIER_GROUPS // 2
        for g in range(FOURIER_GROUPS):
            src = flo_ref if g < half else fhi_ref
            lo = (g % half) * FOURIER_GROUP_DIM
            z = jnp.dot(src[:, lo:lo + FOURIER_GROUP_DIM], ab_ref[g], preferred_element_type=F32)
            dst = slice(g * FOURIER_GROUP_DIM, (g + 1) * FOURIER_GROUP_DIM)
            za_scr[:, dst] = z[:, :FOURIER_GROUP_DIM].astype(BF16)
            zb_scr[:, dst] = z[:, FOURIER_GROUP_DIM:].astype(BF16)

    acc = jnp.dot(c_ref[...], za_scr[...], preferred_element_type=F32)
    acc = acc + jnp.dot(ms_ref[...], zb_scr[...], preferred_element_type=F32)
    o_ref[...] = acc.astype(o_ref.dtype)


def _fourier(proj, ab, dft_c, dft_ms, batch, seq):
    t = proj.shape[0]
    nm = seq // FOURIER_TM
    half_w = FOURIER_WIDTH // 2
    f_blk0 = (ATTN_WIDTH + 2 * KV_WIDTH) // half_w
    return pl.pallas_call(
        _fourier_kernel,
        grid=(batch, nm),
        in_specs=[
            pl.BlockSpec((seq, half_w), lambda b, m: (b, f_blk0)),
            pl.BlockSpec((seq, half_w), lambda b, m: (b, f_blk0 + 1)),
            pl.BlockSpec(ab.shape, lambda b, m: (0, 0, 0)),
            pl.BlockSpec((FOURIER_TM, seq), lambda b, m: (m, 0)),
            pl.BlockSpec((FOURIER_TM, seq), lambda b, m: (m, 0)),
        ],
        out_specs=pl.BlockSpec((FOURIER_TM, FOURIER_WIDTH), lambda b, m: (b * nm + m, 0)),
        out_shape=jax.ShapeDtypeStruct((t, FOURIER_WIDTH), BF16),
        scratch_shapes=[pltpu.VMEM((seq, FOURIER_WIDTH), BF16), pltpu.VMEM((seq, FOURIER_WIDTH), BF16)],
        compiler_params=_params("parallel", "arbitrary"),
        name="fourier",
    )(proj, proj, ab, dft_c, dft_ms)


def _out_proj_kernel(a_ref, f_ref, x_ref, ag_ref, fg_ref, w_ref, g2_ref, h_ref, u_ref,
                     mix_scr, h_scr, w_scr):
    m, j = pl.program_id(0), pl.program_id(1)

    @pl.when(j == 0)
    def _():
        mix_scr[:, :ATTN_WIDTH] = (_rms(a_ref[...].astype(F32)) * ag_ref[...]).astype(BF16)
        mix_scr[:, ATTN_WIDTH:] = (_rms(f_ref[...].astype(F32)) * fg_ref[...]).astype(BF16)

    @pl.when(m == 0)
    def _():
        w_scr[j] = w_ref[...].astype(BF16)

    h = x_ref[...] + jnp.dot(mix_scr[...], w_scr[j], preferred_element_type=F32)
    h_ref[...] = h
    h_scr[:, pl.ds(pl.multiple_of(j * OUT_TN, OUT_TN), OUT_TN)] = h

    @pl.when(j == pl.num_programs(1) - 1)
    def _():
        u_ref[...] = (_rms(h_scr[...]) * g2_ref[...]).astype(u_ref.dtype)


def _out_proj(a, f, x2, ag, fg, w, g2):
    t, d = x2.shape
    row = lambda width: pl.BlockSpec((OUT_TM, width), lambda m, j: (m, 0))
    vec = lambda width: pl.BlockSpec((1, width), lambda m, j: (0, 0))
    tile = pl.BlockSpec((OUT_TM, OUT_TN), lambda m, j: (m, j))
    n_tiles = d // OUT_TN
    return pl.pallas_call(
        _out_proj_kernel,
        grid=(t // OUT_TM, n_tiles),
        in_specs=[
            row(ATTN_WIDTH), row(FOURIER_WIDTH), tile, vec(ATTN_WIDTH), vec(FOURIER_WIDTH),
            pl.BlockSpec((w.shape[0], OUT_TN), _weight_tile_once(n_tiles)),
            vec(d),
        ],
        out_specs=[tile, row(d)],
        out_shape=[jax.ShapeDtypeStruct((t, d), F32), jax.ShapeDtypeStruct((t, d), BF16)],
        scratch_shapes=[pltpu.VMEM((OUT_TM, w.shape[0]), BF16), pltpu.VMEM((OUT_TM, d), F32),
                        pltpu.VMEM((n_tiles, w.shape[0], OUT_TN), BF16)],
        compiler_params=_params("arbitrary", "arbitrary"),
        name="out_proj",
    )(a, f, x2, ag, fg, w, g2)


def _ffn_kernel(u_ref, wg_ref, wv_ref, cwg_ref, cwv_ref, cbg_ref, cbv_ref, wd_ref, h_ref, g_ref,
                y_ref, acc_ref, gate_scr, val_scr, *, n_ff):
    j = pl.program_id(1)
    seq = u_ref.shape[0]
    rc = seq // FFN_ROW_CHUNKS

    @pl.when((pl.program_id(0) == 0) & (j == 0))
    def _():
        acc_ref[...] = jnp.zeros_like(acc_ref)
        gate_scr[...] = jnp.zeros_like(gate_scr)
        val_scr[...] = jnp.zeros_like(val_scr)

    @pl.when(j < n_ff)
    def _():
        wg = wg_ref[...].astype(BF16)
        wv = wv_ref[...].astype(BF16)
        wd = wd_ref[...].astype(BF16)
        cwg, cwv, cbg, cbv = cwg_ref[...], cwv_ref[...], cbg_ref[...], cbv_ref[...]

        def conv3(scr, cw, cb, r0):
            prev = scr[pl.ds(FFN_PAD - 1 + r0, rc), :]
            cur = scr[pl.ds(FFN_PAD + r0, rc), :]
            nxt = scr[pl.ds(FFN_PAD + 1 + r0, rc), :]
            return prev * cw[0:1, :] + cur * cw[1:2, :] + nxt * cw[2:3, :] + cb

        def gate_and_down(c):
            r0 = c * rc
            gate = conv3(gate_scr, cwg, cbg, r0)
            val = conv3(val_scr, cwv, cbv, r0)
            act = (gate * jax.nn.sigmoid(gate) * val).astype(BF16)
            acc_ref[pl.ds(r0, rc), :] += jnp.dot(act, wd, preferred_element_type=F32)

        for c in range(FFN_ROW_CHUNKS):
            r0 = c * rc
            u = u_ref[pl.ds(r0, rc), :]
            gate_scr[pl.ds(FFN_PAD + r0, rc), :] = jnp.dot(u, wg, preferred_element_type=F32)
            val_scr[pl.ds(FFN_PAD + r0, rc), :] = jnp.dot(u, wv, preferred_element_type=F32)
            if c >= 1:
                gate_and_down(c - 1)
        gate_and_down(FFN_ROW_CHUNKS - 1)

    @pl.when(j >= n_ff)
    def _():
        rows = pl.ds(pl.multiple_of((j - n_ff) * FFN_OUT_ROWS, FFN_OUT_ROWS), FFN_OUT_ROWS)
        y_ref[...] = _rms(acc_ref[rows, :] + h_ref[...]) * g_ref[...]
        acc_ref[rows, :] = jnp.zeros((FFN_OUT_ROWS, acc_ref.shape[1]), F32)


def _ffn(u2, w_up, conv_w, conv_b, w_down, h1, g, batch, seq):
    t, d = u2.shape
    d_ff = w_down.shape[0]
    n_ff = d_ff // FFN_TN
    n_out = seq // FFN_OUT_ROWS
    taps = conv_w.shape[0]
    ff = lambda j: jnp.minimum(j, n_ff - 1)
    gate_col = lambda b, j: (0, ff(j))
    val_col = lambda b, j: (0, n_ff + ff(j))
    out_row = lambda b, j: (b * n_out + jnp.maximum(j - n_ff, 0), 0)
    return pl.pallas_call(
        functools.partial(_ffn_kernel, n_ff=n_ff),
        grid=(batch, n_ff + n_out),
        in_specs=[
            pl.BlockSpec((seq, d), lambda b, j: (b, 0), pipeline_mode=pl.Buffered(1)),
            pl.BlockSpec((d, FFN_TN), gate_col),
            pl.BlockSpec((d, FFN_TN), val_col),
            pl.BlockSpec((taps, FFN_TN), gate_col),
            pl.BlockSpec((taps, FFN_TN), val_col),
            pl.BlockSpec((1, FFN_TN), gate_col),
            pl.BlockSpec((1, FFN_TN), val_col),
            pl.BlockSpec((FFN_TN, d), lambda b, j: (ff(j), 0)),
            pl.BlockSpec((FFN_OUT_ROWS, d), out_row),
            pl.BlockSpec((1, d), lambda b, j: (0, 0)),
        ],
        out_specs=pl.BlockSpec((FFN_OUT_ROWS, d), out_row),
        out_shape=jax.ShapeDtypeStruct((t, d), F32),
        scratch_shapes=[
            pltpu.VMEM((seq, d), F32),
            pltpu.VMEM((seq + 2 * FFN_PAD, FFN_TN), F32),
            pltpu.VMEM((seq + 2 * FFN_PAD, FFN_TN), F32),
        ],
        compiler_params=pltpu.CompilerParams(dimension_semantics=("arbitrary", "arbitrary"),
                                             vmem_limit_bytes=FFN_VMEM_LIMIT_BYTES),
        name="ffn",
    )(u2, w_up, w_up, conv_w, conv_w, conv_b, conv_b, w_down, h1, g)


def _rope_tables(seq):
    t = np.arange(seq)
    inv_freq = ROPE_THETA ** (-np.arange(ROPE_PAIRS, dtype=np.float64) / ROPE_PAIRS)

    def axis_tables(pos):
        ang = pos[:, None] * inv_freq[None, :]
        ang = np.concatenate([ang, ang], axis=-1)
        return np.cos(ang), np.sin(ang)

    cos_r, sin_r = axis_tables(t // GRID_W)
    cos_c, sin_c = axis_tables(t % GRID_W)
    cos = np.concatenate([cos_r, cos_c], axis=-1)
    sin = np.concatenate([sin_r, sin_c], axis=-1)
    first_half = (np.arange(HEAD_DIM) % (2 * ROPE_PAIRS)) < ROPE_PAIRS
    sin_lo = np.where(first_half[None, :], -sin, 0.0)
    sin_hi = np.where(first_half[None, :], 0.0, sin)
    return tuple(jnp.asarray(tab, F32) for tab in (cos, sin_lo, sin_hi))


def _dft_cos_sin(rows, n_cols, n, scale=1.0):
    ang = ((rows[:, None] * np.arange(n_cols)[None, :]) % n) * (2.0 * math.pi / n)
    return jnp.asarray(np.cos(ang) * scale, F32), jnp.asarray(np.sin(ang) * scale, F32)


def _dft_tables(n, scale):
    return _dft_cos_sin(np.arange(n), n, n, scale)


def _seq_dft_kernel(ca_ref, sa_ref, cr_ref, sr_ref, c_ref, ms_ref):
    cr, sr = cr_ref[...], sr_ref[...]
    for i in range(DFT_COARSE_PER_STEP):
        ca, sa = ca_ref[i:i + 1, :], sa_ref[i:i + 1, :]
        rows = slice(i * DFT_SUB, (i + 1) * DFT_SUB)
        c_ref[rows, :] = (ca * cr - sa * sr).astype(c_ref.dtype)
        ms_ref[rows, :] = (-(sa * cr + ca * sr)).astype(ms_ref.dtype)


def _seq_dft_tables(n):
    ca, sa = _dft_cos_sin(np.arange(n // DFT_SUB) * DFT_SUB, n, n)
    cr, sr = _dft_cos_sin(np.arange(DFT_SUB), n, n)
    steps = n // (DFT_SUB * DFT_COARSE_PER_STEP)
    coarse_spec = pl.BlockSpec((DFT_COARSE_PER_STEP, n), lambda i: (i, 0))
    fine_spec = pl.BlockSpec((DFT_SUB, n), lambda i: (0, 0))
    out_spec = pl.BlockSpec((DFT_SUB * DFT_COARSE_PER_STEP, n), lambda i: (i, 0))
    return pl.pallas_call(
        _seq_dft_kernel,
        grid=(steps,),
        in_specs=[coarse_spec, coarse_spec, fine_spec, fine_spec],
        out_specs=[out_spec, out_spec],
        out_shape=[jax.ShapeDtypeStruct((n, n), BF16)] * 2,
        compiler_params=_params("parallel"),
        name="seq_dft_tables",
    )(ca, sa, cr, sr)


def kernel(x, norm1_g, w_in, q_norm_g, k_norm_g, w_fmix, attn_out_g, fourier_out_g, w_out,
           norm2_g, w_up, conv_w, conv_b, w_down, final_g):
    batch, seq, d = x.shape
    depth = w_in.shape[0]
    rope = _rope_tables(seq)
    dft_c, dft_ms = _seq_dft_tables(seq)
    cc, sc = _dft_tables(FOURIER_GROUP_DIM, 1.0 / math.sqrt(seq * FOURIER_GROUP_DIM))

    assert depth == 1
    h = x.reshape(batch * seq, d)
    for l in range(depth):
        proj = _in_proj(h, norm1_g[l][None], w_in[l])
        attn = _attention(proj, rope, q_norm_g[l][None], k_norm_g[l][None], batch, seq)
        ab = _fourier_fold(w_fmix[l], cc, sc)
        four = _fourier(proj, ab, dft_c, dft_ms, batch, seq)
        h1, u2 = _out_proj(attn, four, h, attn_out_g[l][None], fourier_out_g[l][None],
                           w_out[l], norm2_g[l][None])
        h = _ffn(u2, w_up[l], conv_w[l], conv_b[l][None], w_down[l], h1, final_g[None], batch, seq)
    return h.reshape(batch, seq, d)
```

```python
import functools
import math

import jax
import jax.numpy as jnp
import numpy as np
from jax import lax
from jax.experimental import pallas as pl
from jax.experimental.pallas import tpu as pltpu

F32 = jnp.float32
BF16 = jnp.bfloat16

HEAD_DIM = 128
N_Q_HEADS = 8
N_KV_HEADS = 2
Q_PER_KV = N_Q_HEADS // N_KV_HEADS
ATTN_WIDTH = N_Q_HEADS * HEAD_DIM
KV_WIDTH = N_KV_HEADS * HEAD_DIM
FOURIER_GROUPS = 8
FOURIER_GROUP_DIM = 128
FOURIER_WIDTH = FOURIER_GROUPS * FOURIER_GROUP_DIM
GRID_W = 64
ROPE_THETA = 10000.0
ROPE_PAIRS = HEAD_DIM // 4
EPS = 1e-6

VMEM_LIMIT_BYTES = 56 * 1024 * 1024

IN_TM, IN_TN = 1024, 512
ATTN_TQ = 256
BF16_SUBLANES = 16
CAST_DOWN_ROWS = 128
FOURIER_TM = 512
DFT_SUB, DFT_COARSE_PER_STEP = 64, 8
OUT_TM, OUT_TN = 1024, 512
FFN_TN = 512
FFN_ROW_SPLIT = (1280, 768)
FFN_PAD = 8
FFN_OUT_ROWS = 256
FFN_VMEM_LIMIT_BYTES = 60 * 1024 * 1024


def _params(*sem):
    return pltpu.CompilerParams(dimension_semantics=sem, vmem_limit_bytes=VMEM_LIMIT_BYTES)


def _rms(x):
    return x * lax.rsqrt(jnp.mean(x * x, axis=-1, keepdims=True) + EPS)


def _weight_tile_once(n_tiles):
    return lambda m, j: (0, jnp.where(m == 0, j, n_tiles - 1))


def _in_proj_kernel(x_ref, g_ref, w_ref, o_ref, u_scr, w_scr):
    m, j = pl.program_id(0), pl.program_id(1)

    @pl.when(j == 0)
    def _():
        u_scr[...] = (_rms(x_ref[...]) * g_ref[...]).astype(BF16)

    @pl.when(m == 0)
    def _():
        w_scr[j] = w_ref[...].astype(BF16)

    o_ref[...] = jnp.dot(u_scr[...], w_scr[j], preferred_element_type=F32).astype(o_ref.dtype)


def _in_proj(x2, g, w):
    t, d = x2.shape
    n = w.shape[1]
    n_tiles = n // IN_TN
    return pl.pallas_call(
        _in_proj_kernel,
        grid=(t // IN_TM, n_tiles),
        in_specs=[
            pl.BlockSpec((IN_TM, d), lambda m, j: (m, 0)),
            pl.BlockSpec((1, d), lambda m, j: (0, 0)),
            pl.BlockSpec((d, IN_TN), _weight_tile_once(n_tiles)),
        ],
        out_specs=pl.BlockSpec((IN_TM, IN_TN), lambda m, j: (m, j)),
        out_shape=jax.ShapeDtypeStruct((t, n), BF16),
        scratch_shapes=[pltpu.VMEM((IN_TM, d), BF16), pltpu.VMEM((n_tiles, d, IN_TN), BF16)],
        compiler_params=_params("arbitrary", "arbitrary"),
        name="in_proj",
    )(x2, g, w)


def _rope(x, cos, sin_lo, sin_hi):
    return (x * cos
            + pltpu.roll(x, HEAD_DIM - ROPE_PAIRS, axis=1) * sin_lo
            + pltpu.roll(x, ROPE_PAIRS, axis=1) * sin_hi)


def _attn_kernel(q_ref, k_ref, v_ref, cq_ref, slq_ref, shq_ref, ck_ref, slk_ref, shk_ref,
                 qg_ref, kg_ref, wu_ref, wd_ref, o_ref, wu_out, wd_out, k_scr, v_scr, *, down_blocks):
    step = ((pl.program_id(0) * pl.num_programs(1) + pl.program_id(1)) * pl.num_programs(2)
            + pl.program_id(2))
    wu_out[...] = wu_ref[...].astype(BF16)

    @pl.when(step < down_blocks)
    def _():
        wd_out[...] = wd_ref[...].astype(BF16)

    @pl.when(pl.program_id(2) == 0)
    def _():
        k = _rms(k_ref[...].astype(F32)) * kg_ref[...]
        k_scr[...] = _rope(k, ck_ref[...], slk_ref[...], shk_ref[...]).astype(BF16)
        v_scr[:, :HEAD_DIM] = v_ref[...]
        v_scr[:, HEAD_DIM:] = jnp.ones((v_ref.shape[0], HEAD_DIM), BF16)

    scale = math.log2(math.e) / math.sqrt(HEAD_DIM)
    cos, sin_lo, sin_hi = cq_ref[...], slq_ref[...], shq_ref[...]
    kk = k_scr[...]
    vv = v_scr[...]
    for h in range(Q_PER_KV):
        sl = slice(h * HEAD_DIM, (h + 1) * HEAD_DIM)
        q = _rms(q_ref[:, sl].astype(F32)) * qg_ref[...]
        q = (_rope(q, cos, sin_lo, sin_hi) * scale).astype(BF16)
        s = lax.dot_general(q, kk, (((1,), (1,)), ((), ())), preferred_element_type=F32)
        p = jnp.exp2(s - jnp.max(s, axis=-1, keepdims=True))
        o = jnp.dot(p.astype(BF16), vv, preferred_element_type=F32)
        o_ref[:, sl] = (o[:, :HEAD_DIM] / o[:, HEAD_DIM:]).astype(o_ref.dtype)


def _attention(proj, tabs, qg, kg, w_up, w_down, batch, seq):
    t = proj.shape[0]
    nq = seq // ATTN_TQ
    gw = Q_PER_KV * HEAD_DIM
    k_blk0 = ATTN_WIDTH // HEAD_DIM
    v_blk0 = (ATTN_WIDTH + KV_WIDTH) // HEAD_DIM
    cos, sin_lo, sin_hi = tabs
    q_tab = pl.BlockSpec((ATTN_TQ, HEAD_DIM), lambda b, g, i: (i, 0))
    k_tab = pl.BlockSpec((seq, HEAD_DIM), lambda b, g, i: (0, 0))
    gain = pl.BlockSpec((1, HEAD_DIM), lambda b, g, i: (0, 0))

    n_steps = batch * N_KV_HEADS * nq
    step = lambda b, g, i: (b * N_KV_HEADS + g) * nq + i
    up_rows = w_up.shape[0] // n_steps
    down_blocks = w_down.shape[0] // CAST_DOWN_ROWS
    assert up_rows * n_steps == w_up.shape[0] and up_rows % BF16_SUBLANES == 0
    assert down_blocks * CAST_DOWN_ROWS == w_down.shape[0] and down_blocks <= n_steps
    up_blk = pl.BlockSpec((up_rows, w_up.shape[1]), lambda b, g, i: (step(b, g, i), 0))
    down_blk = pl.BlockSpec((CAST_DOWN_ROWS, w_down.shape[1]),
                            lambda b, g, i: (jnp.minimum(step(b, g, i), down_blocks - 1), 0))
    return pl.pallas_call(
        functools.partial(_attn_kernel, down_blocks=down_blocks),
        grid=(batch, N_KV_HEADS, nq),
        in_specs=[
            pl.BlockSpec((ATTN_TQ, gw), lambda b, g, i: (b * nq + i, g)),
            pl.BlockSpec((seq, HEAD_DIM), lambda b, g, i: (b, k_blk0 + g)),
            pl.BlockSpec((seq, HEAD_DIM), lambda b, g, i: (b, v_blk0 + g)),
            q_tab, q_tab, q_tab, k_tab, k_tab, k_tab, gain, gain, up_blk, down_blk,
        ],
        out_specs=[pl.BlockSpec((ATTN_TQ, gw), lambda b, g, i: (b * nq + i, g)), up_blk, down_blk],
        out_shape=[jax.ShapeDtypeStruct((t, ATTN_WIDTH), BF16),
                   jax.ShapeDtypeStruct(w_up.shape, BF16), jax.ShapeDtypeStruct(w_down.shape, BF16)],
        scratch_shapes=[pltpu.VMEM((seq, HEAD_DIM), BF16), pltpu.VMEM((seq, 2 * HEAD_DIM), BF16)],
        compiler_params=_params("arbitrary", "arbitrary", "arbitrary"),
        name="attention",
    )(proj, proj, proj, cos, sin_lo, sin_hi, cos, sin_lo, sin_hi, qg, kg, w_up, w_down)


def _fourier_fold_kernel(w_ref, cc_ref, sc_ref, o_ref):
    for g in range(FOURIER_GROUPS):
        w = w_ref[g]
        a = jnp.dot(cc_ref[...], w, preferred_element_type=F32, precision=lax.Precision.HIGHEST)
        b = jnp.dot(sc_ref[...], w, preferred_element_type=F32, precision=lax.Precision.HIGHEST)
        o_ref[g, :, :FOURIER_GROUP_DIM] = a.astype(o_ref.dtype)
        o_ref[g, :, FOURIER_GROUP_DIM:] = b.astype(o_ref.dtype)


def _fourier_fold(w_fmix, cc, sc):
    return pl.pallas_call(
        _fourier_fold_kernel,
        out_shape=jax.ShapeDtypeStruct((FOURIER_GROUPS, FOURIER_GROUP_DIM, 2 * FOURIER_GROUP_DIM), BF16),
        name="fourier_fold",
    )(w_fmix, cc, sc)


def _fourier_kernel(flo_ref, fhi_ref, ab_ref, c_ref, ms_ref, o_ref, za_scr, zb_scr):
    @pl.when(pl.program_id(1) == 0)
    def _():
        half = FOURIER_GROUPS // 2
        for g in range(FOURIER_GROUPS):
            src = flo_ref if g < half else fhi_ref
            lo = (g % half) * FOURIER_GROUP_DIM
            z = jnp.dot(src[:, lo:lo + FOURIER_GROUP_DIM], ab_ref[g], preferred_element_type=F32)
            dst = slice(g * FOURIER_GROUP_DIM, (g + 1) * FOURIER_GROUP_DIM)
            za_scr[:, dst] = z[:, :FOURIER_GROUP_DIM].astype(BF16)
            zb_scr[:, dst] = z[:, FOURIER_GROUP_DIM:].astype(BF16)

    acc = jnp.dot(c_ref[...], za_scr[...], preferred_element_type=F32)
    acc = acc + jnp.dot(ms_ref[...], zb_scr[...], preferred_element_type=F32)
    o_ref[...] = acc.astype(o_ref.dtype)


def _fourier(proj, ab, dft_c, dft_ms, batch, seq):
    t = proj.shape[0]
    nm = seq // FOURIER_TM
    half_w = FOURIER_WIDTH // 2
    f_blk0 = (ATTN_WIDTH + 2 * KV_WIDTH) // half_w
    return pl.pallas_call(
        _fourier_kernel,
        grid=(batch, nm),
        in_specs=[
            pl.BlockSpec((seq, half_w), lambda b, m: (b, f_blk0)),
            pl.BlockSpec((seq, half_w), lambda b, m: (b, f_blk0 + 1)),
            pl.BlockSpec(ab.shape, lambda b, m: (0, 0, 0)),
            pl.BlockSpec((FOURIER_TM, seq), lambda b, m: (m, 0)),
            pl.BlockSpec((FOURIER_TM, seq), lambda b, m: (m, 0)),
        ],
        out_specs=pl.BlockSpec((FOURIER_TM, FOURIER_WIDTH), lambda b, m: (b * nm + m, 0)),
        out_shape=jax.ShapeDtypeStruct((t, FOURIER_WIDTH), BF16),
        scratch_shapes=[pltpu.VMEM((seq, FOURIER_WIDTH), BF16), pltpu.VMEM((seq, FOURIER_WIDTH), BF16)],
        compiler_params=_params("parallel", "arbitrary"),
        name="fourier",
    )(proj, proj, ab, dft_c, dft_ms)


def _out_proj_kernel(a_ref, f_ref, x_ref, ag_ref, fg_ref, w_ref, g2_ref, h_ref, u_ref,
                     mix_scr, h_scr, w_scr):
    m, j = pl.program_id(0), pl.program_id(1)

    @pl.when(j == 0)
    def _():
        mix_scr[:, :ATTN_WIDTH] = (_rms(a_ref[...].astype(F32)) * ag_ref[...]).astype(BF16)
        mix_scr[:, ATTN_WIDTH:] = (_rms(f_ref[...].astype(F32)) * fg_ref[...]).astype(BF16)

    @pl.when(m == 0)
    def _():
        w_scr[j] = w_ref[...].astype(BF16)

    h = x_ref[...] + jnp.dot(mix_scr[...], w_scr[j], preferred_element_type=F32)
    h_ref[...] = h
    h_scr[:, pl.ds(pl.multiple_of(j * OUT_TN, OUT_TN), OUT_TN)] = h

    @pl.when(j == pl.num_programs(1) - 1)
    def _():
        u_ref[...] = (_rms(h_scr[...]) * g2_ref[...]).astype(u_ref.dtype)


def _out_proj(a, f, x2, ag, fg, w, g2):
    t, d = x2.shape
    row = lambda width: pl.BlockSpec((OUT_TM, width), lambda m, j: (m, 0))
    vec = lambda width: pl.BlockSpec((1, width), lambda m, j: (0, 0))
    tile = pl.BlockSpec((OUT_TM, OUT_TN), lambda m, j: (m, j))
    n_tiles = d // OUT_TN
    return pl.pallas_call(
        _out_proj_kernel,
        grid=(t // OUT_TM, n_tiles),
        in_specs=[
            row(ATTN_WIDTH), row(FOURIER_WIDTH), tile, vec(ATTN_WIDTH), vec(FOURIER_WIDTH),
            pl.BlockSpec((w.shape[0], OUT_TN), _weight_tile_once(n_tiles)),
            vec(d),
        ],
        out_specs=[tile, row(d)],
        out_shape=[jax.ShapeDtypeStruct((t, d), F32), jax.ShapeDtypeStruct((t, d), BF16)],
        scratch_shapes=[pltpu.VMEM((OUT_TM, w.shape[0]), BF16), pltpu.VMEM((OUT_TM, d), F32),
                        pltpu.VMEM((n_tiles, w.shape[0], OUT_TN), BF16)],
        compiler_params=_params("arbitrary", "arbitrary"),
        name="out_proj",
    )(a, f, x2, ag, fg, w, g2)


def _ffn_kernel(u_ref, wg_ref, wv_ref, cwg_ref, cwv_ref, cbg_ref, cbv_ref, wd_ref, h_ref, g_ref,
                y_ref, acc_ref, gate_scr, val_scr, *, n_ff):
    j = pl.program_id(1)
    seq = u_ref.shape[0]
    assert sum(FFN_ROW_SPLIT) == seq
    starts = [sum(FFN_ROW_SPLIT[:c]) for c in range(len(FFN_ROW_SPLIT))]

    @pl.when((pl.program_id(0) == 0) & (j == 0))
    def _():
        acc_ref[...] = jnp.zeros_like(acc_ref)
        gate_scr[...] = jnp.zeros_like(gate_scr)
        val_scr[...] = jnp.zeros_like(val_scr)

    @pl.when(j < n_ff)
    def _():
        wg, wv, wd = wg_ref[...], wv_ref[...], wd_ref[...]
        cwg, cwv, cbg, cbv = cwg_ref[...], cwv_ref[...], cbg_ref[...], cbv_ref[...]

        def conv3(scr, cw, cb, r0, rc):
            prev = scr[pl.ds(FFN_PAD - 1 + r0, rc), :]
            cur = scr[pl.ds(FFN_PAD + r0, rc), :]
            nxt = scr[pl.ds(FFN_PAD + 1 + r0, rc), :]
            return prev * cw[0:1, :] + cur * cw[1:2, :] + nxt * cw[2:3, :] + cb

        def gate_and_down(c):
            r0, rc = starts[c], FFN_ROW_SPLIT[c]
            gate = conv3(gate_scr, cwg, cbg, r0, rc)
            val = conv3(val_scr, cwv, cbv, r0, rc)
            act = (gate * jax.nn.sigmoid(gate) * val).astype(BF16)
            acc_ref[pl.ds(r0, rc), :] += jnp.dot(act, wd, preferred_element_type=F32)

        for c, (r0, rc) in enumerate(zip(starts, FFN_ROW_SPLIT)):
            u = u_ref[pl.ds(r0, rc), :]
            gate_scr[pl.ds(FFN_PAD + r0, rc), :] = jnp.dot(u, wg, preferred_element_type=F32)
            val_scr[pl.ds(FFN_PAD + r0, rc), :] = jnp.dot(u, wv, preferred_element_type=F32)
            if c >= 1:
                gate_and_down(c - 1)
        gate_and_down(len(FFN_ROW_SPLIT) - 1)

    @pl.when(j >= n_ff)
    def _():
        rows = pl.ds(pl.multiple_of((j - n_ff) * FFN_OUT_ROWS, FFN_OUT_ROWS), FFN_OUT_ROWS)
        y_ref[...] = _rms(acc_ref[rows, :] + h_ref[...]) * g_ref[...]
        acc_ref[rows, :] = jnp.zeros((FFN_OUT_ROWS, acc_ref.shape[1]), F32)


def _ffn(u2, w_up, conv_w, conv_b, w_down, h1, g, batch, seq):
    t, d = u2.shape
    d_ff = w_down.shape[0]
    n_ff = d_ff // FFN_TN
    n_out = seq // FFN_OUT_ROWS
    taps = conv_w.shape[0]
    ff = lambda j: jnp.minimum(j, n_ff - 1)
    gate_col = lambda b, j: (0, ff(j))
    val_col = lambda b, j: (0, n_ff + ff(j))
    out_row = lambda b, j: (b * n_out + jnp.maximum(j - n_ff, 0), 0)
    return pl.pallas_call(
        functools.partial(_ffn_kernel, n_ff=n_ff),
        grid=(batch, n_ff + n_out),
        in_specs=[
            pl.BlockSpec((seq, d), lambda b, j: (b, 0), pipeline_mode=pl.Buffered(1)),
            pl.BlockSpec((d, FFN_TN), gate_col),
            pl.BlockSpec((d, FFN_TN), val_col),
            pl.BlockSpec((taps, FFN_TN), gate_col),
            pl.BlockSpec((taps, FFN_TN), val_col),
            pl.BlockSpec((1, FFN_TN), gate_col),
            pl.BlockSpec((1, FFN_TN), val_col),
            pl.BlockSpec((FFN_TN, d), lambda b, j: (ff(j), 0)),
            pl.BlockSpec((FFN_OUT_ROWS, d), out_row),
            pl.BlockSpec((1, d), lambda b, j: (0, 0)),
        ],
        out_specs=pl.BlockSpec((FFN_OUT_ROWS, d), out_row),
        out_shape=jax.ShapeDtypeStruct((t, d), F32),
        scratch_shapes=[
            pltpu.VMEM((seq, d), F32),
            pltpu.VMEM((seq + 2 * FFN_PAD, FFN_TN), F32),
            pltpu.VMEM((seq + 2 * FFN_PAD, FFN_TN), F32),
        ],
        compiler_params=pltpu.CompilerParams(dimension_semantics=("arbitrary", "arbitrary"),
                                             vmem_limit_bytes=FFN_VMEM_LIMIT_BYTES),
        name="ffn",
    )(u2, w_up, w_up, conv_w, conv_w, conv_b, conv_b, w_down, h1, g)


def _rope_tables(seq):
    t = np.arange(seq)
    inv_freq = ROPE_THETA ** (-np.arange(ROPE_PAIRS, dtype=np.float64) / ROPE_PAIRS)

    def axis_tables(pos):
        ang = pos[:, None] * inv_freq[None, :]
        ang = np.concatenate([ang, ang], axis=-1)
        return np.cos(ang), np.sin(ang)

    cos_r, sin_r = axis_tables(t // GRID_W)
    cos_c, sin_c = axis_tables(t % GRID_W)
    cos = np.concatenate([cos_r, cos_c], axis=-1)
    sin = np.concatenate([sin_r, sin_c], axis=-1)
    first_half = (np.arange(HEAD_DIM) % (2 * ROPE_PAIRS)) < ROPE_PAIRS
    sin_lo = np.where(first_half[None, :], -sin, 0.0)
    sin_hi = np.where(first_half[None, :], 0.0, sin)
    return tuple(jnp.asarray(tab, F32) for tab in (cos, sin_lo, sin_hi))


def _dft_cos_sin(rows, n_cols, n, scale=1.0):
    ang = ((rows[:, None] * np.arange(n_cols)[None, :]) % n) * (2.0 * math.pi / n)
    return jnp.asarray(np.cos(ang) * scale, F32), jnp.asarray(np.sin(ang) * scale, F32)


def _dft_tables(n, scale):
    return _dft_cos_sin(np.arange(n), n, n, scale)


def _seq_dft_kernel(ca_ref, sa_ref, cr_ref, sr_ref, c_ref, ms_ref):
    cr, sr = cr_ref[...], sr_ref[...]
    for i in range(DFT_COARSE_PER_STEP):
        ca, sa = ca_ref[i:i + 1, :], sa_ref[i:i + 1, :]
        rows = slice(i * DFT_SUB, (i + 1) * DFT_SUB)
        c_ref[rows, :] = (ca * cr - sa * sr).astype(c_ref.dtype)
        ms_ref[rows, :] = (-(sa * cr + ca * sr)).astype(ms_ref.dtype)


def _seq_dft_tables(n):
    ca, sa = _dft_cos_sin(np.arange(n // DFT_SUB) * DFT_SUB, n, n)
    cr, sr = _dft_cos_sin(np.arange(DFT_SUB), n, n)
    steps = n // (DFT_SUB * DFT_COARSE_PER_STEP)
    coarse_spec = pl.BlockSpec((DFT_COARSE_PER_STEP, n), lambda i: (i, 0))
    fine_spec = pl.BlockSpec((DFT_SUB, n), lambda i: (0, 0))
    out_spec = pl.BlockSpec((DFT_SUB * DFT_COARSE_PER_STEP, n), lambda i: (i, 0))
    return pl.pallas_call(
        _seq_dft_kernel,
        grid=(steps,),
        in_specs=[coarse_spec, coarse_spec, fine_spec, fine_spec],
        out_specs=[out_spec, out_spec],
        out_shape=[jax.ShapeDtypeStruct((n, n), BF16)] * 2,
        compiler_params=_params("parallel"),
        name="seq_dft_tables",
    )(ca, sa, cr, sr)


def kernel(x, norm1_g, w_in, q_norm_g, k_norm_g, w_fmix, attn_out_g, fourier_out_g, w_out,
           norm2_g, w_up, conv_w, conv_b, w_down, final_g):
    batch, seq, d = x.shape
    depth = w_in.shape[0]
    rope = _rope_tables(seq)
    dft_c, dft_ms = _seq_dft_tables(seq)
    cc, sc = _dft_tables(FOURIER_GROUP_DIM, 1.0 / math.sqrt(seq * FOURIER_GROUP_DIM))

    assert depth == 1
    h = x.reshape(batch * seq, d)
    for l in range(depth):
        proj = _in_proj(h, norm1_g[l][None], w_in[l])
        attn, w_up_bf16, w_down_bf16 = _attention(proj, rope, q_norm_g[l][None], k_norm_g[l][None],
                                                  w_up[l], w_down[l], batch, seq)
        ab = _fourier_fold(w_fmix[l], cc, sc)
        four = _fourier(proj, ab, dft_c, dft_ms, batch, seq)
        h1, u2 = _out_proj(attn, four, h, attn_out_g[l][None], fourier_out_g[l][None],
                           w_out[l], norm2_g[l][None])
        h = _ffn(u2, w_up_bf16, conv_w[l], conv_b[l][None], w_down_bf16, h1, final_g[None], batch, seq)
    return h.reshape(batch, seq, d)
```

```python
import functools
import math

import jax
import jax.numpy as jnp
import numpy as np
from jax import lax
from jax.experimental import pallas as pl
from jax.experimental.pallas import tpu as pltpu

F32 = jnp.float32
BF16 = jnp.bfloat16

HEAD_DIM = 128
N_Q_HEADS = 8
N_KV_HEADS = 2
Q_PER_KV = N_Q_HEADS // N_KV_HEADS
ATTN_WIDTH = N_Q_HEADS * HEAD_DIM
KV_WIDTH = N_KV_HEADS * HEAD_DIM
FOURIER_GROUPS = 8
FOURIER_GROUP_DIM = 128
FOURIER_WIDTH = FOURIER_GROUPS * FOURIER_GROUP_DIM
GRID_W = 64
ROPE_THETA = 10000.0
ROPE_PAIRS = HEAD_DIM // 4
EPS = 1e-6

VMEM_LIMIT_BYTES = 56 * 1024 * 1024

IN_TM, IN_TN = 1024, 512
ATTN_TQ, ATTN_ROWS = 1024, 256
BF16_SUBLANES = 16
CAST_DOWN_ROWS = 352
FOURIER_TM = 512
DFT_SUB, DFT_COARSE_PER_STEP = 64, 8
OUT_TM, OUT_TN = 1024, 512
FFN_TN = 512
FFN_ROW_SPLIT = (1280, 768)
FFN_PAD = 8
FFN_OUT_ROWS = 256
FFN_VMEM_LIMIT_BYTES = 60 * 1024 * 1024


def _params(*sem):
    return pltpu.CompilerParams(dimension_semantics=sem, vmem_limit_bytes=VMEM_LIMIT_BYTES)


def _rms(x):
    return x * lax.rsqrt(jnp.mean(x * x, axis=-1, keepdims=True) + EPS)


def _weight_tile_once(n_tiles):
    return lambda m, j: (0, jnp.where(m == 0, j, n_tiles - 1))


def _in_proj_kernel(x_ref, g_ref, w_ref, o_ref, u_scr, w_scr):
    m, j = pl.program_id(0), pl.program_id(1)

    @pl.when(j == 0)
    def _():
        u_scr[...] = (_rms(x_ref[...]) * g_ref[...]).astype(BF16)

    @pl.when(m == 0)
    def _():
        w_scr[j] = w_ref[...].astype(BF16)

    o_ref[...] = jnp.dot(u_scr[...], w_scr[j], preferred_element_type=F32).astype(o_ref.dtype)


def _in_proj(x2, g, w):
    t, d = x2.shape
    n = w.shape[1]
    n_tiles = n // IN_TN
    return pl.pallas_call(
        _in_proj_kernel,
        grid=(t // IN_TM, n_tiles),
        in_specs=[
            pl.BlockSpec((IN_TM, d), lambda m, j: (m, 0)),
            pl.BlockSpec((1, d), lambda m, j: (0, 0)),
            pl.BlockSpec((d, IN_TN), _weight_tile_once(n_tiles)),
        ],
        out_specs=pl.BlockSpec((IN_TM, IN_TN), lambda m, j: (m, j)),
        out_shape=jax.ShapeDtypeStruct((t, n), BF16),
        scratch_shapes=[pltpu.VMEM((IN_TM, d), BF16), pltpu.VMEM((n_tiles, d, IN_TN), BF16)],
        compiler_params=_params("arbitrary", "arbitrary"),
        name="in_proj",
    )(x2, g, w)


def _rope(x, cos, sin_lo, sin_hi):
    return (x * cos
            + pltpu.roll(x, HEAD_DIM - ROPE_PAIRS, axis=1) * sin_lo
            + pltpu.roll(x, ROPE_PAIRS, axis=1) * sin_hi)


def _attn_kernel(q_ref, k_ref, v_ref, cq_ref, slq_ref, shq_ref, ck_ref, slk_ref, shk_ref,
                 qg_ref, kg_ref, wu_ref, wd_ref, o_ref, wu_out, wd_out, k_scr, v_scr, *, down_blocks):
    step = ((pl.program_id(0) * pl.num_programs(1) + pl.program_id(1)) * pl.num_programs(2)
            + pl.program_id(2))
    wu_out[...] = wu_ref[...].astype(BF16)

    @pl.when(step < down_blocks)
    def _():
        wd_out[...] = wd_ref[...].astype(BF16)

    @pl.when(pl.program_id(2) == 0)
    def _():
        k = _rms(k_ref[...].astype(F32)) * kg_ref[...]
        k_scr[...] = _rope(k, ck_ref[...], slk_ref[...], shk_ref[...]).astype(BF16)
        v_scr[:, :HEAD_DIM] = v_ref[...]
        v_scr[:, HEAD_DIM:] = jnp.ones((v_ref.shape[0], HEAD_DIM), BF16)

    scale = math.log2(math.e) / math.sqrt(HEAD_DIM)
    cos, sin_lo, sin_hi = cq_ref[...], slq_ref[...], shq_ref[...]
    kk = k_scr[...]
    vv = v_scr[...]
    for r in range(ATTN_TQ // ATTN_ROWS):
        rows = slice(r * ATTN_ROWS, (r + 1) * ATTN_ROWS)
        for h in range(Q_PER_KV):
            sl = slice(h * HEAD_DIM, (h + 1) * HEAD_DIM)
            q = _rms(q_ref[rows, sl].astype(F32)) * qg_ref[...]
            q = (_rope(q, cos[rows], sin_lo[rows], sin_hi[rows]) * scale).astype(BF16)
            s = lax.dot_general(q, kk, (((1,), (1,)), ((), ())), preferred_element_type=F32)
            p = jnp.exp2(s - jnp.max(s, axis=-1, keepdims=True))
            o = jnp.dot(p.astype(BF16), vv, preferred_element_type=F32)
            o_ref[rows, sl] = (o[:, :HEAD_DIM] / o[:, HEAD_DIM:]).astype(o_ref.dtype)


def _attention(proj, tabs, qg, kg, w_up, w_down, batch, seq):
    t = proj.shape[0]
    nq = seq // ATTN_TQ
    gw = Q_PER_KV * HEAD_DIM
    k_blk0 = ATTN_WIDTH // HEAD_DIM
    v_blk0 = (ATTN_WIDTH + KV_WIDTH) // HEAD_DIM
    cos, sin_lo, sin_hi = tabs
    q_tab = pl.BlockSpec((ATTN_TQ, HEAD_DIM), lambda b, g, i: (i, 0))
    k_tab = pl.BlockSpec((seq, HEAD_DIM), lambda b, g, i: (0, 0))
    gain = pl.BlockSpec((1, HEAD_DIM), lambda b, g, i: (0, 0))

    n_steps = batch * N_KV_HEADS * nq
    step = lambda b, g, i: (b * N_KV_HEADS + g) * nq + i
    up_rows = w_up.shape[0] // n_steps
    down_blocks = w_down.shape[0] // CAST_DOWN_ROWS
    assert up_rows * n_steps == w_up.shape[0] and up_rows % BF16_SUBLANES == 0
    assert down_blocks * CAST_DOWN_ROWS == w_down.shape[0] and down_blocks <= n_steps
    up_blk = pl.BlockSpec((up_rows, w_up.shape[1]), lambda b, g, i: (step(b, g, i), 0))
    down_blk = pl.BlockSpec((CAST_DOWN_ROWS, w_down.shape[1]),
                            lambda b, g, i: (jnp.minimum(step(b, g, i), down_blocks - 1), 0))
    return pl.pallas_call(
        functools.partial(_attn_kernel, down_blocks=down_blocks),
        grid=(batch, N_KV_HEADS, nq),
        in_specs=[
            pl.BlockSpec((ATTN_TQ, gw), lambda b, g, i: (b * nq + i, g)),
            pl.BlockSpec((seq, HEAD_DIM), lambda b, g, i: (b, k_blk0 + g)),
            pl.BlockSpec((seq, HEAD_DIM), lambda b, g, i: (b, v_blk0 + g)),
            q_tab, q_tab, q_tab, k_tab, k_tab, k_tab, gain, gain, up_blk, down_blk,
        ],
        out_specs=[pl.BlockSpec((ATTN_TQ, gw), lambda b, g, i: (b * nq + i, g)), up_blk, down_blk],
        out_shape=[jax.ShapeDtypeStruct((t, ATTN_WIDTH), BF16),
                   jax.ShapeDtypeStruct(w_up.shape, BF16), jax.ShapeDtypeStruct(w_down.shape, BF16)],
        scratch_shapes=[pltpu.VMEM((seq, HEAD_DIM), BF16), pltpu.VMEM((seq, 2 * HEAD_DIM), BF16)],
        compiler_params=_params("arbitrary", "arbitrary", "arbitrary"),
        name="attention",
    )(proj, proj, proj, cos, sin_lo, sin_hi, cos, sin_lo, sin_hi, qg, kg, w_up, w_down)


def _fourier_fold_kernel(w_ref, cc_ref, sc_ref, o_ref):
    for g in range(FOURIER_GROUPS):
        w = w_ref[g]
        a = jnp.dot(cc_ref[...], w, preferred_element_type=F32, precision=lax.Precision.HIGHEST)
        b = jnp.dot(sc_ref[...], w, preferred_element_type=F32, precision=lax.Precision.HIGHEST)
        o_ref[g, :, :FOURIER_GROUP_DIM] = a.astype(o_ref.dtype)
        o_ref[g, :, FOURIER_GROUP_DIM:] = b.astype(o_ref.dtype)


def _fourier_fold(w_fmix, cc, sc):
    return pl.pallas_call(
        _fourier_fold_kernel,
        out_shape=jax.ShapeDtypeStruct((FOURIER_GROUPS, FOURIER_GROUP_DIM, 2 * FOURIER_GROUP_DIM), BF16),
        name="fourier_fold",
    )(w_fmix, cc, sc)


def _fourier_kernel(flo_ref, fhi_ref, ab_ref, c_ref, ms_ref, o_ref, za_scr, zb_scr):
    @pl.when(pl.program_id(1) == 0)
    def _():
        half = FOURIER_GROUPS // 2
        for g in range(FOURIER_GROUPS):
            src = flo_ref if g < half else fhi_ref
            lo = (g % half) * FOURIER_GROUP_DIM
            z = jnp.dot(src[:, lo:lo + FOURIER_GROUP_DIM], ab_ref[g], preferred_element_type=F32)
            dst = slice(g * FOURIER_GROUP_DIM, (g + 1) * FOURIER_GROUP_DIM)
            za_scr[:, dst] = z[:, :FOURIER_GROUP_DIM].astype(BF16)
            zb_scr[:, dst] = z[:, FOURIER_GROUP_DIM:].astype(BF16)

    acc = jnp.dot(c_ref[...], za_scr[...], preferred_element_type=F32)
    acc = acc + jnp.dot(ms_ref[...], zb_scr[...], preferred_element_type=F32)
    o_ref[...] = acc.astype(o_ref.dtype)


def _fourier(proj, ab, dft_c, dft_ms, batch, seq):
    t = proj.shape[0]
    nm = seq // FOURIER_TM
    half_w = FOURIER_WIDTH // 2
    f_blk0 = (ATTN_WIDTH + 2 * KV_WIDTH) // half_w
    return pl.pallas_call(
        _fourier_kernel,
        grid=(batch, nm),
        in_specs=[
            pl.BlockSpec((seq, half_w), lambda b, m: (b, f_blk0)),
            pl.BlockSpec((seq, half_w), lambda b, m: (b, f_blk0 + 1)),
            pl.BlockSpec(ab.shape, lambda b, m: (0, 0, 0)),
            pl.BlockSpec((FOURIER_TM, seq), lambda b, m: (m, 0)),
            pl.BlockSpec((FOURIER_TM, seq), lambda b, m: (m, 0)),
        ],
        out_specs=pl.BlockSpec((FOURIER_TM, FOURIER_WIDTH), lambda b, m: (b * nm + m, 0)),
        out_shape=jax.ShapeDtypeStruct((t, FOURIER_WIDTH), BF16),
        scratch_shapes=[pltpu.VMEM((seq, FOURIER_WIDTH), BF16), pltpu.VMEM((seq, FOURIER_WIDTH), BF16)],
        compiler_params=_params("parallel", "arbitrary"),
        name="fourier",
    )(proj, proj, ab, dft_c, dft_ms)


def _out_proj_kernel(a_ref, f_ref, x_ref, ag_ref, fg_ref, w_ref, g2_ref, h_ref, u_ref,
                     mix_scr, h_scr, w_scr):
    m, j = pl.program_id(0), pl.program_id(1)

    @pl.when(j == 0)
    def _():
        mix_scr[:, :ATTN_WIDTH] = (_rms(a_ref[...].astype(F32)) * ag_ref[...]).astype(BF16)
        mix_scr[:, ATTN_WIDTH:] = (_rms(f_ref[...].astype(F32)) * fg_ref[...]).astype(BF16)

    @pl.when(m == 0)
    def _():
        w_scr[j] = w_ref[...].astype(BF16)

    h = x_ref[...] + jnp.dot(mix_scr[...], w_scr[j], preferred_element_type=F32)
    h_ref[...] = h
    h_scr[:, pl.ds(pl.multiple_of(j * OUT_TN, OUT_TN), OUT_TN)] = h

    @pl.when(j == pl.num_programs(1) - 1)
    def _():
        u_ref[...] = (_rms(h_scr[...]) * g2_ref[...]).astype(u_ref.dtype)


def _out_proj(a, f, x2, ag, fg, w, g2):
    t, d = x2.shape
    row = lambda width: pl.BlockSpec((OUT_TM, width), lambda m, j: (m, 0))
    vec = lambda width: pl.BlockSpec((1, width), lambda m, j: (0, 0))
    tile = pl.BlockSpec((OUT_TM, OUT_TN), lambda m, j: (m, j))
    n_tiles = d // OUT_TN
    return pl.pallas_call(
        _out_proj_kernel,
        grid=(t // OUT_TM, n_tiles),
        in_specs=[
            row(ATTN_WIDTH), row(FOURIER_WIDTH), tile, vec(ATTN_WIDTH), vec(FOURIER_WIDTH),
            pl.BlockSpec((w.shape[0], OUT_TN), _weight_tile_once(n_tiles)),
            vec(d),
        ],
        out_specs=[tile, row(d)],
        out_shape=[jax.ShapeDtypeStruct((t, d), F32), jax.ShapeDtypeStruct((t, d), BF16)],
        scratch_shapes=[pltpu.VMEM((OUT_TM, w.shape[0]), BF16), pltpu.VMEM((OUT_TM, d), F32),
                        pltpu.VMEM((n_tiles, w.shape[0], OUT_TN), BF16)],
        compiler_params=_params("arbitrary", "arbitrary"),
        name="out_proj",
    )(a, f, x2, ag, fg, w, g2)


def _ffn_kernel(u_ref, wg_ref, wv_ref, cwg_ref, cwv_ref, cbg_ref, cbv_ref, wd_ref, h_ref, g_ref,
                y_ref, acc_ref, gate_scr, val_scr, *, n_ff):
    j = pl.program_id(1)
    seq = u_ref.shape[0]
    assert sum(FFN_ROW_SPLIT) == seq
    starts = [sum(FFN_ROW_SPLIT[:c]) for c in range(len(FFN_ROW_SPLIT))]

    @pl.when((pl.program_id(0) == 0) & (j == 0))
    def _():
        acc_ref[...] = jnp.zeros_like(acc_ref)
        gate_scr[...] = jnp.zeros_like(gate_scr)
        val_scr[...] = jnp.zeros_like(val_scr)

    @pl.when(j < n_ff)
    def _():
        wg, wv, wd = wg_ref[...], wv_ref[...], wd_ref[...]
        cwg, cwv, cbg, cbv = cwg_ref[...], cwv_ref[...], cbg_ref[...], cbv_ref[...]

        def conv3(scr, cw, cb, r0, rc):
            prev = scr[pl.ds(FFN_PAD - 1 + r0, rc), :]
            cur = scr[pl.ds(FFN_PAD + r0, rc), :]
            nxt = scr[pl.ds(FFN_PAD + 1 + r0, rc), :]
            return prev * cw[0:1, :] + cur * cw[1:2, :] + nxt * cw[2:3, :] + cb

        def gate_and_down(c):
            r0, rc = starts[c], FFN_ROW_SPLIT[c]
            gate = conv3(gate_scr, cwg, cbg, r0, rc)
            val = conv3(val_scr, cwv, cbv, r0, rc)
            act = (gate * jax.nn.sigmoid(gate) * val).astype(BF16)
            acc_ref[pl.ds(r0, rc), :] += jnp.dot(act, wd, preferred_element_type=F32)

        for c, (r0, rc) in enumerate(zip(starts, FFN_ROW_SPLIT)):
            u = u_ref[pl.ds(r0, rc), :]
            gate_scr[pl.ds(FFN_PAD + r0, rc), :] = jnp.dot(u, wg, preferred_element_type=F32)
            val_scr[pl.ds(FFN_PAD + r0, rc), :] = jnp.dot(u, wv, preferred_element_type=F32)
            if c >= 1:
                gate_and_down(c - 1)
        gate_and_down(len(FFN_ROW_SPLIT) - 1)

    @pl.when(j >= n_ff)
    def _():
        rows = pl.ds(pl.multiple_of((j - n_ff) * FFN_OUT_ROWS, FFN_OUT_ROWS), FFN_OUT_ROWS)
        y_ref[...] = _rms(acc_ref[rows, :] + h_ref[...]) * g_ref[...]
        acc_ref[rows, :] = jnp.zeros((FFN_OUT_ROWS, acc_ref.shape[1]), F32)


def _ffn(u2, w_up, conv_w, conv_b, w_down, h1, g, batch, seq):
    t, d = u2.shape
    d_ff = w_down.shape[0]
    n_ff = d_ff // FFN_TN
    n_out = seq // FFN_OUT_ROWS
    taps = conv_w.shape[0]
    ff = lambda j: jnp.minimum(j, n_ff - 1)
    gate_col = lambda b, j: (0, ff(j))
    val_col = lambda b, j: (0, n_ff + ff(j))
    out_row = lambda b, j: (b * n_out + jnp.maximum(j - n_ff, 0), 0)
    return pl.pallas_call(
        functools.partial(_ffn_kernel, n_ff=n_ff),
        grid=(batch, n_ff + n_out),
        in_specs=[
            pl.BlockSpec((seq, d), lambda b, j: (b, 0), pipeline_mode=pl.Buffered(1)),
            pl.BlockSpec((d, FFN_TN), gate_col),
            pl.BlockSpec((d, FFN_TN), val_col),
            pl.BlockSpec((taps, FFN_TN), gate_col),
            pl.BlockSpec((taps, FFN_TN), val_col),
            pl.BlockSpec((1, FFN_TN), gate_col),
            pl.BlockSpec((1, FFN_TN), val_col),
            pl.BlockSpec((FFN_TN, d), lambda b, j: (ff(j), 0)),
            pl.BlockSpec((FFN_OUT_ROWS, d), out_row),
            pl.BlockSpec((1, d), lambda b, j: (0, 0)),
        ],
        out_specs=pl.BlockSpec((FFN_OUT_ROWS, d), out_row),
        out_shape=jax.ShapeDtypeStruct((t, d), F32),
        scratch_shapes=[
            pltpu.VMEM((seq, d), F32),
            pltpu.VMEM((seq + 2 * FFN_PAD, FFN_TN), F32),
            pltpu.VMEM((seq + 2 * FFN_PAD, FFN_TN), F32),
        ],
        compiler_params=pltpu.CompilerParams(dimension_semantics=("arbitrary", "arbitrary"),
                                             vmem_limit_bytes=FFN_VMEM_LIMIT_BYTES),
        name="ffn",
    )(u2, w_up, w_up, conv_w, conv_w, conv_b, conv_b, w_down, h1, g)


def _rope_tables(seq):
    t = np.arange(seq)
    inv_freq = ROPE_THETA ** (-np.arange(ROPE_PAIRS, dtype=np.float64) / ROPE_PAIRS)

    def axis_tables(pos):
        ang = pos[:, None] * inv_freq[None, :]
        ang = np.concatenate([ang, ang], axis=-1)
        return np.cos(ang), np.sin(ang)

    cos_r, sin_r = axis_tables(t // GRID_W)
    cos_c, sin_c = axis_tables(t % GRID_W)
    cos = np.concatenate([cos_r, cos_c], axis=-1)
    sin = np.concatenate([sin_r, sin_c], axis=-1)
    first_half = (np.arange(HEAD_DIM) % (2 * ROPE_PAIRS)) < ROPE_PAIRS
    sin_lo = np.where(first_half[None, :], -sin, 0.0)
    sin_hi = np.where(first_half[None, :], 0.0, sin)
    return tuple(jnp.asarray(tab, F32) for tab in (cos, sin_lo, sin_hi))


def _dft_cos_sin(rows, n_cols, n, scale=1.0):
    ang = ((rows[:, None] * np.arange(n_cols)[None, :]) % n) * (2.0 * math.pi / n)
    return jnp.asarray(np.cos(ang) * scale, F32), jnp.asarray(np.sin(ang) * scale, F32)


def _dft_tables(n, scale):
    return _dft_cos_sin(np.arange(n), n, n, scale)


def _seq_dft_kernel(ca_ref, sa_ref, cr_ref, sr_ref, c_ref, ms_ref):
    cr, sr = cr_ref[...], sr_ref[...]
    for i in range(DFT_COARSE_PER_STEP):
        ca, sa = ca_ref[i:i + 1, :], sa_ref[i:i + 1, :]
        rows = slice(i * DFT_SUB, (i + 1) * DFT_SUB)
        c_ref[rows, :] = (ca * cr - sa * sr).astype(c_ref.dtype)
        ms_ref[rows, :] = (-(sa * cr + ca * sr)).astype(ms_ref.dtype)


def _seq_dft_tables(n):
    ca, sa = _dft_cos_sin(np.arange(n // DFT_SUB) * DFT_SUB, n, n)
    cr, sr = _dft_cos_sin(np.arange(DFT_SUB), n, n)
    steps = n // (DFT_SUB * DFT_COARSE_PER_STEP)
    coarse_spec = pl.BlockSpec((DFT_COARSE_PER_STEP, n), lambda i: (i, 0))
    fine_spec = pl.BlockSpec((DFT_SUB, n), lambda i: (0, 0))
    out_spec = pl.BlockSpec((DFT_SUB * DFT_COARSE_PER_STEP, n), lambda i: (i, 0))
    return pl.pallas_call(
        _seq_dft_kernel,
        grid=(steps,),
        in_specs=[coarse_spec, coarse_spec, fine_spec, fine_spec],
        out_specs=[out_spec, out_spec],
        out_shape=[jax.ShapeDtypeStruct((n, n), BF16)] * 2,
        compiler_params=_params("parallel"),
        name="seq_dft_tables",
    )(ca, sa, cr, sr)


def kernel(x, norm1_g, w_in, q_norm_g, k_norm_g, w_fmix, attn_out_g, fourier_out_g, w_out,
           norm2_g, w_up, conv_w, conv_b, w_down, final_g):
    batch, seq, d = x.shape
    depth = w_in.shape[0]
    rope = _rope_tables(seq)
    dft_c, dft_ms = _seq_dft_tables(seq)
    cc, sc = _dft_tables(FOURIER_GROUP_DIM, 1.0 / math.sqrt(seq * FOURIER_GROUP_DIM))

    assert depth == 1
    h = x.reshape(batch * seq, d)
    for l in range(depth):
        proj = _in_proj(h, norm1_g[l][None], w_in[l])
        attn, w_up_bf16, w_down_bf16 = _attention(proj, rope, q_norm_g[l][None], k_norm_g[l][None],
                                                  w_up[l], w_down[l], batch, seq)
        ab = _fourier_fold(w_fmix[l], cc, sc)
        four = _fourier(proj, ab, dft_c, dft_ms, batch, seq)
        h1, u2 = _out_proj(attn, four, h, attn_out_g[l][None], fourier_out_g[l][None],
                           w_out[l], norm2_g[l][None])
        h = _ffn(u2, w_up_bf16, conv_w[l], conv_b[l][None], w_down_bf16, h1, final_g[None], batch, seq)
    return h.reshape(batch, seq, d)
```

```python
import functools
import math

import jax
import jax.numpy as jnp
import numpy as np
from jax import lax
from jax.experimental import pallas as pl
from jax.experimental.pallas import tpu as pltpu

F32 = jnp.float32
BF16 = jnp.bfloat16

HEAD_DIM = 128
N_Q_HEADS = 8
N_KV_HEADS = 2
Q_PER_KV = N_Q_HEADS // N_KV_HEADS
ATTN_WIDTH = N_Q_HEADS * HEAD_DIM
KV_WIDTH = N_KV_HEADS * HEAD_DIM
FOURIER_GROUPS = 8
FOURIER_GROUP_DIM = 128
FOURIER_WIDTH = FOURIER_GROUPS * FOURIER_GROUP_DIM
GRID_W = 64
ROPE_THETA = 10000.0
ROPE_PAIRS = HEAD_DIM // 4
EPS = 1e-6

VMEM_LIMIT_BYTES = 56 * 1024 * 1024

IN_TM, IN_TN, IN_ROWS = 1024, 512, 512
ATTN_TQ, ATTN_ROWS = 1024, 256
BF16_SUBLANES = 16
CAST_DOWN_ROWS = 352
FOURIER_TM = 512
DFT_SUB, DFT_COARSE_PER_STEP = 64, 8
OUT_TM, OUT_TN = 1024, 512
FFN_TN = 512
FFN_ROW_SPLIT = (1280, 768)
FFN_PAD = 8
FFN_OUT_ROWS = 256
FFN_VMEM_LIMIT_BYTES = 60 * 1024 * 1024


def _params(*sem):
    return pltpu.CompilerParams(dimension_semantics=sem, vmem_limit_bytes=VMEM_LIMIT_BYTES)


def _rms(x):
    return x * lax.rsqrt(jnp.mean(x * x, axis=-1, keepdims=True) + EPS)


def _weight_tile_once(n_tiles):
    return lambda m, j: (0, jnp.where(m == 0, j, n_tiles - 1))


def _in_proj_kernel(x_ref, g_ref, w_ref, o_ref, u_scr, w_scr, *, n_tiles):
    s = pl.program_id(0)

    def normed(rows):
        return (_rms(x_ref[rows, :]) * g_ref[...]).astype(BF16)

    @pl.when(s == 0)
    def _():
        u_scr[...] = normed(slice(None))

    @pl.when(s < n_tiles)
    def _():
        w_scr[s] = w_ref[...].astype(BF16)
        cols = pl.ds(pl.multiple_of(s * IN_TN, IN_TN), IN_TN)
        o_ref[:, cols] = jnp.dot(u_scr[...], w_scr[s], preferred_element_type=F32).astype(o_ref.dtype)

    @pl.when(s >= n_tiles)
    def _():
        for c in range(IN_TM // IN_ROWS):
            rows = slice(c * IN_ROWS, (c + 1) * IN_ROWS)
            u = normed(rows)
            for t in range(n_tiles):
                o_ref[rows, t * IN_TN:(t + 1) * IN_TN] = jnp.dot(
                    u, w_scr[t], preferred_element_type=F32).astype(o_ref.dtype)


def _in_proj(x2, g, w):
    t, d = x2.shape
    n = w.shape[1]
    n_tiles = n // IN_TN
    row_block = lambda s: (jnp.maximum(s - (n_tiles - 1), 0), 0)
    return pl.pallas_call(
        functools.partial(_in_proj_kernel, n_tiles=n_tiles),
        grid=(n_tiles + t // IN_TM - 1,),
        in_specs=[
            pl.BlockSpec((IN_TM, d), row_block),
            pl.BlockSpec((1, d), lambda s: (0, 0)),
            pl.BlockSpec((d, IN_TN), lambda s: (0, jnp.minimum(s, n_tiles - 1))),
        ],
        out_specs=pl.BlockSpec((IN_TM, n), row_block),
        out_shape=jax.ShapeDtypeStruct((t, n), BF16),
        scratch_shapes=[pltpu.VMEM((IN_TM, d), BF16), pltpu.VMEM((n_tiles, d, IN_TN), BF16)],
        compiler_params=_params("arbitrary"),
        name="in_proj",
    )(x2, g, w)


def _rope(x, cos, sin_lo, sin_hi):
    return (x * cos
            + pltpu.roll(x, HEAD_DIM - ROPE_PAIRS, axis=1) * sin_lo
            + pltpu.roll(x, ROPE_PAIRS, axis=1) * sin_hi)


def _attn_kernel(q_ref, k_ref, v_ref, cq_ref, slq_ref, shq_ref, ck_ref, slk_ref, shk_ref,
                 qg_ref, kg_ref, wu_ref, wd_ref, o_ref, wu_out, wd_out, k_scr, v_scr, *, down_blocks):
    step = ((pl.program_id(0) * pl.num_programs(1) + pl.program_id(1)) * pl.num_programs(2)
            + pl.program_id(2))
    wu_out[...] = wu_ref[...].astype(BF16)

    @pl.when(step < down_blocks)
    def _():
        wd_out[...] = wd_ref[...].astype(BF16)

    @pl.when(pl.program_id(2) == 0)
    def _():
        k = _rms(k_ref[...].astype(F32)) * kg_ref[...]
        k_scr[...] = _rope(k, ck_ref[...], slk_ref[...], shk_ref[...]).astype(BF16)
        v_scr[:, :HEAD_DIM] = v_ref[...]
        v_scr[:, HEAD_DIM:] = jnp.ones((v_ref.shape[0], HEAD_DIM), BF16)

    scale = math.log2(math.e) / math.sqrt(HEAD_DIM)
    cos, sin_lo, sin_hi = cq_ref[...], slq_ref[...], shq_ref[...]
    kk = k_scr[...]
    vv = v_scr[...]
    for r in range(ATTN_TQ // ATTN_ROWS):
        rows = slice(r * ATTN_ROWS, (r + 1) * ATTN_ROWS)
        for h in range(Q_PER_KV):
            sl = slice(h * HEAD_DIM, (h + 1) * HEAD_DIM)
            q = _rms(q_ref[rows, sl].astype(F32)) * qg_ref[...]
            q = (_rope(q, cos[rows], sin_lo[rows], sin_hi[rows]) * scale).astype(BF16)
            s = lax.dot_general(q, kk, (((1,), (1,)), ((), ())), preferred_element_type=F32)
            p = jnp.exp2(s - jnp.max(s, axis=-1, keepdims=True))
            o = jnp.dot(p.astype(BF16), vv, preferred_element_type=F32)
            o_ref[rows, sl] = (o[:, :HEAD_DIM] / o[:, HEAD_DIM:]).astype(o_ref.dtype)


def _attention(proj, tabs, qg, kg, w_up, w_down, batch, seq):
    t = proj.shape[0]
    nq = seq // ATTN_TQ
    gw = Q_PER_KV * HEAD_DIM
    k_blk0 = ATTN_WIDTH // HEAD_DIM
    v_blk0 = (ATTN_WIDTH + KV_WIDTH) // HEAD_DIM
    cos, sin_lo, sin_hi = tabs
    q_tab = pl.BlockSpec((ATTN_TQ, HEAD_DIM), lambda b, g, i: (i, 0))
    k_tab = pl.BlockSpec((seq, HEAD_DIM), lambda b, g, i: (0, 0))
    gain = pl.BlockSpec((1, HEAD_DIM), lambda b, g, i: (0, 0))

    n_steps = batch * N_KV_HEADS * nq
    step = lambda b, g, i: (b * N_KV_HEADS + g) * nq + i
    up_rows = w_up.shape[0] // n_steps
    down_blocks = w_down.shape[0] // CAST_DOWN_ROWS
    assert up_rows * n_steps == w_up.shape[0] and up_rows % BF16_SUBLANES == 0
    assert down_blocks * CAST_DOWN_ROWS == w_down.shape[0] and down_blocks <= n_steps
    up_blk = pl.BlockSpec((up_rows, w_up.shape[1]), lambda b, g, i: (step(b, g, i), 0))
    down_blk = pl.BlockSpec((CAST_DOWN_ROWS, w_down.shape[1]),
                            lambda b, g, i: (jnp.minimum(step(b, g, i), down_blocks - 1), 0))
    return pl.pallas_call(
        functools.partial(_attn_kernel, down_blocks=down_blocks),
        grid=(batch, N_KV_HEADS, nq),
        in_specs=[
            pl.BlockSpec((ATTN_TQ, gw), lambda b, g, i: (b * nq + i, g)),
            pl.BlockSpec((seq, HEAD_DIM), lambda b, g, i: (b, k_blk0 + g)),
            pl.BlockSpec((seq, HEAD_DIM), lambda b, g, i: (b, v_blk0 + g)),
            q_tab, q_tab, q_tab, k_tab, k_tab, k_tab, gain, gain, up_blk, down_blk,
        ],
        out_specs=[pl.BlockSpec((ATTN_TQ, gw), lambda b, g, i: (b * nq + i, g)), up_blk, down_blk],
        out_shape=[jax.ShapeDtypeStruct((t, ATTN_WIDTH), BF16),
                   jax.ShapeDtypeStruct(w_up.shape, BF16), jax.ShapeDtypeStruct(w_down.shape, BF16)],
        scratch_shapes=[pltpu.VMEM((seq, HEAD_DIM), BF16), pltpu.VMEM((seq, 2 * HEAD_DIM), BF16)],
        compiler_params=_params("arbitrary", "arbitrary", "arbitrary"),
        name="attention",
    )(proj, proj, proj, cos, sin_lo, sin_hi, cos, sin_lo, sin_hi, qg, kg, w_up, w_down)


def _fourier_fold_kernel(w_ref, cc_ref, sc_ref, o_ref):
    for g in range(FOURIER_GROUPS):
        w = w_ref[g]
        a = jnp.dot(cc_ref[...], w, preferred_element_type=F32, precision=lax.Precision.HIGHEST)
        b = jnp.dot(sc_ref[...], w, preferred_element_type=F32, precision=lax.Precision.HIGHEST)
        o_ref[g, :, :FOURIER_GROUP_DIM] = a.astype(o_ref.dtype)
        o_ref[g, :, FOURIER_GROUP_DIM:] = b.astype(o_ref.dtype)


def _fourier_fold(w_fmix, cc, sc):
    return pl.pallas_call(
        _fourier_fold_kernel,
        out_shape=jax.ShapeDtypeStruct((FOURIER_GROUPS, FOURIER_GROUP_DIM, 2 * FOURIER_GROUP_DIM), BF16),
        name="fourier_fold",
    )(w_fmix, cc, sc)


def _fourier_kernel(flo_ref, fhi_ref, ab_ref, c_ref, ms_ref, o_ref, za_scr, zb_scr):
    @pl.when(pl.program_id(1) == 0)
    def _():
        half = FOURIER_GROUPS // 2
        for g in range(FOURIER_GROUPS):
            src = flo_ref if g < half else fhi_ref
            lo = (g % half) * FOURIER_GROUP_DIM
            z = jnp.dot(src[:, lo:lo + FOURIER_GROUP_DIM], ab_ref[g], preferred_element_type=F32)
            dst = slice(g * FOURIER_GROUP_DIM, (g + 1) * FOURIER_GROUP_DIM)
            za_scr[:, dst] = z[:, :FOURIER_GROUP_DIM].astype(BF16)
            zb_scr[:, dst] = z[:, FOURIER_GROUP_DIM:].astype(BF16)

    acc = jnp.dot(c_ref[...], za_scr[...], preferred_element_type=F32)
    acc = acc + jnp.dot(ms_ref[...], zb_scr[...], preferred_element_type=F32)
    o_ref[...] = acc.astype(o_ref.dtype)


def _fourier(proj, ab, dft_c, dft_ms, batch, seq):
    t = proj.shape[0]
    nm = seq // FOURIER_TM
    half_w = FOURIER_WIDTH // 2
    f_blk0 = (ATTN_WIDTH + 2 * KV_WIDTH) // half_w
    return pl.pallas_call(
        _fourier_kernel,
        grid=(batch, nm),
        in_specs=[
            pl.BlockSpec((seq, half_w), lambda b, m: (b, f_blk0)),
            pl.BlockSpec((seq, half_w), lambda b, m: (b, f_blk0 + 1)),
            pl.BlockSpec(ab.shape, lambda b, m: (0, 0, 0)),
            pl.BlockSpec((FOURIER_TM, seq), lambda b, m: (m, 0)),
            pl.BlockSpec((FOURIER_TM, seq), lambda b, m: (m, 0)),
        ],
        out_specs=pl.BlockSpec((FOURIER_TM, FOURIER_WIDTH), lambda b, m: (b * nm + m, 0)),
        out_shape=jax.ShapeDtypeStruct((t, FOURIER_WIDTH), BF16),
        scratch_shapes=[pltpu.VMEM((seq, FOURIER_WIDTH), BF16), pltpu.VMEM((seq, FOURIER_WIDTH), BF16)],
        compiler_params=_params("parallel", "arbitrary"),
        name="fourier",
    )(proj, proj, ab, dft_c, dft_ms)


def _out_proj_kernel(a_ref, f_ref, x_ref, ag_ref, fg_ref, w_ref, g2_ref, h_ref, u_ref,
                     mix_scr, h_scr, w_scr):
    m, j = pl.program_id(0), pl.program_id(1)

    @pl.when(j == 0)
    def _():
        mix_scr[:, :ATTN_WIDTH] = (_rms(a_ref[...].astype(F32)) * ag_ref[...]).astype(BF16)
        mix_scr[:, ATTN_WIDTH:] = (_rms(f_ref[...].astype(F32)) * fg_ref[...]).astype(BF16)

    @pl.when(m == 0)
    def _():
        w_scr[j] = w_ref[...].astype(BF16)

    h = x_ref[...] + jnp.dot(mix_scr[...], w_scr[j], preferred_element_type=F32)
    h_ref[...] = h
    h_scr[:, pl.ds(pl.multiple_of(j * OUT_TN, OUT_TN), OUT_TN)] = h

    @pl.when(j == pl.num_programs(1) - 1)
    def _():
        u_ref[...] = (_rms(h_scr[...]) * g2_ref[...]).astype(u_ref.dtype)


def _out_proj(a, f, x2, ag, fg, w, g2):
    t, d = x2.shape
    row = lambda width: pl.BlockSpec((OUT_TM, width), lambda m, j: (m, 0))
    vec = lambda width: pl.BlockSpec((1, width), lambda m, j: (0, 0))
    tile = pl.BlockSpec((OUT_TM, OUT_TN), lambda m, j: (m, j))
    n_tiles = d // OUT_TN
    return pl.pallas_call(
        _out_proj_kernel,
        grid=(t // OUT_TM, n_tiles),
        in_specs=[
            row(ATTN_WIDTH), row(FOURIER_WIDTH), tile, vec(ATTN_WIDTH), vec(FOURIER_WIDTH),
            pl.BlockSpec((w.shape[0], OUT_TN), _weight_tile_once(n_tiles)),
            vec(d),
        ],
        out_specs=[tile, row(d)],
        out_shape=[jax.ShapeDtypeStruct((t, d), F32), jax.ShapeDtypeStruct((t, d), BF16)],
        scratch_shapes=[pltpu.VMEM((OUT_TM, w.shape[0]), BF16), pltpu.VMEM((OUT_TM, d), F32),
                        pltpu.VMEM((n_tiles, w.shape[0], OUT_TN), BF16)],
        compiler_params=_params("arbitrary", "arbitrary"),
        name="out_proj",
    )(a, f, x2, ag, fg, w, g2)


def _ffn_kernel(u_ref, wg_ref, wv_ref, cwg_ref, cwv_ref, cbg_ref, cbv_ref, wd_ref, h_ref, g_ref,
                y_ref, acc_ref, gate_scr, val_scr, *, n_ff):
    j = pl.program_id(1)
    seq = u_ref.shape[0]
    assert sum(FFN_ROW_SPLIT) == seq
    starts = [sum(FFN_ROW_SPLIT[:c]) for c in range(len(FFN_ROW_SPLIT))]

    @pl.when((pl.program_id(0) == 0) & (j == 0))
    def _():
        acc_ref[...] = jnp.zeros_like(acc_ref)
        gate_scr[...] = jnp.zeros_like(gate_scr)
        val_scr[...] = jnp.zeros_like(val_scr)

    @pl.when(j < n_ff)
    def _():
        wg, wv, wd = wg_ref[...], wv_ref[...], wd_ref[...]
        cwg, cwv, cbg, cbv = cwg_ref[...], cwv_ref[...], cbg_ref[...], cbv_ref[...]

        def conv3(scr, cw, cb, r0, rc):
            prev = scr[pl.ds(FFN_PAD - 1 + r0, rc), :]
            cur = scr[pl.ds(FFN_PAD + r0, rc), :]
            nxt = scr[pl.ds(FFN_PAD + 1 + r0, rc), :]
            return prev * cw[0:1, :] + cur * cw[1:2, :] + nxt * cw[2:3, :] + cb

        def gate_and_down(c):
            r0, rc = starts[c], FFN_ROW_SPLIT[c]
            gate = conv3(gate_scr, cwg, cbg, r0, rc)
            val = conv3(val_scr, cwv, cbv, r0, rc)
            act = (gate * jax.nn.sigmoid(gate) * val).astype(BF16)
            acc_ref[pl.ds(r0, rc), :] += jnp.dot(act, wd, preferred_element_type=F32)

        for c, (r0, rc) in enumerate(zip(starts, FFN_ROW_SPLIT)):
            u = u_ref[pl.ds(r0, rc), :]
            gate_scr[pl.ds(FFN_PAD + r0, rc), :] = jnp.dot(u, wg, preferred_element_type=F32)
            val_scr[pl.ds(FFN_PAD + r0, rc), :] = jnp.dot(u, wv, preferred_element_type=F32)
            if c >= 1:
                gate_and_down(c - 1)
        gate_and_down(len(FFN_ROW_SPLIT) - 1)

        n_out = seq // FFN_OUT_ROWS
        rows = pl.ds(pl.multiple_of(jnp.minimum(j, n_out - 1) * FFN_OUT_ROWS, FFN_OUT_ROWS), FFN_OUT_ROWS)
        acc_ref[rows, :] += jnp.where(j < n_out, h_ref[...], 0.0)

    @pl.when(j >= n_ff)
    def _():
        rows = pl.ds(pl.multiple_of((j - n_ff) * FFN_OUT_ROWS, FFN_OUT_ROWS), FFN_OUT_ROWS)
        y_ref[...] = _rms(acc_ref[rows, :]) * g_ref[...]
        acc_ref[rows, :] = jnp.zeros((FFN_OUT_ROWS, acc_ref.shape[1]), F32)


def _ffn(u2, w_up, conv_w, conv_b, w_down, h1, g, batch, seq):
    t, d = u2.shape
    d_ff = w_down.shape[0]
    n_ff = d_ff // FFN_TN
    n_out = seq // FFN_OUT_ROWS
    taps = conv_w.shape[0]
    ff = lambda j: jnp.minimum(j, n_ff - 1)
    gate_col = lambda b, j: (0, ff(j))
    val_col = lambda b, j: (0, n_ff + ff(j))
    out_row = lambda b, j: (b * n_out + jnp.maximum(j - n_ff, 0), 0)
    res_row = lambda b, j: (b * n_out + jnp.minimum(j, n_out - 1), 0)
    assert n_out <= n_ff
    return pl.pallas_call(
        functools.partial(_ffn_kernel, n_ff=n_ff),
        grid=(batch, n_ff + n_out),
        in_specs=[
            pl.BlockSpec((seq, d), lambda b, j: (b, 0), pipeline_mode=pl.Buffered(1)),
            pl.BlockSpec((d, FFN_TN), gate_col),
            pl.BlockSpec((d, FFN_TN), val_col),
            pl.BlockSpec((taps, FFN_TN), gate_col),
            pl.BlockSpec((taps, FFN_TN), val_col),
            pl.BlockSpec((1, FFN_TN), gate_col),
            pl.BlockSpec((1, FFN_TN), val_col),
            pl.BlockSpec((FFN_TN, d), lambda b, j: (ff(j), 0)),
            pl.BlockSpec((FFN_OUT_ROWS, d), res_row),
            pl.BlockSpec((1, d), lambda b, j: (0, 0)),
        ],
        out_specs=pl.BlockSpec((FFN_OUT_ROWS, d), out_row),
        out_shape=jax.ShapeDtypeStruct((t, d), F32),
        scratch_shapes=[
            pltpu.VMEM((seq, d), F32),
            pltpu.VMEM((seq + 2 * FFN_PAD, FFN_TN), F32),
            pltpu.VMEM((seq + 2 * FFN_PAD, FFN_TN), F32),
        ],
        compiler_params=pltpu.CompilerParams(dimension_semantics=("arbitrary", "arbitrary"),
                                             vmem_limit_bytes=FFN_VMEM_LIMIT_BYTES),
        name="ffn",
    )(u2, w_up, w_up, conv_w, conv_w, conv_b, conv_b, w_down, h1, g)


def _rope_tables(seq):
    t = np.arange(seq)
    inv_freq = ROPE_THETA ** (-np.arange(ROPE_PAIRS, dtype=np.float64) / ROPE_PAIRS)

    def axis_tables(pos):
        ang = pos[:, None] * inv_freq[None, :]
        ang = np.concatenate([ang, ang], axis=-1)
        return np.cos(ang), np.sin(ang)

    cos_r, sin_r = axis_tables(t // GRID_W)
    cos_c, sin_c = axis_tables(t % GRID_W)
    cos = np.concatenate([cos_r, cos_c], axis=-1)
    sin = np.concatenate([sin_r, sin_c], axis=-1)
    first_half = (np.arange(HEAD_DIM) % (2 * ROPE_PAIRS)) < ROPE_PAIRS
    sin_lo = np.where(first_half[None, :], -sin, 0.0)
    sin_hi = np.where(first_half[None, :], 0.0, sin)
    return tuple(jnp.asarray(tab, F32) for tab in (cos, sin_lo, sin_hi))


def _dft_cos_sin(rows, n_cols, n, scale=1.0):
    ang = ((rows[:, None] * np.arange(n_cols)[None, :]) % n) * (2.0 * math.pi / n)
    return jnp.asarray(np.cos(ang) * scale, F32), jnp.asarray(np.sin(ang) * scale, F32)


def _dft_tables(n, scale):
    return _dft_cos_sin(np.arange(n), n, n, scale)


def _seq_dft_kernel(ca_ref, sa_ref, cr_ref, sr_ref, c_ref, ms_ref):
    cr, sr = cr_ref[...], sr_ref[...]
    for i in range(DFT_COARSE_PER_STEP):
        ca, sa = ca_ref[i:i + 1, :], sa_ref[i:i + 1, :]
        rows = slice(i * DFT_SUB, (i + 1) * DFT_SUB)
        c_ref[rows, :] = (ca * cr - sa * sr).astype(c_ref.dtype)
        ms_ref[rows, :] = (-(sa * cr + ca * sr)).astype(ms_ref.dtype)


def _seq_dft_tables(n):
    ca, sa = _dft_cos_sin(np.arange(n // DFT_SUB) * DFT_SUB, n, n)
    cr, sr = _dft_cos_sin(np.arange(DFT_SUB), n, n)
    steps = n // (DFT_SUB * DFT_COARSE_PER_STEP)
    coarse_spec = pl.BlockSpec((DFT_COARSE_PER_STEP, n), lambda i: (i, 0))
    fine_spec = pl.BlockSpec((DFT_SUB, n), lambda i: (0, 0))
    out_spec = pl.BlockSpec((DFT_SUB * DFT_COARSE_PER_STEP, n), lambda i: (i, 0))
    return pl.pallas_call(
        _seq_dft_kernel,
        grid=(steps,),
        in_specs=[coarse_spec, coarse_spec, fine_spec, fine_spec],
        out_specs=[out_spec, out_spec],
        out_shape=[jax.ShapeDtypeStruct((n, n), BF16)] * 2,
        compiler_params=_params("parallel"),
        name="seq_dft_tables",
    )(ca, sa, cr, sr)


def kernel(x, norm1_g, w_in, q_norm_g, k_norm_g, w_fmix, attn_out_g, fourier_out_g, w_out,
           norm2_g, w_up, conv_w, conv_b, w_down, final_g):
    batch, seq, d = x.shape
    depth = w_in.shape[0]
    rope = _rope_tables(seq)
    dft_c, dft_ms = _seq_dft_tables(seq)
    cc, sc = _dft_tables(FOURIER_GROUP_DIM, 1.0 / math.sqrt(seq * FOURIER_GROUP_DIM))

    assert depth == 1
    h = x.reshape(batch * seq, d)
    for l in range(depth):
        proj = _in_proj(h, norm1_g[l][None], w_in[l])
        attn, w_up_bf16, w_down_bf16 = _attention(proj, rope, q_norm_g[l][None], k_norm_g[l][None],
                                                  w_up[l], w_down[l], batch, seq)
        ab = _fourier_fold(w_fmix[l], cc, sc)
        four = _fourier(proj, ab, dft_c, dft_ms, batch, seq)
        h1, u2 = _out_proj(attn, four, h, attn_out_g[l][None], fourier_out_g[l][None],
                           w_out[l], norm2_g[l][None])
        h = _ffn(u2, w_up_bf16, conv_w[l], conv_b[l][None], w_down_bf16, h1, final_g[None], batch, seq)
    return h.reshape(batch, seq, d)
```

```python
import functools
import math

import jax
import jax.numpy as jnp
import numpy as np
from jax import lax
from jax.experimental import pallas as pl
from jax.experimental.pallas import tpu as pltpu

F32 = jnp.float32
BF16 = jnp.bfloat16

HEAD_DIM = 128
N_Q_HEADS = 8
N_KV_HEADS = 2
Q_PER_KV = N_Q_HEADS // N_KV_HEADS
ATTN_WIDTH = N_Q_HEADS * HEAD_DIM
KV_WIDTH = N_KV_HEADS * HEAD_DIM
FOURIER_GROUPS = 8
FOURIER_GROUP_DIM = 128
FOURIER_WIDTH = FOURIER_GROUPS * FOURIER_GROUP_DIM
GRID_W = 64
ROPE_THETA = 10000.0
ROPE_PAIRS = HEAD_DIM // 4
EPS = 1e-6

VMEM_LIMIT_BYTES = 56 * 1024 * 1024

IN_TM, IN_TN, IN_ROWS = 1024, 512, 512
ATTN_TQ, ATTN_ROWS = 1024, 256
BF16_SUBLANES = 16
CAST_DOWN_ROWS = 352
FOURIER_TM = 512
DFT_SUB = 64
OUT_TM, OUT_TN = 512, 512
FFN_TN = 512
FFN_ROW_SPLIT = (1280, 768)
FFN_PAD = 8
FFN_OUT_ROWS = 256
FFN_VMEM_LIMIT_BYTES = 60 * 1024 * 1024


def _params(*sem):
    return pltpu.CompilerParams(dimension_semantics=sem, vmem_limit_bytes=VMEM_LIMIT_BYTES)


def _rms(x):
    return x * lax.rsqrt(jnp.mean(x * x, axis=-1, keepdims=True) + EPS)


def _in_proj_kernel(x_ref, g_ref, w_ref, o_ref, u_scr, w_scr, *, n_tiles):
    s = pl.program_id(0)

    def normed(rows):
        return (_rms(x_ref[rows, :]) * g_ref[...]).astype(BF16)

    @pl.when(s == 0)
    def _():
        u_scr[...] = normed(slice(None))

    @pl.when(s < n_tiles)
    def _():
        w_scr[s] = w_ref[...].astype(BF16)
        cols = pl.ds(pl.multiple_of(s * IN_TN, IN_TN), IN_TN)
        o_ref[:, cols] = jnp.dot(u_scr[...], w_scr[s], preferred_element_type=F32).astype(o_ref.dtype)

    @pl.when(s >= n_tiles)
    def _():
        for c in range(IN_TM // IN_ROWS):
            rows = slice(c * IN_ROWS, (c + 1) * IN_ROWS)
            u = normed(rows)
            for t in range(n_tiles):
                o_ref[rows, t * IN_TN:(t + 1) * IN_TN] = jnp.dot(
                    u, w_scr[t], preferred_element_type=F32).astype(o_ref.dtype)


def _in_proj(x2, g, w):
    t, d = x2.shape
    n = w.shape[1]
    n_tiles = n // IN_TN
    row_block = lambda s: (jnp.maximum(s - (n_tiles - 1), 0), 0)
    return pl.pallas_call(
        functools.partial(_in_proj_kernel, n_tiles=n_tiles),
        grid=(n_tiles + t // IN_TM - 1,),
        in_specs=[
            pl.BlockSpec((IN_TM, d), row_block),
            pl.BlockSpec((1, d), lambda s: (0, 0)),
            pl.BlockSpec((d, IN_TN), lambda s: (0, jnp.minimum(s, n_tiles - 1))),
        ],
        out_specs=pl.BlockSpec((IN_TM, n), row_block),
        out_shape=jax.ShapeDtypeStruct((t, n), BF16),
        scratch_shapes=[pltpu.VMEM((IN_TM, d), BF16), pltpu.VMEM((n_tiles, d, IN_TN), BF16)],
        compiler_params=_params("arbitrary"),
        name="in_proj",
    )(x2, g, w)


def _rope(x, cos, sin_lo, sin_hi):
    return (x * cos
            + pltpu.roll(x, HEAD_DIM - ROPE_PAIRS, axis=1) * sin_lo
            + pltpu.roll(x, ROPE_PAIRS, axis=1) * sin_hi)


def _attn_kernel(q_ref, k_ref, v_ref, cq_ref, slq_ref, shq_ref, ck_ref, slk_ref, shk_ref,
                 qg_ref, kg_ref, wu_ref, wd_ref, o_ref, wu_out, wd_out, k_scr, v_scr, *, down_blocks):
    step = ((pl.program_id(0) * pl.num_programs(1) + pl.program_id(1)) * pl.num_programs(2)
            + pl.program_id(2))
    wu_out[...] = wu_ref[...].astype(BF16)

    @pl.when(step < down_blocks)
    def _():
        wd_out[...] = wd_ref[...].astype(BF16)

    @pl.when(pl.program_id(2) == 0)
    def _():
        k = _rms(k_ref[...].astype(F32)) * kg_ref[...]
        k_scr[...] = _rope(k, ck_ref[...], slk_ref[...], shk_ref[...]).astype(BF16)
        v_scr[:, :HEAD_DIM] = v_ref[...]
        v_scr[:, HEAD_DIM:] = jnp.ones((v_ref.shape[0], HEAD_DIM), BF16)

    scale = math.log2(math.e) / math.sqrt(HEAD_DIM)
    cos, sin_lo, sin_hi = cq_ref[...], slq_ref[...], shq_ref[...]
    kk = k_scr[...]
    vv = v_scr[...]
    for r in range(ATTN_TQ // ATTN_ROWS):
        rows = slice(r * ATTN_ROWS, (r + 1) * ATTN_ROWS)
        for h in range(Q_PER_KV):
            sl = slice(h * HEAD_DIM, (h + 1) * HEAD_DIM)
            q = _rms(q_ref[rows, sl].astype(F32)) * qg_ref[...]
            q = (_rope(q, cos[rows], sin_lo[rows], sin_hi[rows]) * scale).astype(BF16)
            s = lax.dot_general(q, kk, (((1,), (1,)), ((), ())), preferred_element_type=F32)
            p = jnp.exp2(s - jnp.max(s, axis=-1, keepdims=True))
            o = jnp.dot(p.astype(BF16), vv, preferred_element_type=F32)
            o_ref[rows, sl] = (o[:, :HEAD_DIM] / o[:, HEAD_DIM:]).astype(o_ref.dtype)


def _attention(proj, tabs, qg, kg, w_up, w_down, batch, seq):
    t = proj.shape[0]
    nq = seq // ATTN_TQ
    gw = Q_PER_KV * HEAD_DIM
    k_blk0 = ATTN_WIDTH // HEAD_DIM
    v_blk0 = (ATTN_WIDTH + KV_WIDTH) // HEAD_DIM
    cos, sin_lo, sin_hi = tabs
    q_tab = pl.BlockSpec((ATTN_TQ, HEAD_DIM), lambda b, g, i: (i, 0))
    k_tab = pl.BlockSpec((seq, HEAD_DIM), lambda b, g, i: (0, 0))
    gain = pl.BlockSpec((1, HEAD_DIM), lambda b, g, i: (0, 0))

    n_steps = batch * N_KV_HEADS * nq
    step = lambda b, g, i: (b * N_KV_HEADS + g) * nq + i
    up_rows = w_up.shape[0] // n_steps
    down_blocks = w_down.shape[0] // CAST_DOWN_ROWS
    assert up_rows * n_steps == w_up.shape[0] and up_rows % BF16_SUBLANES == 0
    assert down_blocks * CAST_DOWN_ROWS == w_down.shape[0] and down_blocks <= n_steps
    up_blk = pl.BlockSpec((up_rows, w_up.shape[1]), lambda b, g, i: (step(b, g, i), 0))
    down_blk = pl.BlockSpec((CAST_DOWN_ROWS, w_down.shape[1]),
                            lambda b, g, i: (jnp.minimum(step(b, g, i), down_blocks - 1), 0))
    return pl.pallas_call(
        functools.partial(_attn_kernel, down_blocks=down_blocks),
        grid=(batch, N_KV_HEADS, nq),
        in_specs=[
            pl.BlockSpec((ATTN_TQ, gw), lambda b, g, i: (b * nq + i, g)),
            pl.BlockSpec((seq, HEAD_DIM), lambda b, g, i: (b, k_blk0 + g)),
            pl.BlockSpec((seq, HEAD_DIM), lambda b, g, i: (b, v_blk0 + g)),
            q_tab, q_tab, q_tab, k_tab, k_tab, k_tab, gain, gain, up_blk, down_blk,
        ],
        out_specs=[pl.BlockSpec((ATTN_TQ, gw), lambda b, g, i: (b * nq + i, g)), up_blk, down_blk],
        out_shape=[jax.ShapeDtypeStruct((t, ATTN_WIDTH), BF16),
                   jax.ShapeDtypeStruct(w_up.shape, BF16), jax.ShapeDtypeStruct(w_down.shape, BF16)],
        scratch_shapes=[pltpu.VMEM((seq, HEAD_DIM), BF16), pltpu.VMEM((seq, 2 * HEAD_DIM), BF16)],
        compiler_params=_params("arbitrary", "arbitrary", "arbitrary"),
        name="attention",
    )(proj, proj, proj, cos, sin_lo, sin_hi, cos, sin_lo, sin_hi, qg, kg, w_up, w_down)


def _fourier_fold_kernel(w_ref, cc_ref, sc_ref, o_ref):
    for g in range(FOURIER_GROUPS):
        w = w_ref[g]
        a = jnp.dot(cc_ref[...], w, preferred_element_type=F32, precision=lax.Precision.HIGHEST)
        b = jnp.dot(sc_ref[...], w, preferred_element_type=F32, precision=lax.Precision.HIGHEST)
        o_ref[g, :, :FOURIER_GROUP_DIM] = a.astype(o_ref.dtype)
        o_ref[g, :, FOURIER_GROUP_DIM:] = b.astype(o_ref.dtype)


def _fourier_fold(w_fmix, cc, sc):
    return pl.pallas_call(
        _fourier_fold_kernel,
        out_shape=jax.ShapeDtypeStruct((FOURIER_GROUPS, FOURIER_GROUP_DIM, 2 * FOURIER_GROUP_DIM), BF16),
        name="fourier_fold",
    )(w_fmix, cc, sc)


def _fourier_kernel(flo_ref, fhi_ref, ab_ref, ca_ref, sa_ref, cr_ref, sr_ref, o_ref,
                    za_scr, zb_scr, c_scr, ms_scr):
    m = pl.program_id(1)
    rows = pl.ds(pl.multiple_of(m * FOURIER_TM, FOURIER_TM), FOURIER_TM)

    @pl.when(pl.program_id(0) == 0)
    def _():
        cr, sr = cr_ref[...], sr_ref[...]
        per_block = FOURIER_TM // DFT_SUB
        for i in range(per_block):
            ca = ca_ref[pl.ds(m * per_block + i, 1), :]
            sa = sa_ref[pl.ds(m * per_block + i, 1), :]
            sub = pl.ds(pl.multiple_of(m * FOURIER_TM + i * DFT_SUB, DFT_SUB), DFT_SUB)
            c_scr[sub, :] = (ca * cr - sa * sr).astype(BF16)
            ms_scr[sub, :] = (-(sa * cr + ca * sr)).astype(BF16)

    @pl.when(m == 0)
    def _():
        half = FOURIER_GROUPS // 2
        for g in range(FOURIER_GROUPS):
            src = flo_ref if g < half else fhi_ref
            lo = (g % half) * FOURIER_GROUP_DIM
            z = jnp.dot(src[:, lo:lo + FOURIER_GROUP_DIM], ab_ref[g], preferred_element_type=F32)
            dst = slice(g * FOURIER_GROUP_DIM, (g + 1) * FOURIER_GROUP_DIM)
            za_scr[:, dst] = z[:, :FOURIER_GROUP_DIM].astype(BF16)
            zb_scr[:, dst] = z[:, FOURIER_GROUP_DIM:].astype(BF16)

    acc = jnp.dot(c_scr[rows, :], za_scr[...], preferred_element_type=F32)
    acc = acc + jnp.dot(ms_scr[rows, :], zb_scr[...], preferred_element_type=F32)
    o_ref[...] = acc.astype(o_ref.dtype)


def _fourier(proj, ab, batch, seq):
    t = proj.shape[0]
    nm = seq // FOURIER_TM
    half_w = FOURIER_WIDTH // 2
    f_blk0 = (ATTN_WIDTH + 2 * KV_WIDTH) // half_w
    ca, sa = _dft_cos_sin(np.arange(seq // DFT_SUB) * DFT_SUB, seq, seq)
    cr, sr = _dft_cos_sin(np.arange(DFT_SUB), seq, seq)
    whole = lambda arr: pl.BlockSpec(arr.shape, lambda b, m: (0,) * arr.ndim)
    return pl.pallas_call(
        _fourier_kernel,
        grid=(batch, nm),
        in_specs=[
            pl.BlockSpec((seq, half_w), lambda b, m: (b, f_blk0)),
            pl.BlockSpec((seq, half_w), lambda b, m: (b, f_blk0 + 1)),
            whole(ab), whole(ca), whole(sa), whole(cr), whole(sr),
        ],
        out_specs=pl.BlockSpec((FOURIER_TM, FOURIER_WIDTH), lambda b, m: (b * nm + m, 0)),
        out_shape=jax.ShapeDtypeStruct((t, FOURIER_WIDTH), BF16),
        scratch_shapes=[pltpu.VMEM((seq, FOURIER_WIDTH), BF16), pltpu.VMEM((seq, FOURIER_WIDTH), BF16),
                        pltpu.VMEM((seq, seq), BF16), pltpu.VMEM((seq, seq), BF16)],
        compiler_params=_params("arbitrary", "arbitrary"),
        name="fourier",
    )(proj, proj, ab, ca, sa, cr, sr)


def _out_proj_kernel(a_ref, f_ref, x_ref, mg_ref, w_ref, g2_ref, h_ref, u_ref, w_scr, *, n_tiles):
    s = pl.program_id(0)

    def inv_rms(ref):
        v = ref[...].astype(F32)
        return lax.rsqrt(jnp.mean(v * v, axis=-1, keepdims=True) + EPS)

    def column_tile(t, cols, ra, rf):
        h_ref[:, cols] = (x_ref[:, cols]
                          + ra * jnp.dot(a_ref[...], w_scr[t, :ATTN_WIDTH, :], preferred_element_type=F32)
                          + rf * jnp.dot(f_ref[...], w_scr[t, ATTN_WIDTH:, :], preferred_element_type=F32))

    def finish():
        u_ref[...] = (_rms(h_ref[...]) * g2_ref[...]).astype(u_ref.dtype)

    @pl.when(s < n_tiles)
    def _():
        w_scr[s] = (w_ref[...] * mg_ref[...]).astype(BF16)
        column_tile(s, pl.ds(pl.multiple_of(s * OUT_TN, OUT_TN), OUT_TN), inv_rms(a_ref), inv_rms(f_ref))

    @pl.when(s == n_tiles - 1)
    def _():
        finish()

    @pl.when(s >= n_tiles)
    def _():
        ra, rf = inv_rms(a_ref), inv_rms(f_ref)
        for t in range(n_tiles):
            column_tile(t, slice(t * OUT_TN, (t + 1) * OUT_TN), ra, rf)
        finish()


def _out_proj(a, f, x2, ag, fg, w, g2):
    t, d = x2.shape
    n_tiles = d // OUT_TN
    row_block = lambda s: (jnp.maximum(s - (n_tiles - 1), 0), 0)
    row = lambda width: pl.BlockSpec((OUT_TM, width), row_block)
    mix_gain = jnp.concatenate([ag, fg], axis=1).reshape(w.shape[0], 1)
    return pl.pallas_call(
        functools.partial(_out_proj_kernel, n_tiles=n_tiles),
        grid=(n_tiles + t // OUT_TM - 1,),
        in_specs=[
            row(ATTN_WIDTH), row(FOURIER_WIDTH), row(d),
            pl.BlockSpec((w.shape[0], 1), lambda s: (0, 0)),
            pl.BlockSpec((w.shape[0], OUT_TN), lambda s: (0, jnp.minimum(s, n_tiles - 1))),
            pl.BlockSpec((1, d), lambda s: (0, 0)),
        ],
        out_specs=[row(d), row(d)],
        out_shape=[jax.ShapeDtypeStruct((t, d), F32), jax.ShapeDtypeStruct((t, d), BF16)],
        scratch_shapes=[pltpu.VMEM((n_tiles, w.shape[0], OUT_TN), BF16)],
        compiler_params=_params("arbitrary"),
        name="out_proj",
    )(a, f, x2, mix_gain, w, g2)


def _ffn_kernel(u_ref, wg_ref, wv_ref, cwg_ref, cwv_ref, cbg_ref, cbv_ref, wd_ref, h_ref, g_ref,
                y_ref, acc_ref, gate_scr, val_scr, *, n_ff):
    j = pl.program_id(1)
    seq = u_ref.shape[0]
    assert sum(FFN_ROW_SPLIT) == seq
    starts = [sum(FFN_ROW_SPLIT[:c]) for c in range(len(FFN_ROW_SPLIT))]

    @pl.when((pl.program_id(0) == 0) & (j == 0))
    def _():
        acc_ref[...] = jnp.zeros_like(acc_ref)
        gate_scr[...] = jnp.zeros_like(gate_scr)
        val_scr[...] = jnp.zeros_like(val_scr)

    @pl.when(j < n_ff)
    def _():
        wg, wv, wd = wg_ref[...], wv_ref[...], wd_ref[...]
        cwg, cwv, cbg, cbv = cwg_ref[...], cwv_ref[...], cbg_ref[...], cbv_ref[...]

        def conv3(scr, cw, cb, r0, rc):
            prev = scr[pl.ds(FFN_PAD - 1 + r0, rc), :]
            cur = scr[pl.ds(FFN_PAD + r0, rc), :]
            nxt = scr[pl.ds(FFN_PAD + 1 + r0, rc), :]
            return prev * cw[0:1, :] + cur * cw[1:2, :] + nxt * cw[2:3, :] + cb

        def gate_and_down(c):
            r0, rc = starts[c], FFN_ROW_SPLIT[c]
            gate = conv3(gate_scr, cwg, cbg, r0, rc)
            val = conv3(val_scr, cwv, cbv, r0, rc)
            act = (gate * jax.nn.sigmoid(gate) * val).astype(BF16)
            acc_ref[pl.ds(r0, rc), :] += jnp.dot(act, wd, preferred_element_type=F32)

        for c, (r0, rc) in enumerate(zip(starts, FFN_ROW_SPLIT)):
            u = u_ref[pl.ds(r0, rc), :]
            gate_scr[pl.ds(FFN_PAD + r0, rc), :] = jnp.dot(u, wg, preferred_element_type=F32)
            val_scr[pl.ds(FFN_PAD + r0, rc), :] = jnp.dot(u, wv, preferred_element_type=F32)
            if c >= 1:
                gate_and_down(c - 1)
        gate_and_down(len(FFN_ROW_SPLIT) - 1)

        n_out = seq // FFN_OUT_ROWS
        rows = pl.ds(pl.multiple_of(jnp.minimum(j, n_out - 1) * FFN_OUT_ROWS, FFN_OUT_ROWS), FFN_OUT_ROWS)
        acc_ref[rows, :] += jnp.where(j < n_out, h_ref[...], 0.0)

    @pl.when(j >= n_ff)
    def _():
        rows = pl.ds(pl.multiple_of((j - n_ff) * FFN_OUT_ROWS, FFN_OUT_ROWS), FFN_OUT_ROWS)
        y_ref[...] = _rms(acc_ref[rows, :]) * g_ref[...]
        acc_ref[rows, :] = jnp.zeros((FFN_OUT_ROWS, acc_ref.shape[1]), F32)


def _ffn(u2, w_up, conv_w, conv_b, w_down, h1, g, batch, seq):
    t, d = u2.shape
    d_ff = w_down.shape[0]
    n_ff = d_ff // FFN_TN
    n_out = seq // FFN_OUT_ROWS
    taps = conv_w.shape[0]
    ff = lambda j: jnp.minimum(j, n_ff - 1)
    gate_col = lambda b, j: (0, ff(j))
    val_col = lambda b, j: (0, n_ff + ff(j))
    out_row = lambda b, j: (b * n_out + jnp.maximum(j - n_ff, 0), 0)
    res_row = lambda b, j: (b * n_out + jnp.minimum(j, n_out - 1), 0)
    assert n_out <= n_ff
    return pl.pallas_call(
        functools.partial(_ffn_kernel, n_ff=n_ff),
        grid=(batch, n_ff + n_out),
        in_specs=[
            pl.BlockSpec((seq, d), lambda b, j: (b, 0), pipeline_mode=pl.Buffered(1)),
            pl.BlockSpec((d, FFN_TN), gate_col),
            pl.BlockSpec((d, FFN_TN), val_col),
            pl.BlockSpec((taps, FFN_TN), gate_col),
            pl.BlockSpec((taps, FFN_TN), val_col),
            pl.BlockSpec((1, FFN_TN), gate_col),
            pl.BlockSpec((1, FFN_TN), val_col),
            pl.BlockSpec((FFN_TN, d), lambda b, j: (ff(j), 0)),
            pl.BlockSpec((FFN_OUT_ROWS, d), res_row),
            pl.BlockSpec((1, d), lambda b, j: (0, 0)),
        ],
        out_specs=pl.BlockSpec((FFN_OUT_ROWS, d), out_row),
        out_shape=jax.ShapeDtypeStruct((t, d), F32),
        scratch_shapes=[
            pltpu.VMEM((seq, d), F32),
            pltpu.VMEM((seq + 2 * FFN_PAD, FFN_TN), F32),
            pltpu.VMEM((seq + 2 * FFN_PAD, FFN_TN), F32),
        ],
        compiler_params=pltpu.CompilerParams(dimension_semantics=("arbitrary", "arbitrary"),
                                             vmem_limit_bytes=FFN_VMEM_LIMIT_BYTES),
        name="ffn",
    )(u2, w_up, w_up, conv_w, conv_w, conv_b, conv_b, w_down, h1, g)


def _rope_tables(seq):
    t = np.arange(seq)
    inv_freq = ROPE_THETA ** (-np.arange(ROPE_PAIRS, dtype=np.float64) / ROPE_PAIRS)

    def axis_tables(pos):
        ang = pos[:, None] * inv_freq[None, :]
        ang = np.concatenate([ang, ang], axis=-1)
        return np.cos(ang), np.sin(ang)

    cos_r, sin_r = axis_tables(t // GRID_W)
    cos_c, sin_c = axis_tables(t % GRID_W)
    cos = np.concatenate([cos_r, cos_c], axis=-1)
    sin = np.concatenate([sin_r, sin_c], axis=-1)
    first_half = (np.arange(HEAD_DIM) % (2 * ROPE_PAIRS)) < ROPE_PAIRS
    sin_lo = np.where(first_half[None, :], -sin, 0.0)
    sin_hi = np.where(first_half[None, :], 0.0, sin)
    return tuple(jnp.asarray(tab, F32) for tab in (cos, sin_lo, sin_hi))


def _dft_cos_sin(rows, n_cols, n, scale=1.0):
    ang = ((rows[:, None] * np.arange(n_cols)[None, :]) % n) * (2.0 * math.pi / n)
    return jnp.asarray(np.cos(ang) * scale, F32), jnp.asarray(np.sin(ang) * scale, F32)


def _dft_tables(n, scale):
    return _dft_cos_sin(np.arange(n), n, n, scale)


def kernel(x, norm1_g, w_in, q_norm_g, k_norm_g, w_fmix, attn_out_g, fourier_out_g, w_out,
           norm2_g, w_up, conv_w, conv_b, w_down, final_g):
    batch, seq, d = x.shape
    depth = w_in.shape[0]
    rope = _rope_tables(seq)
    cc, sc = _dft_tables(FOURIER_GROUP_DIM, 1.0 / math.sqrt(seq * FOURIER_GROUP_DIM))

    assert depth == 1
    h = x.reshape(batch * seq, d)
    for l in range(depth):
        proj = _in_proj(h, norm1_g[l][None], w_in[l])
        attn, w_up_bf16, w_down_bf16 = _attention(proj, rope, q_norm_g[l][None], k_norm_g[l][None],
                                                  w_up[l], w_down[l], batch, seq)
        ab = _fourier_fold(w_fmix[l], cc, sc)
        four = _fourier(proj, ab, batch, seq)
        h1, u2 = _out_proj(attn, four, h, attn_out_g[l][None], fourier_out_g[l][None],
                           w_out[l], norm2_g[l][None])
        h = _ffn(u2, w_up_bf16, conv_w[l], conv_b[l][None], w_down_bf16, h1, final_g[None], batch, seq)
    return h.reshape(batch, seq, d)
```

```python
import functools
import math

import jax
import jax.numpy as jnp
import numpy as np
from jax import lax
from jax.experimental import pallas as pl
from jax.experimental.pallas import tpu as pltpu

F32 = jnp.float32
BF16 = jnp.bfloat16

HEAD_DIM = 128
N_Q_HEADS = 8
N_KV_HEADS = 2
Q_PER_KV = N_Q_HEADS // N_KV_HEADS
ATTN_WIDTH = N_Q_HEADS * HEAD_DIM
KV_WIDTH = N_KV_HEADS * HEAD_DIM
FOURIER_GROUPS = 8
FOURIER_GROUP_DIM = 128
FOURIER_WIDTH = FOURIER_GROUPS * FOURIER_GROUP_DIM
GRID_W = 64
ROPE_THETA = 10000.0
ROPE_PAIRS = HEAD_DIM // 4
EPS = 1e-6

VMEM_LIMIT_BYTES = 56 * 1024 * 1024

IN_TM, IN_TN, IN_ROWS = 1024, 512, 512
ATTN_TQ, ATTN_ROWS = 1024, 256
BF16_SUBLANES = 16
CAST_DOWN_ROWS = 352
FOURIER_TM = 512
DFT_SUB = 64
OUT_TM, OUT_TN = 512, 512
FFN_TN = 512
FFN_ROW_SPLIT = (1280, 768)
FFN_PAD = 8
FFN_OUT_ROWS = 256
FFN_VMEM_LIMIT_BYTES = 60 * 1024 * 1024


def _params(*sem):
    return pltpu.CompilerParams(dimension_semantics=sem, vmem_limit_bytes=VMEM_LIMIT_BYTES)


def _rms(x):
    return x * lax.rsqrt(jnp.mean(x * x, axis=-1, keepdims=True) + EPS)


def _in_proj_kernel(x_ref, g_ref, w_ref, o_ref, u_scr, w_scr, *, n_tiles):
    s = pl.program_id(0)

    def normed(rows):
        return (_rms(x_ref[rows, :]) * g_ref[...]).astype(BF16)

    @pl.when(s == 0)
    def _():
        u_scr[...] = normed(slice(None))

    @pl.when(s < n_tiles)
    def _():
        w_scr[s] = w_ref[...].astype(BF16)
        cols = pl.ds(pl.multiple_of(s * IN_TN, IN_TN), IN_TN)
        o_ref[:, cols] = jnp.dot(u_scr[...], w_scr[s], preferred_element_type=F32).astype(o_ref.dtype)

    @pl.when(s >= n_tiles)
    def _():
        for c in range(IN_TM // IN_ROWS):
            rows = slice(c * IN_ROWS, (c + 1) * IN_ROWS)
            u = normed(rows)
            for t in range(n_tiles):
                o_ref[rows, t * IN_TN:(t + 1) * IN_TN] = jnp.dot(
                    u, w_scr[t], preferred_element_type=F32).astype(o_ref.dtype)


def _in_proj(x2, g, w):
    t, d = x2.shape
    n = w.shape[1]
    n_tiles = n // IN_TN
    row_block = lambda s: (jnp.maximum(s - (n_tiles - 1), 0), 0)
    return pl.pallas_call(
        functools.partial(_in_proj_kernel, n_tiles=n_tiles),
        grid=(n_tiles + t // IN_TM - 1,),
        in_specs=[
            pl.BlockSpec((IN_TM, d), row_block),
            pl.BlockSpec((1, d), lambda s: (0, 0)),
            pl.BlockSpec((d, IN_TN), lambda s: (0, jnp.minimum(s, n_tiles - 1))),
        ],
        out_specs=pl.BlockSpec((IN_TM, n), row_block),
        out_shape=jax.ShapeDtypeStruct((t, n), BF16),
        scratch_shapes=[pltpu.VMEM((IN_TM, d), BF16), pltpu.VMEM((n_tiles, d, IN_TN), BF16)],
        compiler_params=_params("arbitrary"),
        name="in_proj",
    )(x2, g, w)


def _rope(x, cos, sin_lo, sin_hi):
    return (x * cos
            + pltpu.roll(x, HEAD_DIM - ROPE_PAIRS, axis=1) * sin_lo
            + pltpu.roll(x, ROPE_PAIRS, axis=1) * sin_hi)


def _attn_kernel(q_ref, k_ref, v_ref, cq_ref, slq_ref, shq_ref, ck_ref, slk_ref, shk_ref,
                 qg_ref, kg_ref, wu_ref, wd_ref, o_ref, wu_out, wd_out, k_scr, v_scr, *, down_blocks):
    step = ((pl.program_id(0) * pl.num_programs(1) + pl.program_id(1)) * pl.num_programs(2)
            + pl.program_id(2))
    wu_out[...] = wu_ref[...].astype(BF16)

    @pl.when(step < down_blocks)
    def _():
        wd_out[...] = wd_ref[...].astype(BF16)

    @pl.when(pl.program_id(2) == 0)
    def _():
        k = _rms(k_ref[...].astype(F32)) * kg_ref[...]
        k_scr[...] = _rope(k, ck_ref[...], slk_ref[...], shk_ref[...]).astype(BF16)
        v_scr[:, :HEAD_DIM] = v_ref[...]
        v_scr[:, HEAD_DIM:] = jnp.ones((v_ref.shape[0], HEAD_DIM), BF16)

    scale = math.log2(math.e) / math.sqrt(HEAD_DIM)
    cos, sin_lo, sin_hi = cq_ref[...], slq_ref[...], shq_ref[...]
    kk = k_scr[...]
    vv = v_scr[...]
    for r in range(ATTN_TQ // ATTN_ROWS):
        rows = slice(r * ATTN_ROWS, (r + 1) * ATTN_ROWS)
        for h in range(Q_PER_KV):
            sl = slice(h * HEAD_DIM, (h + 1) * HEAD_DIM)
            q = _rms(q_ref[rows, sl].astype(F32)) * qg_ref[...]
            q = (_rope(q, cos[rows], sin_lo[rows], sin_hi[rows]) * scale).astype(BF16)
            s = lax.dot_general(q, kk, (((1,), (1,)), ((), ())), preferred_element_type=F32)
            p = jnp.exp2(s - jnp.max(s, axis=-1, keepdims=True))
            o = jnp.dot(p.astype(BF16), vv, preferred_element_type=F32)
            o_ref[rows, sl] = (o[:, :HEAD_DIM] / o[:, HEAD_DIM:]).astype(o_ref.dtype)


def _attention(proj, tabs, qg, kg, w_up, w_down, batch, seq):
    t = proj.shape[0]
    nq = seq // ATTN_TQ
    gw = Q_PER_KV * HEAD_DIM
    k_blk0 = ATTN_WIDTH // HEAD_DIM
    v_blk0 = (ATTN_WIDTH + KV_WIDTH) // HEAD_DIM
    cos, sin_lo, sin_hi = tabs
    q_tab = pl.BlockSpec((ATTN_TQ, HEAD_DIM), lambda b, g, i: (i, 0))
    k_tab = pl.BlockSpec((seq, HEAD_DIM), lambda b, g, i: (0, 0))
    gain = pl.BlockSpec((1, HEAD_DIM), lambda b, g, i: (0, 0))

    n_steps = batch * N_KV_HEADS * nq
    step = lambda b, g, i: (b * N_KV_HEADS + g) * nq + i
    up_rows = w_up.shape[0] // n_steps
    down_blocks = w_down.shape[0] // CAST_DOWN_ROWS
    assert up_rows * n_steps == w_up.shape[0] and up_rows % BF16_SUBLANES == 0
    assert down_blocks * CAST_DOWN_ROWS == w_down.shape[0] and down_blocks <= n_steps
    up_blk = pl.BlockSpec((up_rows, w_up.shape[1]), lambda b, g, i: (step(b, g, i), 0))
    down_blk = pl.BlockSpec((CAST_DOWN_ROWS, w_down.shape[1]),
                            lambda b, g, i: (jnp.minimum(step(b, g, i), down_blocks - 1), 0))
    return pl.pallas_call(
        functools.partial(_attn_kernel, down_blocks=down_blocks),
        grid=(batch, N_KV_HEADS, nq),
        in_specs=[
            pl.BlockSpec((ATTN_TQ, gw), lambda b, g, i: (b * nq + i, g)),
            pl.BlockSpec((seq, HEAD_DIM), lambda b, g, i: (b, k_blk0 + g)),
            pl.BlockSpec((seq, HEAD_DIM), lambda b, g, i: (b, v_blk0 + g)),
            q_tab, q_tab, q_tab, k_tab, k_tab, k_tab, gain, gain, up_blk, down_blk,
        ],
        out_specs=[pl.BlockSpec((ATTN_TQ, gw), lambda b, g, i: (b * nq + i, g)), up_blk, down_blk],
        out_shape=[jax.ShapeDtypeStruct((t, ATTN_WIDTH), BF16),
                   jax.ShapeDtypeStruct(w_up.shape, BF16), jax.ShapeDtypeStruct(w_down.shape, BF16)],
        scratch_shapes=[pltpu.VMEM((seq, HEAD_DIM), BF16), pltpu.VMEM((seq, 2 * HEAD_DIM), BF16)],
        compiler_params=_params("arbitrary", "arbitrary", "arbitrary"),
        name="attention",
    )(proj, proj, proj, cos, sin_lo, sin_hi, cos, sin_lo, sin_hi, qg, kg, w_up, w_down)


def _fourier_fold_kernel(w_ref, cc_ref, sc_ref, o_ref):
    for g in range(FOURIER_GROUPS):
        w = w_ref[g]
        a = jnp.dot(cc_ref[...], w, preferred_element_type=F32, precision=lax.Precision.HIGHEST)
        b = jnp.dot(sc_ref[...], w, preferred_element_type=F32, precision=lax.Precision.HIGHEST)
        o_ref[g, :, :FOURIER_GROUP_DIM] = a.astype(o_ref.dtype)
        o_ref[g, :, FOURIER_GROUP_DIM:] = b.astype(o_ref.dtype)


def _fourier_fold(w_fmix, cc, sc):
    return pl.pallas_call(
        _fourier_fold_kernel,
        out_shape=jax.ShapeDtypeStruct((FOURIER_GROUPS, FOURIER_GROUP_DIM, 2 * FOURIER_GROUP_DIM), BF16),
        name="fourier_fold",
    )(w_fmix, cc, sc)


def _fourier_kernel(flo_ref, fhi_ref, ab_ref, ca_ref, sa_ref, cr_ref, sr_ref, o_ref,
                    za_scr, zb_scr, z_stage, c_scr, ms_scr):
    m = pl.program_id(1)
    half = za_scr.shape[0] // 2
    rows = pl.ds(pl.multiple_of(m * FOURIER_TM, FOURIER_TM), FOURIER_TM)

    @pl.when(pl.program_id(0) == 0)
    def _():
        cr, sr = cr_ref[...], sr_ref[...]
        per_block = FOURIER_TM // DFT_SUB
        for i in range(per_block):
            ca = ca_ref[pl.ds(m * per_block + i, 1), :]
            sa = sa_ref[pl.ds(m * per_block + i, 1), :]
            sub = pl.ds(pl.multiple_of(m * FOURIER_TM + i * DFT_SUB, DFT_SUB), DFT_SUB)
            c_scr[sub, :] = (ca * cr - sa * sr).astype(BF16)
            ms_scr[sub, :] = (-(sa * cr + ca * sr)).astype(BF16)

    @pl.when(m == 0)
    def _():
        groups_per_ref = FOURIER_GROUPS // 2
        for g in range(FOURIER_GROUPS):
            src = flo_ref if g < groups_per_ref else fhi_ref
            lo = (g % groups_per_ref) * FOURIER_GROUP_DIM
            z = jnp.dot(src[:, lo:lo + FOURIER_GROUP_DIM], ab_ref[g], preferred_element_type=F32)
            z_stage[0] = z[:, :FOURIER_GROUP_DIM]
            z_stage[1] = z[:, FOURIER_GROUP_DIM:]
            dst = slice(g * FOURIER_GROUP_DIM, (g + 1) * FOURIER_GROUP_DIM)
            for parity in range(2):
                part = slice(parity * half, (parity + 1) * half)
                za_scr[part, dst] = z_stage[0, pl.ds(parity, half, stride=2), :].astype(BF16)
                zb_scr[part, dst] = z_stage[1, pl.ds(parity, half, stride=2), :].astype(BF16)

    def half_sum(cols):
        return (jnp.dot(c_scr[rows, cols], za_scr[cols, :], preferred_element_type=F32)
                + jnp.dot(ms_scr[rows, cols], zb_scr[cols, :], preferred_element_type=F32))

    even, odd = half_sum(slice(0, half)), half_sum(slice(half, 2 * half))
    o_ref[0] = (even + odd).astype(o_ref.dtype)
    o_ref[1] = (even - odd).astype(o_ref.dtype)


def _fourier(proj, ab, batch, seq):
    t = proj.shape[0]
    half = seq // 2
    nm = half // FOURIER_TM
    half_w = FOURIER_WIDTH // 2
    f_blk0 = (ATTN_WIDTH + 2 * KV_WIDTH) // half_w
    cols = np.concatenate([np.arange(0, seq, 2), np.arange(1, seq, 2)])
    ca, sa = _dft_cos_sin(np.arange(half // DFT_SUB) * DFT_SUB, cols, seq)
    cr, sr = _dft_cos_sin(np.arange(DFT_SUB), cols, seq)
    whole = lambda arr: pl.BlockSpec(arr.shape, lambda b, m: (0,) * arr.ndim)
    out = pl.pallas_call(
        _fourier_kernel,
        grid=(batch, nm),
        in_specs=[
            pl.BlockSpec((seq, half_w), lambda b, m: (b, f_blk0)),
            pl.BlockSpec((seq, half_w), lambda b, m: (b, f_blk0 + 1)),
            whole(ab), whole(ca), whole(sa), whole(cr), whole(sr),
        ],
        out_specs=pl.BlockSpec((None, 2, FOURIER_TM, FOURIER_WIDTH), lambda b, m: (b, 0, m, 0)),
        out_shape=jax.ShapeDtypeStruct((batch, 2, half, FOURIER_WIDTH), BF16),
        scratch_shapes=[pltpu.VMEM((seq, FOURIER_WIDTH), BF16), pltpu.VMEM((seq, FOURIER_WIDTH), BF16),
                        pltpu.VMEM((2, seq, FOURIER_GROUP_DIM), F32),
                        pltpu.VMEM((half, seq), BF16), pltpu.VMEM((half, seq), BF16)],
        compiler_params=_params("arbitrary", "arbitrary"),
        name="fourier",
    )(proj, proj, ab, ca, sa, cr, sr)
    return out.reshape(t, FOURIER_WIDTH)


def _out_proj_kernel(a_ref, f_ref, x_ref, mg_ref, w_ref, g2_ref, h_ref, u_ref, w_scr, *, n_tiles):
    s = pl.program_id(0)

    def inv_rms(ref):
        v = ref[...].astype(F32)
        return lax.rsqrt(jnp.mean(v * v, axis=-1, keepdims=True) + EPS)

    def column_tile(t, cols, ra, rf):
        h_ref[:, cols] = (x_ref[:, cols]
                          + ra * jnp.dot(a_ref[...], w_scr[t, :ATTN_WIDTH, :], preferred_element_type=F32)
                          + rf * jnp.dot(f_ref[...], w_scr[t, ATTN_WIDTH:, :], preferred_element_type=F32))

    def finish():
        u_ref[...] = (_rms(h_ref[...]) * g2_ref[...]).astype(u_ref.dtype)

    @pl.when(s < n_tiles)
    def _():
        w_scr[s] = (w_ref[...] * mg_ref[...]).astype(BF16)
        column_tile(s, pl.ds(pl.multiple_of(s * OUT_TN, OUT_TN), OUT_TN), inv_rms(a_ref), inv_rms(f_ref))

    @pl.when(s == n_tiles - 1)
    def _():
        finish()

    @pl.when(s >= n_tiles)
    def _():
        ra, rf = inv_rms(a_ref), inv_rms(f_ref)
        for t in range(n_tiles):
            column_tile(t, slice(t * OUT_TN, (t + 1) * OUT_TN), ra, rf)
        finish()


def _out_proj(a, f, x2, ag, fg, w, g2):
    t, d = x2.shape
    n_tiles = d // OUT_TN
    row_block = lambda s: (jnp.maximum(s - (n_tiles - 1), 0), 0)
    row = lambda width: pl.BlockSpec((OUT_TM, width), row_block)
    mix_gain = jnp.concatenate([ag, fg], axis=1).reshape(w.shape[0], 1)
    return pl.pallas_call(
        functools.partial(_out_proj_kernel, n_tiles=n_tiles),
        grid=(n_tiles + t // OUT_TM - 1,),
        in_specs=[
            row(ATTN_WIDTH), row(FOURIER_WIDTH), row(d),
            pl.BlockSpec((w.shape[0], 1), lambda s: (0, 0)),
            pl.BlockSpec((w.shape[0], OUT_TN), lambda s: (0, jnp.minimum(s, n_tiles - 1))),
            pl.BlockSpec((1, d), lambda s: (0, 0)),
        ],
        out_specs=[row(d), row(d)],
        out_shape=[jax.ShapeDtypeStruct((t, d), F32), jax.ShapeDtypeStruct((t, d), BF16)],
        scratch_shapes=[pltpu.VMEM((n_tiles, w.shape[0], OUT_TN), BF16)],
        compiler_params=_params("arbitrary"),
        name="out_proj",
    )(a, f, x2, mix_gain, w, g2)


def _ffn_kernel(u_ref, wg_ref, wv_ref, cwg_ref, cwv_ref, cbg_ref, cbv_ref, wd_ref, h_ref, g_ref,
                y_ref, acc_ref, gate_scr, val_scr, *, n_ff):
    j = pl.program_id(1)
    seq = u_ref.shape[0]
    assert sum(FFN_ROW_SPLIT) == seq
    starts = [sum(FFN_ROW_SPLIT[:c]) for c in range(len(FFN_ROW_SPLIT))]

    @pl.when((pl.program_id(0) == 0) & (j == 0))
    def _():
        acc_ref[...] = jnp.zeros_like(acc_ref)
        gate_scr[...] = jnp.zeros_like(gate_scr)
        val_scr[...] = jnp.zeros_like(val_scr)

    @pl.when(j < n_ff)
    def _():
        wg, wv, wd = wg_ref[...], wv_ref[...], wd_ref[...]
        cwg, cwv, cbg, cbv = cwg_ref[...], cwv_ref[...], cbg_ref[...], cbv_ref[...]

        def conv3(scr, cw, cb, r0, rc):
            prev = scr[pl.ds(FFN_PAD - 1 + r0, rc), :]
            cur = scr[pl.ds(FFN_PAD + r0, rc), :]
            nxt = scr[pl.ds(FFN_PAD + 1 + r0, rc), :]
            return prev * cw[0:1, :] + cur * cw[1:2, :] + nxt * cw[2:3, :] + cb

        def gate_and_down(c):
            r0, rc = starts[c], FFN_ROW_SPLIT[c]
            gate = conv3(gate_scr, cwg, cbg, r0, rc)
            val = conv3(val_scr, cwv, cbv, r0, rc)
            act = (gate * jax.nn.sigmoid(gate) * val).astype(BF16)
            acc_ref[pl.ds(r0, rc), :] += jnp.dot(act, wd, preferred_element_type=F32)

        for c, (r0, rc) in enumerate(zip(starts, FFN_ROW_SPLIT)):
            u = u_ref[pl.ds(r0, rc), :]
            gate_scr[pl.ds(FFN_PAD + r0, rc), :] = jnp.dot(u, wg, preferred_element_type=F32)
            val_scr[pl.ds(FFN_PAD + r0, rc), :] = jnp.dot(u, wv, preferred_element_type=F32)
            if c >= 1:
                gate_and_down(c - 1)
        gate_and_down(len(FFN_ROW_SPLIT) - 1)

        n_out = seq // FFN_OUT_ROWS
        rows = pl.ds(pl.multiple_of(jnp.minimum(j, n_out - 1) * FFN_OUT_ROWS, FFN_OUT_ROWS), FFN_OUT_ROWS)
        acc_ref[rows, :] += jnp.where(j < n_out, h_ref[...], 0.0)

    @pl.when(j >= n_ff)
    def _():
        rows = pl.ds(pl.multiple_of((j - n_ff) * FFN_OUT_ROWS, FFN_OUT_ROWS), FFN_OUT_ROWS)
        y_ref[...] = _rms(acc_ref[rows, :]) * g_ref[...]
        acc_ref[rows, :] = jnp.zeros((FFN_OUT_ROWS, acc_ref.shape[1]), F32)


def _ffn(u2, w_up, conv_w, conv_b, w_down, h1, g, batch, seq):
    t, d = u2.shape
    d_ff = w_down.shape[0]
    n_ff = d_ff // FFN_TN
    n_out = seq // FFN_OUT_ROWS
    taps = conv_w.shape[0]
    ff = lambda j: jnp.minimum(j, n_ff - 1)
    gate_col = lambda b, j: (0, ff(j))
    val_col = lambda b, j: (0, n_ff + ff(j))
    out_row = lambda b, j: (b * n_out + jnp.maximum(j - n_ff, 0), 0)
    res_row = lambda b, j: (b * n_out + jnp.minimum(j, n_out - 1), 0)
    assert n_out <= n_ff
    return pl.pallas_call(
        functools.partial(_ffn_kernel, n_ff=n_ff),
        grid=(batch, n_ff + n_out),
        in_specs=[
            pl.BlockSpec((seq, d), lambda b, j: (b, 0), pipeline_mode=pl.Buffered(1)),
            pl.BlockSpec((d, FFN_TN), gate_col),
            pl.BlockSpec((d, FFN_TN), val_col),
            pl.BlockSpec((taps, FFN_TN), gate_col),
            pl.BlockSpec((taps, FFN_TN), val_col),
            pl.BlockSpec((1, FFN_TN), gate_col),
            pl.BlockSpec((1, FFN_TN), val_col),
            pl.BlockSpec((FFN_TN, d), lambda b, j: (ff(j), 0)),
            pl.BlockSpec((FFN_OUT_ROWS, d), res_row),
            pl.BlockSpec((1, d), lambda b, j: (0, 0)),
        ],
        out_specs=pl.BlockSpec((FFN_OUT_ROWS, d), out_row),
        out_shape=jax.ShapeDtypeStruct((t, d), F32),
        scratch_shapes=[
            pltpu.VMEM((seq, d), F32),
            pltpu.VMEM((seq + 2 * FFN_PAD, FFN_TN), F32),
            pltpu.VMEM((seq + 2 * FFN_PAD, FFN_TN), F32),
        ],
        compiler_params=pltpu.CompilerParams(dimension_semantics=("arbitrary", "arbitrary"),
                                             vmem_limit_bytes=FFN_VMEM_LIMIT_BYTES),
        name="ffn",
    )(u2, w_up, w_up, conv_w, conv_w, conv_b, conv_b, w_down, h1, g)


def _rope_tables(seq):
    t = np.arange(seq)
    inv_freq = ROPE_THETA ** (-np.arange(ROPE_PAIRS, dtype=np.float64) / ROPE_PAIRS)

    def axis_tables(pos):
        ang = pos[:, None] * inv_freq[None, :]
        ang = np.concatenate([ang, ang], axis=-1)
        return np.cos(ang), np.sin(ang)

    cos_r, sin_r = axis_tables(t // GRID_W)
    cos_c, sin_c = axis_tables(t % GRID_W)
    cos = np.concatenate([cos_r, cos_c], axis=-1)
    sin = np.concatenate([sin_r, sin_c], axis=-1)
    first_half = (np.arange(HEAD_DIM) % (2 * ROPE_PAIRS)) < ROPE_PAIRS
    sin_lo = np.where(first_half[None, :], -sin, 0.0)
    sin_hi = np.where(first_half[None, :], 0.0, sin)
    return tuple(jnp.asarray(tab, F32) for tab in (cos, sin_lo, sin_hi))


def _dft_cos_sin(rows, cols, n, scale=1.0):
    ang = ((rows[:, None] * cols[None, :]) % n) * (2.0 * math.pi / n)
    return jnp.asarray(np.cos(ang) * scale, F32), jnp.asarray(np.sin(ang) * scale, F32)


def _dft_tables(n, scale):
    return _dft_cos_sin(np.arange(n), np.arange(n), n, scale)


def kernel(x, norm1_g, w_in, q_norm_g, k_norm_g, w_fmix, attn_out_g, fourier_out_g, w_out,
           norm2_g, w_up, conv_w, conv_b, w_down, final_g):
    batch, seq, d = x.shape
    depth = w_in.shape[0]
    rope = _rope_tables(seq)
    cc, sc = _dft_tables(FOURIER_GROUP_DIM, 1.0 / math.sqrt(seq * FOURIER_GROUP_DIM))

    assert depth == 1
    h = x.reshape(batch * seq, d)
    for l in range(depth):
        proj = _in_proj(h, norm1_g[l][None], w_in[l])
        attn, w_up_bf16, w_down_bf16 = _attention(proj, rope, q_norm_g[l][None], k_norm_g[l][None],
                                                  w_up[l], w_down[l], batch, seq)
        ab = _fourier_fold(w_fmix[l], cc, sc)
        four = _fourier(proj, ab, batch, seq)
        h1, u2 = _out_proj(attn, four, h, attn_out_g[l][None], fourier_out_g[l][None],
                           w_out[l], norm2_g[l][None])
        h = _ffn(u2, w_up_bf16, conv_w[l], conv_b[l][None], w_down_bf16, h1, final_g[None], batch, seq)
    return h.reshape(batch, seq, d)
```

```python
import functools
import math

import jax
import jax.numpy as jnp
import numpy as np
from jax import lax
from jax.experimental import pallas as pl
from jax.experimental.pallas import tpu as pltpu

F32 = jnp.float32
BF16 = jnp.bfloat16

HEAD_DIM = 128
N_Q_HEADS = 8
N_KV_HEADS = 2
Q_PER_KV = N_Q_HEADS // N_KV_HEADS
ATTN_WIDTH = N_Q_HEADS * HEAD_DIM
KV_WIDTH = N_KV_HEADS * HEAD_DIM
FOURIER_GROUPS = 8
FOURIER_GROUP_DIM = 128
FOURIER_WIDTH = FOURIER_GROUPS * FOURIER_GROUP_DIM
GRID_W = 64
ROPE_THETA = 10000.0
ROPE_PAIRS = HEAD_DIM // 4
EPS = 1e-6

VMEM_LIMIT_BYTES = 56 * 1024 * 1024

IN_TM, IN_TN, IN_ROWS = 1024, 512, 512
ATTN_TQ, ATTN_ROWS = 1024, 256
BF16_SUBLANES = 16
FOURIER_TM = 512
DFT_SUB = 64
OUT_TM, OUT_TN = 512, 512
FFN_TN = 512
FFN_ROW_SPLIT = (1280, 768)
FFN_PAD = 8
FFN_OUT_ROWS = 256
FFN_Y_ROWS = 128
FFN_VMEM_LIMIT_BYTES = 60 * 1024 * 1024


def _params(*sem):
    return pltpu.CompilerParams(dimension_semantics=sem, vmem_limit_bytes=VMEM_LIMIT_BYTES)


def _rms(x):
    return x * lax.rsqrt(jnp.mean(x * x, axis=-1, keepdims=True) + EPS)


def _in_proj_kernel(x_ref, g_ref, w_ref, o_ref, u_scr, w_scr, *, n_tiles):
    s = pl.program_id(0)

    def normed(rows):
        return (_rms(x_ref[rows, :]) * g_ref[...]).astype(BF16)

    @pl.when(s == 0)
    def _():
        u_scr[...] = normed(slice(None))

    @pl.when(s < n_tiles)
    def _():
        w_scr[s] = w_ref[...].astype(BF16)
        cols = pl.ds(pl.multiple_of(s * IN_TN, IN_TN), IN_TN)
        o_ref[:, cols] = jnp.dot(u_scr[...], w_scr[s], preferred_element_type=F32).astype(o_ref.dtype)

    @pl.when(s >= n_tiles)
    def _():
        for c in range(IN_TM // IN_ROWS):
            rows = slice(c * IN_ROWS, (c + 1) * IN_ROWS)
            u = normed(rows)
            for t in range(n_tiles):
                o_ref[rows, t * IN_TN:(t + 1) * IN_TN] = jnp.dot(
                    u, w_scr[t], preferred_element_type=F32).astype(o_ref.dtype)


def _in_proj(x2, g, w):
    t, d = x2.shape
    n = w.shape[1]
    n_tiles = n // IN_TN
    row_block = lambda s: (jnp.maximum(s - (n_tiles - 1), 0), 0)
    return pl.pallas_call(
        functools.partial(_in_proj_kernel, n_tiles=n_tiles),
        grid=(n_tiles + t // IN_TM - 1,),
        in_specs=[
            pl.BlockSpec((IN_TM, d), row_block),
            pl.BlockSpec((1, d), lambda s: (0, 0)),
            pl.BlockSpec((d, IN_TN), lambda s: (0, jnp.minimum(s, n_tiles - 1))),
        ],
        out_specs=pl.BlockSpec((IN_TM, n), row_block),
        out_shape=jax.ShapeDtypeStruct((t, n), BF16),
        scratch_shapes=[pltpu.VMEM((IN_TM, d), BF16), pltpu.VMEM((n_tiles, d, IN_TN), BF16)],
        compiler_params=_params("arbitrary"),
        name="in_proj",
    )(x2, g, w)


def _rope(x, cos, sin_lo, sin_hi):
    return (x * cos
            + pltpu.roll(x, HEAD_DIM - ROPE_PAIRS, axis=1) * sin_lo
            + pltpu.roll(x, ROPE_PAIRS, axis=1) * sin_hi)


def _attn_kernel(q_ref, k_ref, v_ref, cq_ref, slq_ref, shq_ref, ck_ref, slk_ref, shk_ref,
                 qg_ref, kg_ref, wu_ref, wd_ref, o_ref, wu_out, wd_out, k_scr, v_scr):
    @pl.when(pl.program_id(2) == 0)
    def _():
        k = _rms(k_ref[...].astype(F32)) * kg_ref[...]
        k_scr[...] = _rope(k, ck_ref[...], slk_ref[...], shk_ref[...]).astype(BF16)
        v_scr[:, :HEAD_DIM] = v_ref[...]
        v_scr[:, HEAD_DIM:] = jnp.ones((v_ref.shape[0], HEAD_DIM), BF16)

    scale = math.log2(math.e) / math.sqrt(HEAD_DIM)
    cos, sin_lo, sin_hi = cq_ref[...], slq_ref[...], shq_ref[...]
    kk = k_scr[...]
    vv = v_scr[...]
    for r in range(ATTN_TQ // ATTN_ROWS):
        rows = slice(r * ATTN_ROWS, (r + 1) * ATTN_ROWS)
        for h in range(Q_PER_KV):
            sl = slice(h * HEAD_DIM, (h + 1) * HEAD_DIM)
            q = _rms(q_ref[rows, sl].astype(F32)) * qg_ref[...]
            q = (_rope(q, cos[rows], sin_lo[rows], sin_hi[rows]) * scale).astype(BF16)
            s = lax.dot_general(q, kk, (((1,), (1,)), ((), ())), preferred_element_type=F32)
            p = jnp.exp2(s - jnp.max(s, axis=-1, keepdims=True))
            o = jnp.dot(p.astype(BF16), vv, preferred_element_type=F32)
            o_ref[rows, sl] = (o[:, :HEAD_DIM] / o[:, HEAD_DIM:]).astype(o_ref.dtype)

    wu_out[...] = wu_ref[...].astype(BF16)
    wd_out[...] = wd_ref[...].astype(BF16)


def _attention(proj, tabs, qg, kg, w_up, w_down, batch, seq):
    t = proj.shape[0]
    nq = seq // ATTN_TQ
    gw = Q_PER_KV * HEAD_DIM
    k_blk0 = ATTN_WIDTH // HEAD_DIM
    v_blk0 = (ATTN_WIDTH + KV_WIDTH) // HEAD_DIM
    cos, sin_lo, sin_hi = tabs
    q_tab = pl.BlockSpec((ATTN_TQ, HEAD_DIM), lambda b, g, i: (i, 0))
    k_tab = pl.BlockSpec((seq, HEAD_DIM), lambda b, g, i: (0, 0))
    gain = pl.BlockSpec((1, HEAD_DIM), lambda b, g, i: (0, 0))

    n_steps = batch * N_KV_HEADS * nq
    step = lambda b, g, i: (b * N_KV_HEADS + g) * nq + i
    up_rows, down_rows = w_up.shape[0] // n_steps, w_down.shape[0] // n_steps
    assert up_rows * n_steps == w_up.shape[0] and up_rows % BF16_SUBLANES == 0
    assert down_rows * n_steps == w_down.shape[0] and down_rows % BF16_SUBLANES == 0
    up_blk = pl.BlockSpec((up_rows, w_up.shape[1]), lambda b, g, i: (step(b, g, i), 0))
    down_blk = pl.BlockSpec((down_rows, w_down.shape[1]), lambda b, g, i: (step(b, g, i), 0))
    return pl.pallas_call(
        _attn_kernel,
        grid=(batch, N_KV_HEADS, nq),
        in_specs=[
            pl.BlockSpec((ATTN_TQ, gw), lambda b, g, i: (b * nq + i, g)),
            pl.BlockSpec((seq, HEAD_DIM), lambda b, g, i: (b, k_blk0 + g)),
            pl.BlockSpec((seq, HEAD_DIM), lambda b, g, i: (b, v_blk0 + g)),
            q_tab, q_tab, q_tab, k_tab, k_tab, k_tab, gain, gain, up_blk, down_blk,
        ],
        out_specs=[pl.BlockSpec((ATTN_TQ, gw), lambda b, g, i: (b * nq + i, g)), up_blk, down_blk],
        out_shape=[jax.ShapeDtypeStruct((t, ATTN_WIDTH), BF16),
                   jax.ShapeDtypeStruct(w_up.shape, BF16), jax.ShapeDtypeStruct(w_down.shape, BF16)],
        scratch_shapes=[pltpu.VMEM((seq, HEAD_DIM), BF16), pltpu.VMEM((seq, 2 * HEAD_DIM), BF16)],
        compiler_params=_params("arbitrary", "arbitrary", "arbitrary"),
        name="attention",
    )(proj, proj, proj, cos, sin_lo, sin_hi, cos, sin_lo, sin_hi, qg, kg, w_up, w_down)


def _fourier_fold_kernel(w_ref, cc_ref, sc_ref, o_ref):
    for g in range(FOURIER_GROUPS):
        w = w_ref[g]
        a = jnp.dot(cc_ref[...], w, preferred_element_type=F32, precision=lax.Precision.HIGHEST)
        b = jnp.dot(sc_ref[...], w, preferred_element_type=F32, precision=lax.Precision.HIGHEST)
        o_ref[g, :, :FOURIER_GROUP_DIM] = a.astype(o_ref.dtype)
        o_ref[g, :, FOURIER_GROUP_DIM:] = b.astype(o_ref.dtype)


def _fourier_fold(w_fmix, cc, sc):
    return pl.pallas_call(
        _fourier_fold_kernel,
        out_shape=jax.ShapeDtypeStruct((FOURIER_GROUPS, FOURIER_GROUP_DIM, 2 * FOURIER_GROUP_DIM), BF16),
        name="fourier_fold",
    )(w_fmix, cc, sc)


def _fourier_kernel(flo_ref, fhi_ref, ab_ref, ca_ref, sa_ref, cr_ref, sr_ref, o_ref,
                    za_scr, zb_scr, z_stage, c_scr, ms_scr):
    m = pl.program_id(1)
    half = za_scr.shape[0] // 2
    rows = pl.ds(pl.multiple_of(m * FOURIER_TM, FOURIER_TM), FOURIER_TM)

    @pl.when(pl.program_id(0) == 0)
    def _():
        cr, sr = cr_ref[...], sr_ref[...]
        per_block = FOURIER_TM // DFT_SUB
        for i in range(per_block):
            ca = ca_ref[pl.ds(m * per_block + i, 1), :]
            sa = sa_ref[pl.ds(m * per_block + i, 1), :]
            sub = pl.ds(pl.multiple_of(m * FOURIER_TM + i * DFT_SUB, DFT_SUB), DFT_SUB)
            c_scr[sub, :] = (ca * cr - sa * sr).astype(BF16)
            ms_scr[sub, :] = (-(sa * cr + ca * sr)).astype(BF16)

    @pl.when(m == 0)
    def _():
        groups_per_ref = FOURIER_GROUPS // 2
        for g in range(FOURIER_GROUPS):
            src = flo_ref if g < groups_per_ref else fhi_ref
            lo = (g % groups_per_ref) * FOURIER_GROUP_DIM
            z = jnp.dot(src[:, lo:lo + FOURIER_GROUP_DIM], ab_ref[g], preferred_element_type=F32)
            z_stage[0] = z[:, :FOURIER_GROUP_DIM]
            z_stage[1] = z[:, FOURIER_GROUP_DIM:]
            dst = slice(g * FOURIER_GROUP_DIM, (g + 1) * FOURIER_GROUP_DIM)
            for parity in range(2):
                part = slice(parity * half, (parity + 1) * half)
                za_scr[part, dst] = z_stage[0, pl.ds(parity, half, stride=2), :].astype(BF16)
                zb_scr[part, dst] = z_stage[1, pl.ds(parity, half, stride=2), :].astype(BF16)

    def half_sum(cols):
        return (jnp.dot(c_scr[rows, cols], za_scr[cols, :], preferred_element_type=F32)
                + jnp.dot(ms_scr[rows, cols], zb_scr[cols, :], preferred_element_type=F32))

    even, odd = half_sum(slice(0, half)), half_sum(slice(half, 2 * half))
    o_ref[0] = (even + odd).astype(o_ref.dtype)
    o_ref[1] = (even - odd).astype(o_ref.dtype)


def _fourier(proj, ab, batch, seq):
    t = proj.shape[0]
    half = seq // 2
    nm = half // FOURIER_TM
    half_w = FOURIER_WIDTH // 2
    f_blk0 = (ATTN_WIDTH + 2 * KV_WIDTH) // half_w
    cols = np.concatenate([np.arange(0, seq, 2), np.arange(1, seq, 2)])
    ca, sa = _dft_cos_sin(np.arange(half // DFT_SUB) * DFT_SUB, cols, seq)
    cr, sr = _dft_cos_sin(np.arange(DFT_SUB), cols, seq)
    whole = lambda arr: pl.BlockSpec(arr.shape, lambda b, m: (0,) * arr.ndim)
    out = pl.pallas_call(
        _fourier_kernel,
        grid=(batch, nm),
        in_specs=[
            pl.BlockSpec((seq, half_w), lambda b, m: (b, f_blk0)),
            pl.BlockSpec((seq, half_w), lambda b, m: (b, f_blk0 + 1)),
            whole(ab), whole(ca), whole(sa), whole(cr), whole(sr),
        ],
        out_specs=pl.BlockSpec((None, 2, FOURIER_TM, FOURIER_WIDTH), lambda b, m: (b, 0, m, 0)),
        out_shape=jax.ShapeDtypeStruct((batch, 2, half, FOURIER_WIDTH), BF16),
        scratch_shapes=[pltpu.VMEM((seq, FOURIER_WIDTH), BF16), pltpu.VMEM((seq, FOURIER_WIDTH), BF16),
                        pltpu.VMEM((2, seq, FOURIER_GROUP_DIM), F32),
                        pltpu.VMEM((half, seq), BF16), pltpu.VMEM((half, seq), BF16)],
        compiler_params=_params("arbitrary", "arbitrary"),
        name="fourier",
    )(proj, proj, ab, ca, sa, cr, sr)
    return out.reshape(t, FOURIER_WIDTH)


def _out_proj_kernel(a_ref, f_ref, x_ref, mg_ref, w_ref, g2_ref, h_ref, u_ref, w_scr, *, n_tiles):
    s = pl.program_id(0)

    def inv_rms(ref):
        v = ref[...].astype(F32)
        return lax.rsqrt(jnp.mean(v * v, axis=-1, keepdims=True) + EPS)

    def column_tile(t, cols, ra, rf):
        h_ref[:, cols] = (x_ref[:, cols]
                          + ra * jnp.dot(a_ref[...], w_scr[t, :ATTN_WIDTH, :], preferred_element_type=F32)
                          + rf * jnp.dot(f_ref[...], w_scr[t, ATTN_WIDTH:, :], preferred_element_type=F32))

    def finish():
        u_ref[...] = (_rms(h_ref[...]) * g2_ref[...]).astype(u_ref.dtype)

    @pl.when(s < n_tiles)
    def _():
        w_scr[s] = (w_ref[...] * mg_ref[...]).astype(BF16)
        column_tile(s, pl.ds(pl.multiple_of(s * OUT_TN, OUT_TN), OUT_TN), inv_rms(a_ref), inv_rms(f_ref))

    @pl.when(s == n_tiles - 1)
    def _():
        finish()

    @pl.when(s >= n_tiles)
    def _():
        ra, rf = inv_rms(a_ref), inv_rms(f_ref)
        for t in range(n_tiles):
            column_tile(t, slice(t * OUT_TN, (t + 1) * OUT_TN), ra, rf)
        finish()


def _out_proj(a, f, x2, ag, fg, w, g2):
    t, d = x2.shape
    n_tiles = d // OUT_TN
    row_block = lambda s: (jnp.maximum(s - (n_tiles - 1), 0), 0)
    row = lambda width: pl.BlockSpec((OUT_TM, width), row_block)
    mix_gain = jnp.concatenate([ag, fg], axis=1).reshape(w.shape[0], 1)
    return pl.pallas_call(
        functools.partial(_out_proj_kernel, n_tiles=n_tiles),
        grid=(n_tiles + t // OUT_TM - 1,),
        in_specs=[
            row(ATTN_WIDTH), row(FOURIER_WIDTH), row(d),
            pl.BlockSpec((w.shape[0], 1), lambda s: (0, 0)),
            pl.BlockSpec((w.shape[0], OUT_TN), lambda s: (0, jnp.minimum(s, n_tiles - 1))),
            pl.BlockSpec((1, d), lambda s: (0, 0)),
        ],
        out_specs=[row(d), row(d)],
        out_shape=[jax.ShapeDtypeStruct((t, d), F32), jax.ShapeDtypeStruct((t, d), BF16)],
        scratch_shapes=[pltpu.VMEM((n_tiles, w.shape[0], OUT_TN), BF16)],
        compiler_params=_params("arbitrary"),
        name="out_proj",
    )(a, f, x2, mix_gain, w, g2)


def _ffn_kernel(u_hbm, wg_ref, wv_ref, cwg_ref, cwv_ref, cbg_ref, cbv_ref, wd_ref, h_ref, g_ref,
                y_hbm, acc_ref, gate_scr, val_scr, u_ref, y_buf, u_sem, y_sem):
    b, j = pl.program_id(0), pl.program_id(1)
    n_seq, n_ff = pl.num_programs(0), pl.num_programs(1)
    seq = u_ref.shape[0]
    n_out = seq // FFN_OUT_ROWS
    assert sum(FFN_ROW_SPLIT) == seq
    starts = [sum(FFN_ROW_SPLIT[:c]) for c in range(len(FFN_ROW_SPLIT))]

    def u_copy(seq_idx):
        return pltpu.make_async_copy(u_hbm.at[pl.ds(seq_idx * seq, seq), :], u_ref, u_sem)

    def y_copy(piece, slot):
        dst = y_hbm.at[pl.ds(b * seq + piece * FFN_Y_ROWS, FFN_Y_ROWS), :]
        return pltpu.make_async_copy(y_buf.at[slot], dst, y_sem.at[slot])

    @pl.when(j == 0)
    def _():
        @pl.when(b == 0)
        def _():
            acc_ref[...] = jnp.zeros_like(acc_ref)
            gate_scr[...] = jnp.zeros_like(gate_scr)
            val_scr[...] = jnp.zeros_like(val_scr)
            u_copy(0).start()

        u_copy(b).wait()

    wg, wv, wd = wg_ref[...], wv_ref[...], wd_ref[...]
    cwg, cwv, cbg, cbv = cwg_ref[...], cwv_ref[...], cbg_ref[...], cbv_ref[...]

    def conv3(scr, cw, cb, r0, rc):
        prev = scr[pl.ds(FFN_PAD - 1 + r0, rc), :]
        cur = scr[pl.ds(FFN_PAD + r0, rc), :]
        nxt = scr[pl.ds(FFN_PAD + 1 + r0, rc), :]
        return prev * cw[0:1, :] + cur * cw[1:2, :] + nxt * cw[2:3, :] + cb

    def gate_and_down(c):
        r0, rc = starts[c], FFN_ROW_SPLIT[c]
        gate = conv3(gate_scr, cwg, cbg, r0, rc)
        val = conv3(val_scr, cwv, cbv, r0, rc)
        act = (gate * jax.nn.sigmoid(gate) * val).astype(BF16)
        acc_ref[pl.ds(r0, rc), :] += jnp.dot(act, wd, preferred_element_type=F32)

    for c, (r0, rc) in enumerate(zip(starts, FFN_ROW_SPLIT)):
        u = u_ref[pl.ds(r0, rc), :]
        gate_scr[pl.ds(FFN_PAD + r0, rc), :] = jnp.dot(u, wg, preferred_element_type=F32)
        val_scr[pl.ds(FFN_PAD + r0, rc), :] = jnp.dot(u, wv, preferred_element_type=F32)
        if c >= 1:
            gate_and_down(c - 1)
    gate_and_down(len(FFN_ROW_SPLIT) - 1)

    rows = pl.ds(pl.multiple_of(jnp.minimum(j, n_out - 1) * FFN_OUT_ROWS, FFN_OUT_ROWS), FFN_OUT_ROWS)
    acc_ref[rows, :] += jnp.where(j < n_out, h_ref[...], 0.0)

    @pl.when(j == n_ff - 1)
    def _():
        @pl.when(b + 1 < n_seq)
        def _():
            u_copy(b + 1).start()

        n_pieces = seq // FFN_Y_ROWS
        for piece in range(n_pieces):
            slot = piece % 2
            if piece >= 2:
                y_copy(piece - 2, slot).wait()
            rows = slice(piece * FFN_Y_ROWS, (piece + 1) * FFN_Y_ROWS)
            y_buf[slot] = _rms(acc_ref[rows, :]) * g_ref[...]
            acc_ref[rows, :] = jnp.zeros((FFN_Y_ROWS, acc_ref.shape[1]), F32)
            y_copy(piece, slot).start()
        for piece in range(max(n_pieces - 2, 0), n_pieces):
            y_copy(piece, piece % 2).wait()


def _ffn(u2, w_up, conv_w, conv_b, w_down, h1, g, batch, seq):
    t, d = u2.shape
    d_ff = w_down.shape[0]
    n_ff = d_ff // FFN_TN
    n_out = seq // FFN_OUT_ROWS
    taps = conv_w.shape[0]
    gate_col = lambda b, j: (0, j)
    val_col = lambda b, j: (0, n_ff + j)
    res_row = lambda b, j: (b * n_out + jnp.minimum(j, n_out - 1), 0)
    assert n_out <= n_ff
    return pl.pallas_call(
        _ffn_kernel,
        grid=(batch, n_ff),
        in_specs=[
            pl.BlockSpec(memory_space=pl.ANY),
            pl.BlockSpec((d, FFN_TN), gate_col),
            pl.BlockSpec((d, FFN_TN), val_col),
            pl.BlockSpec((taps, FFN_TN), gate_col),
            pl.BlockSpec((taps, FFN_TN), val_col),
            pl.BlockSpec((1, FFN_TN), gate_col),
            pl.BlockSpec((1, FFN_TN), val_col),
            pl.BlockSpec((FFN_TN, d), lambda b, j: (j, 0)),
            pl.BlockSpec((FFN_OUT_ROWS, d), res_row),
            pl.BlockSpec((1, d), lambda b, j: (0, 0)),
        ],
        out_specs=pl.BlockSpec(memory_space=pl.ANY),
        out_shape=jax.ShapeDtypeStruct((t, d), F32),
        scratch_shapes=[
            pltpu.VMEM((seq, d), F32),
            pltpu.VMEM((seq + 2 * FFN_PAD, FFN_TN), F32),
            pltpu.VMEM((seq + 2 * FFN_PAD, FFN_TN), F32),
            pltpu.VMEM((seq, d), BF16),
            pltpu.VMEM((2, FFN_Y_ROWS, d), F32),
            pltpu.SemaphoreType.DMA(()),
            pltpu.SemaphoreType.DMA((2,)),
        ],
        compiler_params=pltpu.CompilerParams(dimension_semantics=("arbitrary", "arbitrary"),
                                             vmem_limit_bytes=FFN_VMEM_LIMIT_BYTES),
        name="ffn",
    )(u2, w_up, w_up, conv_w, conv_w, conv_b, conv_b, w_down, h1, g)


def _rope_tables(seq):
    t = np.arange(seq)
    inv_freq = ROPE_THETA ** (-np.arange(ROPE_PAIRS, dtype=np.float64) / ROPE_PAIRS)

    def axis_tables(pos):
        ang = pos[:, None] * inv_freq[None, :]
        ang = np.concatenate([ang, ang], axis=-1)
        return np.cos(ang), np.sin(ang)

    cos_r, sin_r = axis_tables(t // GRID_W)
    cos_c, sin_c = axis_tables(t % GRID_W)
    cos = np.concatenate([cos_r, cos_c], axis=-1)
    sin = np.concatenate([sin_r, sin_c], axis=-1)
    first_half = (np.arange(HEAD_DIM) % (2 * ROPE_PAIRS)) < ROPE_PAIRS
    sin_lo = np.where(first_half[None, :], -sin, 0.0)
    sin_hi = np.where(first_half[None, :], 0.0, sin)
    return tuple(jnp.asarray(tab, F32) for tab in (cos, sin_lo, sin_hi))


def _dft_cos_sin(rows, cols, n, scale=1.0):
    ang = ((rows[:, None] * cols[None, :]) % n) * (2.0 * math.pi / n)
    return jnp.asarray(np.cos(ang) * scale, F32), jnp.asarray(np.sin(ang) * scale, F32)


def _dft_tables(n, scale):
    return _dft_cos_sin(np.arange(n), np.arange(n), n, scale)


def kernel(x, norm1_g, w_in, q_norm_g, k_norm_g, w_fmix, attn_out_g, fourier_out_g, w_out,
           norm2_g, w_up, conv_w, conv_b, w_down, final_g):
    batch, seq, d = x.shape
    depth = w_in.shape[0]
    rope = _rope_tables(seq)
    cc, sc = _dft_tables(FOURIER_GROUP_DIM, 1.0 / math.sqrt(seq * FOURIER_GROUP_DIM))

    assert depth == 1
    h = x.reshape(batch * seq, d)
    for l in range(depth):
        proj = _in_proj(h, norm1_g[l][None], w_in[l])
        attn, w_up_bf16, w_down_bf16 = _attention(proj, rope, q_norm_g[l][None], k_norm_g[l][None],
                                                  w_up[l], w_down[l], batch, seq)
        ab = _fourier_fold(w_fmix[l], cc, sc)
        four = _fourier(proj, ab, batch, seq)
        h1, u2 = _out_proj(attn, four, h, attn_out_g[l][None], fourier_out_g[l][None],
                           w_out[l], norm2_g[l][None])
        h = _ffn(u2, w_up_bf16, conv_w[l], conv_b[l][None], w_down_bf16, h1, final_g[None], batch, seq)
    return h.reshape(batch, seq, d)
```

```python
import functools
import math

import jax
import jax.numpy as jnp
import numpy as np
from jax import lax
from jax.experimental import pallas as pl
from jax.experimental.pallas import tpu as pltpu

F32 = jnp.float32
BF16 = jnp.bfloat16

HEAD_DIM = 128
N_Q_HEADS = 8
N_KV_HEADS = 2
Q_PER_KV = N_Q_HEADS // N_KV_HEADS
ATTN_WIDTH = N_Q_HEADS * HEAD_DIM
KV_WIDTH = N_KV_HEADS * HEAD_DIM
FOURIER_GROUPS = 8
FOURIER_GROUP_DIM = 128
FOURIER_WIDTH = FOURIER_GROUPS * FOURIER_GROUP_DIM
GRID_W = 64
ROPE_THETA = 10000.0
ROPE_PAIRS = HEAD_DIM // 4
EPS = 1e-6

VMEM_LIMIT_BYTES = 56 * 1024 * 1024

IN_TM, IN_TN, IN_ROWS = 1024, 512, 512
ATTN_TQ, ATTN_ROWS = 1024, 256
BF16_SUBLANES = 16
FOURIER_TM = 512
DFT_SUB = 64
OUT_TM, OUT_TN = 512, 512
FFN_TN = 512
FFN_ROW_SPLIT = (1280, 768)
FFN_PAD = 8
FFN_OUT_ROWS = 256
FFN_Y_ROWS = 128
FFN_VMEM_LIMIT_BYTES = 60 * 1024 * 1024


def _params(*sem):
    return pltpu.CompilerParams(dimension_semantics=sem, vmem_limit_bytes=VMEM_LIMIT_BYTES)


def _rms(x):
    return x * lax.rsqrt(jnp.mean(x * x, axis=-1, keepdims=True) + EPS)


def _in_proj_kernel(x_ref, g_ref, w_ref, o_ref, u_scr, w_scr, *, n_tiles):
    s = pl.program_id(0)

    def normed(rows):
        return (_rms(x_ref[rows, :]) * g_ref[...]).astype(BF16)

    @pl.when(s == 0)
    def _():
        u_scr[...] = normed(slice(None))

    @pl.when(s < n_tiles)
    def _():
        w_scr[s] = w_ref[...].astype(BF16)
        cols = pl.ds(pl.multiple_of(s * IN_TN, IN_TN), IN_TN)
        o_ref[:, cols] = jnp.dot(u_scr[...], w_scr[s], preferred_element_type=F32).astype(o_ref.dtype)

    @pl.when(s >= n_tiles)
    def _():
        for c in range(IN_TM // IN_ROWS):
            rows = slice(c * IN_ROWS, (c + 1) * IN_ROWS)
            u = normed(rows)
            for t in range(n_tiles):
                o_ref[rows, t * IN_TN:(t + 1) * IN_TN] = jnp.dot(
                    u, w_scr[t], preferred_element_type=F32).astype(o_ref.dtype)


def _in_proj(x2, g, w):
    t, d = x2.shape
    n = w.shape[1]
    n_tiles = n // IN_TN
    row_block = lambda s: (jnp.maximum(s - (n_tiles - 1), 0), 0)
    return pl.pallas_call(
        functools.partial(_in_proj_kernel, n_tiles=n_tiles),
        grid=(n_tiles + t // IN_TM - 1,),
        in_specs=[
            pl.BlockSpec((IN_TM, d), row_block),
            pl.BlockSpec((1, d), lambda s: (0, 0)),
            pl.BlockSpec((d, IN_TN), lambda s: (0, jnp.minimum(s, n_tiles - 1))),
        ],
        out_specs=pl.BlockSpec((IN_TM, n), row_block),
        out_shape=jax.ShapeDtypeStruct((t, n), BF16),
        scratch_shapes=[pltpu.VMEM((IN_TM, d), BF16), pltpu.VMEM((n_tiles, d, IN_TN), BF16)],
        compiler_params=_params("arbitrary"),
        name="in_proj",
    )(x2, g, w)


def _rope(x, cos, sin_lo, sin_hi):
    return (x * cos
            + pltpu.roll(x, HEAD_DIM - ROPE_PAIRS, axis=1) * sin_lo
            + pltpu.roll(x, ROPE_PAIRS, axis=1) * sin_hi)


def _attn_kernel(q_ref, k_ref, v_ref, cq_ref, slq_ref, shq_ref, ck_ref, slk_ref, shk_ref,
                 qg_ref, kg_ref, wu_ref, wd_ref, o_ref, wu_out, wd_out, k_scr, v_scr):
    @pl.when(pl.program_id(2) == 0)
    def _():
        k = _rms(k_ref[...].astype(F32)) * kg_ref[...]
        k_scr[...] = _rope(k, ck_ref[...], slk_ref[...], shk_ref[...]).astype(BF16)
        v_scr[:, :HEAD_DIM] = v_ref[...]
        v_scr[:, HEAD_DIM:] = jnp.ones((v_ref.shape[0], HEAD_DIM), BF16)

    scale = math.log2(math.e) / math.sqrt(HEAD_DIM)
    cos, sin_lo, sin_hi = cq_ref[...], slq_ref[...], shq_ref[...]
    kk = k_scr[...]
    vv = v_scr[...]
    for r in range(ATTN_TQ // ATTN_ROWS):
        rows = slice(r * ATTN_ROWS, (r + 1) * ATTN_ROWS)
        for h in range(Q_PER_KV):
            sl = slice(h * HEAD_DIM, (h + 1) * HEAD_DIM)
            q = _rms(q_ref[rows, sl].astype(F32)) * qg_ref[...]
            q = (_rope(q, cos[rows], sin_lo[rows], sin_hi[rows]) * scale).astype(BF16)
            s = lax.dot_general(q, kk, (((1,), (1,)), ((), ())), preferred_element_type=F32)
            p = jnp.exp2(s - jnp.max(s, axis=-1, keepdims=True))
            o = jnp.dot(p.astype(BF16), vv, preferred_element_type=F32)
            o_ref[rows, sl] = (o[:, :HEAD_DIM] / o[:, HEAD_DIM:]).astype(o_ref.dtype)

    wu_out[...] = wu_ref[...].astype(BF16)
    wd_out[...] = wd_ref[...].astype(BF16)


def _attention(proj, tabs, qg, kg, w_up, w_down, batch, seq):
    t = proj.shape[0]
    nq = seq // ATTN_TQ
    gw = Q_PER_KV * HEAD_DIM
    k_blk0 = ATTN_WIDTH // HEAD_DIM
    v_blk0 = (ATTN_WIDTH + KV_WIDTH) // HEAD_DIM
    cos, sin_lo, sin_hi = tabs
    q_tab = pl.BlockSpec((ATTN_TQ, HEAD_DIM), lambda b, g, i: (i, 0))
    k_tab = pl.BlockSpec((seq, HEAD_DIM), lambda b, g, i: (0, 0))
    gain = pl.BlockSpec((1, HEAD_DIM), lambda b, g, i: (0, 0))

    n_steps = batch * N_KV_HEADS * nq
    step = lambda b, g, i: (b * N_KV_HEADS + g) * nq + i
    up_rows, down_rows = w_up.shape[0] // n_steps, w_down.shape[0] // n_steps
    assert up_rows * n_steps == w_up.shape[0] and up_rows % BF16_SUBLANES == 0
    assert down_rows * n_steps == w_down.shape[0] and down_rows % BF16_SUBLANES == 0
    up_blk = pl.BlockSpec((up_rows, w_up.shape[1]), lambda b, g, i: (step(b, g, i), 0))
    down_blk = pl.BlockSpec((down_rows, w_down.shape[1]), lambda b, g, i: (step(b, g, i), 0))
    return pl.pallas_call(
        _attn_kernel,
        grid=(batch, N_KV_HEADS, nq),
        in_specs=[
            pl.BlockSpec((ATTN_TQ, gw), lambda b, g, i: (b * nq + i, g)),
            pl.BlockSpec((seq, HEAD_DIM), lambda b, g, i: (b, k_blk0 + g)),
            pl.BlockSpec((seq, HEAD_DIM), lambda b, g, i: (b, v_blk0 + g)),
            q_tab, q_tab, q_tab, k_tab, k_tab, k_tab, gain, gain, up_blk, down_blk,
        ],
        out_specs=[pl.BlockSpec((ATTN_TQ, gw), lambda b, g, i: (b * nq + i, g)), up_blk, down_blk],
        out_shape=[jax.ShapeDtypeStruct((t, ATTN_WIDTH), BF16),
                   jax.ShapeDtypeStruct(w_up.shape, BF16), jax.ShapeDtypeStruct(w_down.shape, BF16)],
        scratch_shapes=[pltpu.VMEM((seq, HEAD_DIM), BF16), pltpu.VMEM((seq, 2 * HEAD_DIM), BF16)],
        compiler_params=_params("arbitrary", "arbitrary", "arbitrary"),
        name="attention",
    )(proj, proj, proj, cos, sin_lo, sin_hi, cos, sin_lo, sin_hi, qg, kg, w_up, w_down)


def _fourier_fold_kernel(w_ref, cc_ref, sc_ref, o_ref):
    for g in range(FOURIER_GROUPS):
        w = w_ref[g]
        a = jnp.dot(cc_ref[...], w, preferred_element_type=F32, precision=lax.Precision.HIGHEST)
        b = jnp.dot(sc_ref[...], w, preferred_element_type=F32, precision=lax.Precision.HIGHEST)
        o_ref[g, :, :FOURIER_GROUP_DIM] = a.astype(o_ref.dtype)
        o_ref[g, :, FOURIER_GROUP_DIM:] = b.astype(o_ref.dtype)


def _fourier_fold(w_fmix, cc, sc):
    return pl.pallas_call(
        _fourier_fold_kernel,
        out_shape=jax.ShapeDtypeStruct((FOURIER_GROUPS, FOURIER_GROUP_DIM, 2 * FOURIER_GROUP_DIM), BF16),
        name="fourier_fold",
    )(w_fmix, cc, sc)


def _fourier_kernel(flo_ref, fhi_ref, ab_ref, ca_ref, sa_ref, cr_ref, sr_ref, o_ref,
                    za_scr, zb_scr, z_stage, c_scr, ms_scr):
    m = pl.program_id(1)
    half = za_scr.shape[0] // 2
    rows = pl.ds(pl.multiple_of(m * FOURIER_TM, FOURIER_TM), FOURIER_TM)

    @pl.when(pl.program_id(0) == 0)
    def _():
        cr, sr = cr_ref[...], sr_ref[...]
        per_block = FOURIER_TM // DFT_SUB
        for i in range(per_block):
            ca = ca_ref[pl.ds(m * per_block + i, 1), :]
            sa = sa_ref[pl.ds(m * per_block + i, 1), :]
            sub = pl.ds(pl.multiple_of(m * FOURIER_TM + i * DFT_SUB, DFT_SUB), DFT_SUB)
            c_scr[sub, :] = (ca * cr - sa * sr).astype(BF16)
            ms_scr[sub, :] = (-(sa * cr + ca * sr)).astype(BF16)

    @pl.when(m == 0)
    def _():
        groups_per_ref = FOURIER_GROUPS // 2
        for g in range(FOURIER_GROUPS):
            src = flo_ref if g < groups_per_ref else fhi_ref
            lo = (g % groups_per_ref) * FOURIER_GROUP_DIM
            z = jnp.dot(src[:, lo:lo + FOURIER_GROUP_DIM], ab_ref[g], preferred_element_type=F32)
            z_stage[0] = z[:, :FOURIER_GROUP_DIM]
            z_stage[1] = z[:, FOURIER_GROUP_DIM:]
            dst = slice(g * FOURIER_GROUP_DIM, (g + 1) * FOURIER_GROUP_DIM)
            for parity in range(2):
                part = slice(parity * half, (parity + 1) * half)
                za_scr[part, dst] = z_stage[0, pl.ds(parity, half, stride=2), :].astype(BF16)
                zb_scr[part, dst] = z_stage[1, pl.ds(parity, half, stride=2), :].astype(BF16)

    def half_sum(cols):
        return (jnp.dot(c_scr[rows, cols], za_scr[cols, :], preferred_element_type=F32)
                + jnp.dot(ms_scr[rows, cols], zb_scr[cols, :], preferred_element_type=F32))

    even, odd = half_sum(slice(0, half)), half_sum(slice(half, 2 * half))
    o_ref[0] = (even + odd).astype(o_ref.dtype)
    o_ref[1] = (even - odd).astype(o_ref.dtype)


def _fourier(proj, ab, batch, seq):
    t = proj.shape[0]
    half = seq // 2
    nm = half // FOURIER_TM
    half_w = FOURIER_WIDTH // 2
    f_blk0 = (ATTN_WIDTH + 2 * KV_WIDTH) // half_w
    cols = np.concatenate([np.arange(0, seq, 2), np.arange(1, seq, 2)])
    ca, sa = _dft_cos_sin(np.arange(half // DFT_SUB) * DFT_SUB, cols, seq)
    cr, sr = _dft_cos_sin(np.arange(DFT_SUB), cols, seq)
    whole = lambda arr: pl.BlockSpec(arr.shape, lambda b, m: (0,) * arr.ndim)
    out = pl.pallas_call(
        _fourier_kernel,
        grid=(batch, nm),
        in_specs=[
            pl.BlockSpec((seq, half_w), lambda b, m: (b, f_blk0)),
            pl.BlockSpec((seq, half_w), lambda b, m: (b, f_blk0 + 1)),
            whole(ab), whole(ca), whole(sa), whole(cr), whole(sr),
        ],
        out_specs=pl.BlockSpec((None, 2, FOURIER_TM, FOURIER_WIDTH), lambda b, m: (b, 0, m, 0)),
        out_shape=jax.ShapeDtypeStruct((batch, 2, half, FOURIER_WIDTH), BF16),
        scratch_shapes=[pltpu.VMEM((seq, FOURIER_WIDTH), BF16), pltpu.VMEM((seq, FOURIER_WIDTH), BF16),
                        pltpu.VMEM((2, seq, FOURIER_GROUP_DIM), F32),
                        pltpu.VMEM((half, seq), BF16), pltpu.VMEM((half, seq), BF16)],
        compiler_params=_params("arbitrary", "arbitrary"),
        name="fourier",
    )(proj, proj, ab, ca, sa, cr, sr)
    return out.reshape(t, FOURIER_WIDTH)


def _out_proj_kernel(a_ref, f_ref, x_ref, mg_ref, w_ref, g2_ref, h_ref, u_ref, w_scr, *, n_tiles):
    s = pl.program_id(0)

    def inv_rms(ref):
        v = ref[...].astype(F32)
        return lax.rsqrt(jnp.mean(v * v, axis=-1, keepdims=True) + EPS)

    def column_tile(t, cols, ra, rf):
        h_ref[:, cols] = (x_ref[:, cols]
                          + ra * jnp.dot(a_ref[...], w_scr[t, :ATTN_WIDTH, :], preferred_element_type=F32)
                          + rf * jnp.dot(f_ref[...], w_scr[t, ATTN_WIDTH:, :], preferred_element_type=F32))

    def finish():
        u_ref[...] = (_rms(h_ref[...]) * g2_ref[...]).astype(u_ref.dtype)

    @pl.when(s < n_tiles)
    def _():
        w_scr[s] = (w_ref[...] * mg_ref[...]).astype(BF16)
        column_tile(s, pl.ds(pl.multiple_of(s * OUT_TN, OUT_TN), OUT_TN), inv_rms(a_ref), inv_rms(f_ref))

    @pl.when(s == n_tiles - 1)
    def _():
        finish()

    @pl.when(s >= n_tiles)
    def _():
        ra, rf = inv_rms(a_ref), inv_rms(f_ref)
        for t in range(n_tiles):
            column_tile(t, slice(t * OUT_TN, (t + 1) * OUT_TN), ra, rf)
        finish()


def _out_proj(a, f, x2, ag, fg, w, g2):
    t, d = x2.shape
    n_tiles = d // OUT_TN
    row_block = lambda s: (jnp.maximum(s - (n_tiles - 1), 0), 0)
    row = lambda width: pl.BlockSpec((OUT_TM, width), row_block)
    mix_gain = jnp.concatenate([ag, fg], axis=1).reshape(w.shape[0], 1)
    return pl.pallas_call(
        functools.partial(_out_proj_kernel, n_tiles=n_tiles),
        grid=(n_tiles + t // OUT_TM - 1,),
        in_specs=[
            row(ATTN_WIDTH), row(FOURIER_WIDTH), row(d),
            pl.BlockSpec((w.shape[0], 1), lambda s: (0, 0)),
            pl.BlockSpec((w.shape[0], OUT_TN), lambda s: (0, jnp.minimum(s, n_tiles - 1))),
            pl.BlockSpec((1, d), lambda s: (0, 0)),
        ],
        out_specs=[row(d), row(d)],
        out_shape=[jax.ShapeDtypeStruct((t, d), F32), jax.ShapeDtypeStruct((t, d), BF16)],
        scratch_shapes=[pltpu.VMEM((n_tiles, w.shape[0], OUT_TN), BF16)],
        compiler_params=_params("arbitrary"),
        name="out_proj",
    )(a, f, x2, mix_gain, w, g2)


def _ffn_kernel(u_hbm, wg_ref, wv_ref, cwg_ref, cwv_ref, cbg_ref, cbv_ref, wd_ref, h_ref, g_ref,
                y_hbm, acc_ref, gate_scr, val_scr, u_ref, y_buf, u_sem, y_sem):
    b, j = pl.program_id(0), pl.program_id(1)
    n_seq, n_ff = pl.num_programs(0), pl.num_programs(1)
    seq = u_ref.shape[0]
    n_out = seq // FFN_OUT_ROWS
    assert sum(FFN_ROW_SPLIT) == seq
    starts = [sum(FFN_ROW_SPLIT[:c]) for c in range(len(FFN_ROW_SPLIT))]

    def u_copy(seq_idx):
        return pltpu.make_async_copy(u_hbm.at[pl.ds(seq_idx * seq, seq), :], u_ref, u_sem)

    def y_copy(piece, slot):
        dst = y_hbm.at[pl.ds(b * seq + piece * FFN_Y_ROWS, FFN_Y_ROWS), :]
        return pltpu.make_async_copy(y_buf.at[slot], dst, y_sem.at[slot])

    @pl.when(j == 0)
    def _():
        @pl.when(b == 0)
        def _():
            acc_ref[...] = jnp.zeros_like(acc_ref)
            gate_scr[...] = jnp.zeros_like(gate_scr)
            val_scr[...] = jnp.zeros_like(val_scr)
            u_copy(0).start()

        u_copy(b).wait()

    wg, wv, wd = wg_ref[...], wv_ref[...], wd_ref[...]
    cwg, cwv, cbg, cbv = cwg_ref[...], cwv_ref[...], cbg_ref[...], cbv_ref[...]

    def conv3(scr, cw, cb, r0, rc):
        prev = scr[pl.ds(FFN_PAD - 1 + r0, rc), :]
        cur = scr[pl.ds(FFN_PAD + r0, rc), :]
        nxt = scr[pl.ds(FFN_PAD + 1 + r0, rc), :]
        return prev * cw[0:1, :] + cur * cw[1:2, :] + nxt * cw[2:3, :] + cb

    def gate_and_down(c):
        r0, rc = starts[c], FFN_ROW_SPLIT[c]
        gate = conv3(gate_scr, cwg, cbg, r0, rc)
        val = conv3(val_scr, cwv, cbv, r0, rc)
        act = (gate * jax.nn.sigmoid(gate) * val).astype(BF16)
        rows = pl.ds(r0, rc)
        acc_ref[rows, :] = (jnp.where(j == 0, 0.0, acc_ref[rows, :])
                            + jnp.dot(act, wd, preferred_element_type=F32))

    for c, (r0, rc) in enumerate(zip(starts, FFN_ROW_SPLIT)):
        u = u_ref[pl.ds(r0, rc), :]
        gate_scr[pl.ds(FFN_PAD + r0, rc), :] = jnp.dot(u, wg, preferred_element_type=F32)
        val_scr[pl.ds(FFN_PAD + r0, rc), :] = jnp.dot(u, wv, preferred_element_type=F32)
        if c >= 1:
            gate_and_down(c - 1)
    gate_and_down(len(FFN_ROW_SPLIT) - 1)

    rows = pl.ds(pl.multiple_of(jnp.minimum(j, n_out - 1) * FFN_OUT_ROWS, FFN_OUT_ROWS), FFN_OUT_ROWS)
    acc_ref[rows, :] += jnp.where(j < n_out, h_ref[...], 0.0)

    @pl.when(j == n_ff - 1)
    def _():
        @pl.when(b + 1 < n_seq)
        def _():
            u_copy(b + 1).start()

        n_pieces = seq // FFN_Y_ROWS
        for piece in range(n_pieces):
            slot = piece % 2
            if piece >= 2:
                y_copy(piece - 2, slot).wait()
            rows = slice(piece * FFN_Y_ROWS, (piece + 1) * FFN_Y_ROWS)
            y_buf[slot] = _rms(acc_ref[rows, :]) * g_ref[...]
            y_copy(piece, slot).start()
        for piece in range(max(n_pieces - 2, 0), n_pieces):
            y_copy(piece, piece % 2).wait()


def _ffn(u2, w_up, conv_w, conv_b, w_down, h1, g, batch, seq):
    t, d = u2.shape
    d_ff = w_down.shape[0]
    n_ff = d_ff // FFN_TN
    n_out = seq // FFN_OUT_ROWS
    taps = conv_w.shape[0]
    gate_col = lambda b, j: (0, j)
    val_col = lambda b, j: (0, n_ff + j)
    res_row = lambda b, j: (b * n_out + jnp.minimum(j, n_out - 1), 0)
    assert n_out <= n_ff
    return pl.pallas_call(
        _ffn_kernel,
        grid=(batch, n_ff),
        in_specs=[
            pl.BlockSpec(memory_space=pl.ANY),
            pl.BlockSpec((d, FFN_TN), gate_col),
            pl.BlockSpec((d, FFN_TN), val_col),
            pl.BlockSpec((taps, FFN_TN), gate_col),
            pl.BlockSpec((taps, FFN_TN), val_col),
            pl.BlockSpec((1, FFN_TN), gate_col),
            pl.BlockSpec((1, FFN_TN), val_col),
            pl.BlockSpec((FFN_TN, d), lambda b, j: (j, 0)),
            pl.BlockSpec((FFN_OUT_ROWS, d), res_row),
            pl.BlockSpec((1, d), lambda b, j: (0, 0)),
        ],
        out_specs=pl.BlockSpec(memory_space=pl.ANY),
        out_shape=jax.ShapeDtypeStruct((t, d), F32),
        scratch_shapes=[
            pltpu.VMEM((seq, d), F32),
            pltpu.VMEM((seq + 2 * FFN_PAD, FFN_TN), F32),
            pltpu.VMEM((seq + 2 * FFN_PAD, FFN_TN), F32),
            pltpu.VMEM((seq, d), BF16),
            pltpu.VMEM((2, FFN_Y_ROWS, d), F32),
            pltpu.SemaphoreType.DMA(()),
            pltpu.SemaphoreType.DMA((2,)),
        ],
        compiler_params=pltpu.CompilerParams(dimension_semantics=("arbitrary", "arbitrary"),
                                             vmem_limit_bytes=FFN_VMEM_LIMIT_BYTES),
        name="ffn",
    )(u2, w_up, w_up, conv_w, conv_w, conv_b, conv_b, w_down, h1, g)


def _rope_tables(seq):
    t = np.arange(seq)
    inv_freq = ROPE_THETA ** (-np.arange(ROPE_PAIRS, dtype=np.float64) / ROPE_PAIRS)

    def axis_tables(pos):
        ang = pos[:, None] * inv_freq[None, :]
        ang = np.concatenate([ang, ang], axis=-1)
        return np.cos(ang), np.sin(ang)

    cos_r, sin_r = axis_tables(t // GRID_W)
    cos_c, sin_c = axis_tables(t % GRID_W)
    cos = np.concatenate([cos_r, cos_c], axis=-1)
    sin = np.concatenate([sin_r, sin_c], axis=-1)
    first_half = (np.arange(HEAD_DIM) % (2 * ROPE_PAIRS)) < ROPE_PAIRS
    sin_lo = np.where(first_half[None, :], -sin, 0.0)
    sin_hi = np.where(first_half[None, :], 0.0, sin)
    return tuple(jnp.asarray(tab, F32) for tab in (cos, sin_lo, sin_hi))


def _dft_cos_sin(rows, cols, n, scale=1.0):
    ang = ((rows[:, None] * cols[None, :]) % n) * (2.0 * math.pi / n)
    return jnp.asarray(np.cos(ang) * scale, F32), jnp.asarray(np.sin(ang) * scale, F32)


def _dft_tables(n, scale):
    return _dft_cos_sin(np.arange(n), np.arange(n), n, scale)


def kernel(x, norm1_g, w_in, q_norm_g, k_norm_g, w_fmix, attn_out_g, fourier_out_g, w_out,
           norm2_g, w_up, conv_w, conv_b, w_down, final_g):
    batch, seq, d = x.shape
    depth = w_in.shape[0]
    rope = _rope_tables(seq)
    cc, sc = _dft_tables(FOURIER_GROUP_DIM, 1.0 / math.sqrt(seq * FOURIER_GROUP_DIM))

    assert depth == 1
    h = x.reshape(batch * seq, d)
    for l in range(depth):
        proj = _in_proj(h, norm1_g[l][None], w_in[l])
        attn, w_up_bf16, w_down_bf16 = _attention(proj, rope, q_norm_g[l][None], k_norm_g[l][None],
                                                  w_up[l], w_down[l], batch, seq)
        ab = _fourier_fold(w_fmix[l], cc, sc)
        four = _fourier(proj, ab, batch, seq)
        h1, u2 = _out_proj(attn, four, h, attn_out_g[l][None], fourier_out_g[l][None],
                           w_out[l], norm2_g[l][None])
        h = _ffn(u2, w_up_bf16, conv_w[l], conv_b[l][None], w_down_bf16, h1, final_g[None], batch, seq)
    return h.reshape(batch, seq, d)
```

```python
import functools
import math

import jax
import jax.numpy as jnp
import numpy as np
from jax import lax
from jax.experimental import pallas as pl
from jax.experimental.pallas import tpu as pltpu

F32 = jnp.float32
BF16 = jnp.bfloat16

HEAD_DIM = 128
N_Q_HEADS = 8
N_KV_HEADS = 2
Q_PER_KV = N_Q_HEADS // N_KV_HEADS
ATTN_WIDTH = N_Q_HEADS * HEAD_DIM
KV_WIDTH = N_KV_HEADS * HEAD_DIM
FOURIER_GROUPS = 8
FOURIER_GROUP_DIM = 128
FOURIER_WIDTH = FOURIER_GROUPS * FOURIER_GROUP_DIM
GRID_W = 64
ROPE_THETA = 10000.0
ROPE_PAIRS = HEAD_DIM // 4
EPS = 1e-6

VMEM_LIMIT_BYTES = 56 * 1024 * 1024

IN_TM, IN_TN, IN_ROWS = 1024, 512, 512
ATTN_TQ, ATTN_ROWS = 1024, 256
BF16_SUBLANES = 16
FOURIER_TM = 512
DFT_SUB = 64
OUT_TM, OUT_TN = 512, 512
FFN_TN = 512
FFN_ROW_SPLIT = (1280, 768)
FFN_PAD = 8
FFN_OUT_ROWS = 256
FFN_Y_ROWS = 128
FFN_VMEM_LIMIT_BYTES = 60 * 1024 * 1024


def _params(*sem):
    return pltpu.CompilerParams(dimension_semantics=sem, vmem_limit_bytes=VMEM_LIMIT_BYTES)


def _rms(x):
    return x * lax.rsqrt(jnp.mean(x * x, axis=-1, keepdims=True) + EPS)


def _in_proj_kernel(x_ref, g_ref, w_ref, o_ref, u_scr, w_scr, *, n_tiles):
    s = pl.program_id(0)

    def normed(rows):
        return (_rms(x_ref[rows, :]) * g_ref[...]).astype(BF16)

    @pl.when(s == 0)
    def _():
        u_scr[...] = normed(slice(None))

    @pl.when(s < n_tiles)
    def _():
        w_scr[s] = w_ref[...].astype(BF16)
        cols = pl.ds(pl.multiple_of(s * IN_TN, IN_TN), IN_TN)
        o_ref[:, cols] = jnp.dot(u_scr[...], w_scr[s], preferred_element_type=F32).astype(o_ref.dtype)

    @pl.when(s >= n_tiles)
    def _():
        for c in range(IN_TM // IN_ROWS):
            rows = slice(c * IN_ROWS, (c + 1) * IN_ROWS)
            u = normed(rows)
            for t in range(n_tiles):
                o_ref[rows, t * IN_TN:(t + 1) * IN_TN] = jnp.dot(
                    u, w_scr[t], preferred_element_type=F32).astype(o_ref.dtype)


def _in_proj(x2, g, w):
    t, d = x2.shape
    n = w.shape[1]
    n_tiles = n // IN_TN
    row_block = lambda s: (jnp.maximum(s - (n_tiles - 1), 0), 0)
    return pl.pallas_call(
        functools.partial(_in_proj_kernel, n_tiles=n_tiles),
        grid=(n_tiles + t // IN_TM - 1,),
        in_specs=[
            pl.BlockSpec((IN_TM, d), row_block),
            pl.BlockSpec((1, d), lambda s: (0, 0)),
            pl.BlockSpec((d, IN_TN), lambda s: (0, jnp.minimum(s, n_tiles - 1))),
        ],
        out_specs=pl.BlockSpec((IN_TM, n), row_block),
        out_shape=jax.ShapeDtypeStruct((t, n), BF16),
        scratch_shapes=[pltpu.VMEM((IN_TM, d), BF16), pltpu.VMEM((n_tiles, d, IN_TN), BF16)],
        compiler_params=_params("arbitrary"),
        name="in_proj",
    )(x2, g, w)


def _rope(x, cos, sin_lo, sin_hi):
    return (x * cos
            + pltpu.roll(x, HEAD_DIM - ROPE_PAIRS, axis=1) * sin_lo
            + pltpu.roll(x, ROPE_PAIRS, axis=1) * sin_hi)


def _attn_kernel(q_ref, k_ref, v_ref, cq_ref, slq_ref, shq_ref, ck_ref, slk_ref, shk_ref,
                 qg_ref, kg_ref, wu_ref, wd_ref, o_ref, wu_out, wd_out, k_scr, v_scr):
    @pl.when(pl.program_id(2) == 0)
    def _():
        k = _rms(k_ref[...].astype(F32)) * kg_ref[...]
        k_scr[...] = _rope(k, ck_ref[...], slk_ref[...], shk_ref[...]).astype(BF16)
        v_scr[:, :HEAD_DIM] = v_ref[...]
        v_scr[:, HEAD_DIM:] = jnp.ones((v_ref.shape[0], HEAD_DIM), BF16)

    scale = math.log2(math.e) / math.sqrt(HEAD_DIM)
    cos, sin_lo, sin_hi = cq_ref[...], slq_ref[...], shq_ref[...]
    kk = k_scr[...]
    vv = v_scr[...]
    for r in range(ATTN_TQ // ATTN_ROWS):
        rows = slice(r * ATTN_ROWS, (r + 1) * ATTN_ROWS)
        for h in range(Q_PER_KV):
            sl = slice(h * HEAD_DIM, (h + 1) * HEAD_DIM)
            q = _rms(q_ref[rows, sl].astype(F32)) * qg_ref[...]
            q = (_rope(q, cos[rows], sin_lo[rows], sin_hi[rows]) * scale).astype(BF16)
            s = lax.dot_general(q, kk, (((1,), (1,)), ((), ())), preferred_element_type=F32)
            p = jnp.exp2(s - jnp.max(s, axis=-1, keepdims=True))
            o = jnp.dot(p.astype(BF16), vv, preferred_element_type=F32)
            o_ref[rows, sl] = (o[:, :HEAD_DIM] / o[:, HEAD_DIM:]).astype(o_ref.dtype)

    wu_out[...] = wu_ref[...].astype(BF16)
    wd_out[...] = wd_ref[...].astype(BF16)


def _attention(proj, tabs, qg, kg, w_up, w_down, batch, seq):
    t = proj.shape[0]
    nq = seq // ATTN_TQ
    gw = Q_PER_KV * HEAD_DIM
    k_blk0 = ATTN_WIDTH // HEAD_DIM
    v_blk0 = (ATTN_WIDTH + KV_WIDTH) // HEAD_DIM
    cos, sin_lo, sin_hi = tabs
    q_tab = pl.BlockSpec((ATTN_TQ, HEAD_DIM), lambda b, g, i: (i, 0))
    k_tab = pl.BlockSpec((seq, HEAD_DIM), lambda b, g, i: (0, 0))
    gain = pl.BlockSpec((1, HEAD_DIM), lambda b, g, i: (0, 0))

    n_steps = batch * N_KV_HEADS * nq
    step = lambda b, g, i: (b * N_KV_HEADS + g) * nq + i
    up_rows, down_rows = w_up.shape[0] // n_steps, w_down.shape[0] // n_steps
    assert up_rows * n_steps == w_up.shape[0] and up_rows % BF16_SUBLANES == 0
    assert down_rows * n_steps == w_down.shape[0] and down_rows % BF16_SUBLANES == 0
    up_blk = pl.BlockSpec((up_rows, w_up.shape[1]), lambda b, g, i: (step(b, g, i), 0))
    down_blk = pl.BlockSpec((down_rows, w_down.shape[1]), lambda b, g, i: (step(b, g, i), 0))
    return pl.pallas_call(
        _attn_kernel,
        grid=(batch, N_KV_HEADS, nq),
        in_specs=[
            pl.BlockSpec((ATTN_TQ, gw), lambda b, g, i: (b * nq + i, g)),
            pl.BlockSpec((seq, HEAD_DIM), lambda b, g, i: (b, k_blk0 + g)),
            pl.BlockSpec((seq, HEAD_DIM), lambda b, g, i: (b, v_blk0 + g)),
            q_tab, q_tab, q_tab, k_tab, k_tab, k_tab, gain, gain, up_blk, down_blk,
        ],
        out_specs=[pl.BlockSpec((ATTN_TQ, gw), lambda b, g, i: (b * nq + i, g)), up_blk, down_blk],
        out_shape=[jax.ShapeDtypeStruct((t, ATTN_WIDTH), BF16),
                   jax.ShapeDtypeStruct(w_up.shape, BF16), jax.ShapeDtypeStruct(w_down.shape, BF16)],
        scratch_shapes=[pltpu.VMEM((seq, HEAD_DIM), BF16), pltpu.VMEM((seq, 2 * HEAD_DIM), BF16)],
        compiler_params=_params("arbitrary", "arbitrary", "arbitrary"),
        name="attention",
    )(proj, proj, proj, cos, sin_lo, sin_hi, cos, sin_lo, sin_hi, qg, kg, w_up, w_down)


def _fourier_kernel(flo_ref, fhi_ref, w_ref, cc_ref, sc_ref, ca_ref, sa_ref, cr_ref, sr_ref, o_ref,
                    ab_ref, za_scr, zb_scr, z_stage, c_scr, ms_scr):
    m = pl.program_id(1)
    half = za_scr.shape[0] // 2
    rows = pl.ds(pl.multiple_of(m * FOURIER_TM, FOURIER_TM), FOURIER_TM)

    @pl.when((pl.program_id(0) == 0) & (m == 0))
    def _():
        for g in range(FOURIER_GROUPS):
            w = w_ref[g]
            a = jnp.dot(cc_ref[...], w, preferred_element_type=F32, precision=lax.Precision.HIGHEST)
            b = jnp.dot(sc_ref[...], w, preferred_element_type=F32, precision=lax.Precision.HIGHEST)
            ab_ref[g, :, :FOURIER_GROUP_DIM] = a.astype(BF16)
            ab_ref[g, :, FOURIER_GROUP_DIM:] = b.astype(BF16)

    @pl.when(pl.program_id(0) == 0)
    def _():
        cr, sr = cr_ref[...], sr_ref[...]
        per_block = FOURIER_TM // DFT_SUB
        for i in range(per_block):
            ca = ca_ref[pl.ds(m * per_block + i, 1), :]
            sa = sa_ref[pl.ds(m * per_block + i, 1), :]
            sub = pl.ds(pl.multiple_of(m * FOURIER_TM + i * DFT_SUB, DFT_SUB), DFT_SUB)
            c_scr[sub, :] = (ca * cr - sa * sr).astype(BF16)
            ms_scr[sub, :] = (-(sa * cr + ca * sr)).astype(BF16)

    @pl.when(m == 0)
    def _():
        groups_per_ref = FOURIER_GROUPS // 2
        for g in range(FOURIER_GROUPS):
            src = flo_ref if g < groups_per_ref else fhi_ref
            lo = (g % groups_per_ref) * FOURIER_GROUP_DIM
            z = jnp.dot(src[:, lo:lo + FOURIER_GROUP_DIM], ab_ref[g], preferred_element_type=F32)
            z_stage[0] = z[:, :FOURIER_GROUP_DIM]
            z_stage[1] = z[:, FOURIER_GROUP_DIM:]
            dst = slice(g * FOURIER_GROUP_DIM, (g + 1) * FOURIER_GROUP_DIM)
            for parity in range(2):
                part = slice(parity * half, (parity + 1) * half)
                za_scr[part, dst] = z_stage[0, pl.ds(parity, half, stride=2), :].astype(BF16)
                zb_scr[part, dst] = z_stage[1, pl.ds(parity, half, stride=2), :].astype(BF16)

    def half_sum(cols):
        return (jnp.dot(c_scr[rows, cols], za_scr[cols, :], preferred_element_type=F32)
                + jnp.dot(ms_scr[rows, cols], zb_scr[cols, :], preferred_element_type=F32))

    even, odd = half_sum(slice(0, half)), half_sum(slice(half, 2 * half))
    o_ref[0] = (even + odd).astype(o_ref.dtype)
    o_ref[1] = (even - odd).astype(o_ref.dtype)


def _fourier(proj, w_fmix, batch, seq):
    t = proj.shape[0]
    cc, sc = _dft_tables(FOURIER_GROUP_DIM, 1.0 / math.sqrt(seq * FOURIER_GROUP_DIM))
    half = seq // 2
    nm = half // FOURIER_TM
    half_w = FOURIER_WIDTH // 2
    f_blk0 = (ATTN_WIDTH + 2 * KV_WIDTH) // half_w
    cols = np.concatenate([np.arange(0, seq, 2), np.arange(1, seq, 2)])
    ca, sa = _dft_cos_sin(np.arange(half // DFT_SUB) * DFT_SUB, cols, seq)
    cr, sr = _dft_cos_sin(np.arange(DFT_SUB), cols, seq)
    whole = lambda arr: pl.BlockSpec(arr.shape, lambda b, m: (0,) * arr.ndim)
    out = pl.pallas_call(
        _fourier_kernel,
        grid=(batch, nm),
        in_specs=[
            pl.BlockSpec((seq, half_w), lambda b, m: (b, f_blk0)),
            pl.BlockSpec((seq, half_w), lambda b, m: (b, f_blk0 + 1)),
            whole(w_fmix), whole(cc), whole(sc), whole(ca), whole(sa), whole(cr), whole(sr),
        ],
        out_specs=pl.BlockSpec((None, 2, FOURIER_TM, FOURIER_WIDTH), lambda b, m: (b, 0, m, 0)),
        out_shape=jax.ShapeDtypeStruct((batch, 2, half, FOURIER_WIDTH), BF16),
        scratch_shapes=[pltpu.VMEM((FOURIER_GROUPS, FOURIER_GROUP_DIM, 2 * FOURIER_GROUP_DIM), BF16),
                        pltpu.VMEM((seq, FOURIER_WIDTH), BF16), pltpu.VMEM((seq, FOURIER_WIDTH), BF16),
                        pltpu.VMEM((2, seq, FOURIER_GROUP_DIM), F32),
                        pltpu.VMEM((half, seq), BF16), pltpu.VMEM((half, seq), BF16)],
        compiler_params=_params("arbitrary", "arbitrary"),
        name="fourier",
    )(proj, proj, w_fmix, cc, sc, ca, sa, cr, sr)
    return out.reshape(t, FOURIER_WIDTH)


def _out_proj_kernel(a_ref, f_ref, x_ref, mg_ref, w_ref, g2_ref, h_ref, u_ref, w_scr, *, n_tiles):
    s = pl.program_id(0)

    def inv_rms(ref):
        v = ref[...].astype(F32)
        return lax.rsqrt(jnp.mean(v * v, axis=-1, keepdims=True) + EPS)

    def column_tile(t, cols, ra, rf):
        h_ref[:, cols] = (x_ref[:, cols]
                          + ra * jnp.dot(a_ref[...], w_scr[t, :ATTN_WIDTH, :], preferred_element_type=F32)
                          + rf * jnp.dot(f_ref[...], w_scr[t, ATTN_WIDTH:, :], preferred_element_type=F32))

    def finish():
        u_ref[...] = (_rms(h_ref[...]) * g2_ref[...]).astype(u_ref.dtype)

    @pl.when(s < n_tiles)
    def _():
        w_scr[s] = (w_ref[...] * mg_ref[...]).astype(BF16)
        column_tile(s, pl.ds(pl.multiple_of(s * OUT_TN, OUT_TN), OUT_TN), inv_rms(a_ref), inv_rms(f_ref))

    @pl.when(s == n_tiles - 1)
    def _():
        finish()

    @pl.when(s >= n_tiles)
    def _():
        ra, rf = inv_rms(a_ref), inv_rms(f_ref)
        for t in range(n_tiles):
            column_tile(t, slice(t * OUT_TN, (t + 1) * OUT_TN), ra, rf)
        finish()


def _out_proj(a, f, x2, ag, fg, w, g2):
    t, d = x2.shape
    n_tiles = d // OUT_TN
    row_block = lambda s: (jnp.maximum(s - (n_tiles - 1), 0), 0)
    row = lambda width: pl.BlockSpec((OUT_TM, width), row_block)
    mix_gain = jnp.concatenate([ag, fg], axis=1).reshape(w.shape[0], 1)
    return pl.pallas_call(
        functools.partial(_out_proj_kernel, n_tiles=n_tiles),
        grid=(n_tiles + t // OUT_TM - 1,),
        in_specs=[
            row(ATTN_WIDTH), row(FOURIER_WIDTH), row(d),
            pl.BlockSpec((w.shape[0], 1), lambda s: (0, 0)),
            pl.BlockSpec((w.shape[0], OUT_TN), lambda s: (0, jnp.minimum(s, n_tiles - 1))),
            pl.BlockSpec((1, d), lambda s: (0, 0)),
        ],
        out_specs=[row(d), row(d)],
        out_shape=[jax.ShapeDtypeStruct((t, d), F32), jax.ShapeDtypeStruct((t, d), BF16)],
        scratch_shapes=[pltpu.VMEM((n_tiles, w.shape[0], OUT_TN), BF16)],
        compiler_params=_params("arbitrary"),
        name="out_proj",
    )(a, f, x2, mix_gain, w, g2)


def _ffn_kernel(u_hbm, wg_ref, wv_ref, cwg_ref, cwv_ref, cbg_ref, cbv_ref, wd_ref, h_ref, g_ref,
                y_hbm, acc_ref, gate_scr, val_scr, u_ref, y_buf, u_sem, y_sem):
    b, j = pl.program_id(0), pl.program_id(1)
    n_seq, n_ff = pl.num_programs(0), pl.num_programs(1)
    seq = u_ref.shape[0]
    n_out = seq // FFN_OUT_ROWS
    assert sum(FFN_ROW_SPLIT) == seq
    starts = [sum(FFN_ROW_SPLIT[:c]) for c in range(len(FFN_ROW_SPLIT))]

    def u_copy(seq_idx):
        return pltpu.make_async_copy(u_hbm.at[pl.ds(seq_idx * seq, seq), :], u_ref, u_sem)

    def y_copy(piece, slot):
        dst = y_hbm.at[pl.ds(b * seq + piece * FFN_Y_ROWS, FFN_Y_ROWS), :]
        return pltpu.make_async_copy(y_buf.at[slot], dst, y_sem.at[slot])

    @pl.when(j == 0)
    def _():
        @pl.when(b == 0)
        def _():
            acc_ref[...] = jnp.zeros_like(acc_ref)
            gate_scr[...] = jnp.zeros_like(gate_scr)
            val_scr[...] = jnp.zeros_like(val_scr)
            u_copy(0).start()

        u_copy(b).wait()

    wg, wv, wd = wg_ref[...], wv_ref[...], wd_ref[...]
    cwg, cwv, cbg, cbv = cwg_ref[...], cwv_ref[...], cbg_ref[...], cbv_ref[...]

    def conv3(scr, cw, cb, r0, rc):
        prev = scr[pl.ds(FFN_PAD - 1 + r0, rc), :]
        cur = scr[pl.ds(FFN_PAD + r0, rc), :]
        nxt = scr[pl.ds(FFN_PAD + 1 + r0, rc), :]
        return prev * cw[0:1, :] + cur * cw[1:2, :] + nxt * cw[2:3, :] + cb

    def gate_and_down(c):
        r0, rc = starts[c], FFN_ROW_SPLIT[c]
        gate = conv3(gate_scr, cwg, cbg, r0, rc)
        val = conv3(val_scr, cwv, cbv, r0, rc)
        act = (gate * jax.nn.sigmoid(gate) * val).astype(BF16)
        rows = pl.ds(r0, rc)
        acc_ref[rows, :] = (jnp.where(j == 0, 0.0, acc_ref[rows, :])
                            + jnp.dot(act, wd, preferred_element_type=F32))

    for c, (r0, rc) in enumerate(zip(starts, FFN_ROW_SPLIT)):
        u = u_ref[pl.ds(r0, rc), :]
        gate_scr[pl.ds(FFN_PAD + r0, rc), :] = jnp.dot(u, wg, preferred_element_type=F32)
        val_scr[pl.ds(FFN_PAD + r0, rc), :] = jnp.dot(u, wv, preferred_element_type=F32)
        if c >= 1:
            gate_and_down(c - 1)
    gate_and_down(len(FFN_ROW_SPLIT) - 1)

    rows = pl.ds(pl.multiple_of(jnp.minimum(j, n_out - 1) * FFN_OUT_ROWS, FFN_OUT_ROWS), FFN_OUT_ROWS)
    acc_ref[rows, :] += jnp.where(j < n_out, h_ref[...], 0.0)

    @pl.when(j == n_ff - 1)
    def _():
        @pl.when(b + 1 < n_seq)
        def _():
            u_copy(b + 1).start()

        n_pieces = seq // FFN_Y_ROWS
        for piece in range(n_pieces):
            slot = piece % 2
            if piece >= 2:
                y_copy(piece - 2, slot).wait()
            rows = slice(piece * FFN_Y_ROWS, (piece + 1) * FFN_Y_ROWS)
            y_buf[slot] = _rms(acc_ref[rows, :]) * g_ref[...]
            y_copy(piece, slot).start()
        for piece in range(max(n_pieces - 2, 0), n_pieces):
            y_copy(piece, piece % 2).wait()


def _ffn(u2, w_up, conv_w, conv_b, w_down, h1, g, batch, seq):
    t, d = u2.shape
    d_ff = w_down.shape[0]
    n_ff = d_ff // FFN_TN
    n_out = seq // FFN_OUT_ROWS
    taps = conv_w.shape[0]
    gate_col = lambda b, j: (0, j)
    val_col = lambda b, j: (0, n_ff + j)
    res_row = lambda b, j: (b * n_out + jnp.minimum(j, n_out - 1), 0)
    assert n_out <= n_ff
    return pl.pallas_call(
        _ffn_kernel,
        grid=(batch, n_ff),
        in_specs=[
            pl.BlockSpec(memory_space=pl.ANY),
            pl.BlockSpec((d, FFN_TN), gate_col),
            pl.BlockSpec((d, FFN_TN), val_col),
            pl.BlockSpec((taps, FFN_TN), gate_col),
            pl.BlockSpec((taps, FFN_TN), val_col),
            pl.BlockSpec((1, FFN_TN), gate_col),
            pl.BlockSpec((1, FFN_TN), val_col),
            pl.BlockSpec((FFN_TN, d), lambda b, j: (j, 0)),
            pl.BlockSpec((FFN_OUT_ROWS, d), res_row),
            pl.BlockSpec((1, d), lambda b, j: (0, 0)),
        ],
        out_specs=pl.BlockSpec(memory_space=pl.ANY),
        out_shape=jax.ShapeDtypeStruct((t, d), F32),
        scratch_shapes=[
            pltpu.VMEM((seq, d), F32),
            pltpu.VMEM((seq + 2 * FFN_PAD, FFN_TN), F32),
            pltpu.VMEM((seq + 2 * FFN_PAD, FFN_TN), F32),
            pltpu.VMEM((seq, d), BF16),
            pltpu.VMEM((2, FFN_Y_ROWS, d), F32),
            pltpu.SemaphoreType.DMA(()),
            pltpu.SemaphoreType.DMA((2,)),
        ],
        compiler_params=pltpu.CompilerParams(dimension_semantics=("arbitrary", "arbitrary"),
                                             vmem_limit_bytes=FFN_VMEM_LIMIT_BYTES),
        name="ffn",
    )(u2, w_up, w_up, conv_w, conv_w, conv_b, conv_b, w_down, h1, g)


def _rope_tables(seq):
    t = np.arange(seq)
    inv_freq = ROPE_THETA ** (-np.arange(ROPE_PAIRS, dtype=np.float64) / ROPE_PAIRS)

    def axis_tables(pos):
        ang = pos[:, None] * inv_freq[None, :]
        ang = np.concatenate([ang, ang], axis=-1)
        return np.cos(ang), np.sin(ang)

    cos_r, sin_r = axis_tables(t // GRID_W)
    cos_c, sin_c = axis_tables(t % GRID_W)
    cos = np.concatenate([cos_r, cos_c], axis=-1)
    sin = np.concatenate([sin_r, sin_c], axis=-1)
    first_half = (np.arange(HEAD_DIM) % (2 * ROPE_PAIRS)) < ROPE_PAIRS
    sin_lo = np.where(first_half[None, :], -sin, 0.0)
    sin_hi = np.where(first_half[None, :], 0.0, sin)
    return tuple(jnp.asarray(tab, F32) for tab in (cos, sin_lo, sin_hi))


def _dft_cos_sin(rows, cols, n, scale=1.0):
    ang = ((rows[:, None] * cols[None, :]) % n) * (2.0 * math.pi / n)
    return jnp.asarray(np.cos(ang) * scale, F32), jnp.asarray(np.sin(ang) * scale, F32)


def _dft_tables(n, scale):
    return _dft_cos_sin(np.arange(n), np.arange(n), n, scale)


def kernel(x, norm1_g, w_in, q_norm_g, k_norm_g, w_fmix, attn_out_g, fourier_out_g, w_out,
           norm2_g, w_up, conv_w, conv_b, w_down, final_g):
    batch, seq, d = x.shape
    depth = w_in.shape[0]
    rope = _rope_tables(seq)

    assert depth == 1
    h = x.reshape(batch * seq, d)
    for l in range(depth):
        proj = _in_proj(h, norm1_g[l][None], w_in[l])
        attn, w_up_bf16, w_down_bf16 = _attention(proj, rope, q_norm_g[l][None], k_norm_g[l][None],
                                                  w_up[l], w_down[l], batch, seq)
        four = _fourier(proj, w_fmix[l], batch, seq)
        h1, u2 = _out_proj(attn, four, h, attn_out_g[l][None], fourier_out_g[l][None],
                           w_out[l], norm2_g[l][None])
        h = _ffn(u2, w_up_bf16, conv_w[l], conv_b[l][None], w_down_bf16, h1, final_g[None], batch, seq)
    return h.reshape(batch, seq, d)
```

```python
import functools
import math

import jax
import jax.numpy as jnp
import numpy as np
from jax import lax
from jax.experimental import pallas as pl
from jax.experimental.pallas import tpu as pltpu

F32 = jnp.float32
BF16 = jnp.bfloat16

HEAD_DIM = 128
N_Q_HEADS = 8
N_KV_HEADS = 2
Q_PER_KV = N_Q_HEADS // N_KV_HEADS
ATTN_WIDTH = N_Q_HEADS * HEAD_DIM
KV_WIDTH = N_KV_HEADS * HEAD_DIM
FOURIER_GROUPS = 8
FOURIER_GROUP_DIM = 128
FOURIER_WIDTH = FOURIER_GROUPS * FOURIER_GROUP_DIM
GRID_W = 64
ROPE_THETA = 10000.0
ROPE_PAIRS = HEAD_DIM // 4
EPS = 1e-6

VMEM_LIMIT_BYTES = 56 * 1024 * 1024

IN_TM, IN_TN, IN_ROWS = 1024, 512, 512
ATTN_TQ, ATTN_ROWS = 1024, 256
BF16_SUBLANES = 16
FOURIER_TM = 512
DFT_SUB = 64
OUT_TM, OUT_TN = 512, 512
FFN_TN = 512
FFN_ROW_SPLIT = (1280, 768)
FFN_PAD = 8
FFN_OUT_ROWS = 256
FFN_Y_ROWS = 256
FFN_VMEM_LIMIT_BYTES = 60 * 1024 * 1024


def _params(*sem):
    return pltpu.CompilerParams(dimension_semantics=sem, vmem_limit_bytes=VMEM_LIMIT_BYTES)


def _rms(x):
    return x * lax.rsqrt(jnp.mean(x * x, axis=-1, keepdims=True) + EPS)


def _in_proj_kernel(x_ref, g_ref, w_ref, o_ref, u_scr, w_scr, *, n_tiles):
    s = pl.program_id(0)

    def normed(rows):
        return (_rms(x_ref[rows, :]) * g_ref[...]).astype(BF16)

    @pl.when(s == 0)
    def _():
        u_scr[...] = normed(slice(None))

    @pl.when(s < n_tiles)
    def _():
        w_scr[s] = w_ref[...].astype(BF16)
        cols = pl.ds(pl.multiple_of(s * IN_TN, IN_TN), IN_TN)
        o_ref[:, cols] = jnp.dot(u_scr[...], w_scr[s], preferred_element_type=F32).astype(o_ref.dtype)

    @pl.when(s >= n_tiles)
    def _():
        for c in range(IN_TM // IN_ROWS):
            rows = slice(c * IN_ROWS, (c + 1) * IN_ROWS)
            u = normed(rows)
            for t in range(n_tiles):
                o_ref[rows, t * IN_TN:(t + 1) * IN_TN] = jnp.dot(
                    u, w_scr[t], preferred_element_type=F32).astype(o_ref.dtype)


def _in_proj(x2, g, w):
    t, d = x2.shape
    n = w.shape[1]
    n_tiles = n // IN_TN
    row_block = lambda s: (jnp.maximum(s - (n_tiles - 1), 0), 0)
    return pl.pallas_call(
        functools.partial(_in_proj_kernel, n_tiles=n_tiles),
        grid=(n_tiles + t // IN_TM - 1,),
        in_specs=[
            pl.BlockSpec((IN_TM, d), row_block),
            pl.BlockSpec((1, d), lambda s: (0, 0)),
            pl.BlockSpec((d, IN_TN), lambda s: (0, jnp.minimum(s, n_tiles - 1))),
        ],
        out_specs=pl.BlockSpec((IN_TM, n), row_block),
        out_shape=jax.ShapeDtypeStruct((t, n), BF16),
        scratch_shapes=[pltpu.VMEM((IN_TM, d), BF16), pltpu.VMEM((n_tiles, d, IN_TN), BF16)],
        compiler_params=_params("arbitrary"),
        name="in_proj",
    )(x2, g, w)


def _rope(x, cos, sin_lo, sin_hi):
    return (x * cos
            + pltpu.roll(x, HEAD_DIM - ROPE_PAIRS, axis=1) * sin_lo
            + pltpu.roll(x, ROPE_PAIRS, axis=1) * sin_hi)


def _attn_kernel(q_ref, k_ref, v_ref, cq_ref, slq_ref, shq_ref, ck_ref, slk_ref, shk_ref,
                 qg_ref, kg_ref, wu_ref, wd_ref, o_ref, wu_out, wd_out, k_scr, v_scr):
    @pl.when(pl.program_id(2) == 0)
    def _():
        k = _rms(k_ref[...].astype(F32)) * kg_ref[...]
        k_scr[...] = _rope(k, ck_ref[...], slk_ref[...], shk_ref[...]).astype(BF16)
        v_scr[:, :HEAD_DIM] = v_ref[...]
        v_scr[:, HEAD_DIM:] = jnp.ones((v_ref.shape[0], HEAD_DIM), BF16)

    scale = math.log2(math.e) / math.sqrt(HEAD_DIM)
    cos, sin_lo, sin_hi = cq_ref[...], slq_ref[...], shq_ref[...]
    kk = k_scr[...]
    vv = v_scr[...]
    for r in range(ATTN_TQ // ATTN_ROWS):
        rows = slice(r * ATTN_ROWS, (r + 1) * ATTN_ROWS)
        for h in range(Q_PER_KV):
            sl = slice(h * HEAD_DIM, (h + 1) * HEAD_DIM)
            q = _rms(q_ref[rows, sl].astype(F32)) * qg_ref[...]
            q = (_rope(q, cos[rows], sin_lo[rows], sin_hi[rows]) * scale).astype(BF16)
            s = lax.dot_general(q, kk, (((1,), (1,)), ((), ())), preferred_element_type=F32)
            p = jnp.exp2(s - jnp.max(s, axis=-1, keepdims=True))
            o = jnp.dot(p.astype(BF16), vv, preferred_element_type=F32)
            o_ref[rows, sl] = (o[:, :HEAD_DIM] / o[:, HEAD_DIM:]).astype(o_ref.dtype)

    wu_out[...] = wu_ref[...].astype(BF16)
    wd_out[...] = wd_ref[...].astype(BF16)


def _attention(proj, tabs, qg, kg, w_up, w_down, batch, seq):
    t = proj.shape[0]
    nq = seq // ATTN_TQ
    gw = Q_PER_KV * HEAD_DIM
    k_blk0 = ATTN_WIDTH // HEAD_DIM
    v_blk0 = (ATTN_WIDTH + KV_WIDTH) // HEAD_DIM
    cos, sin_lo, sin_hi = tabs
    q_tab = pl.BlockSpec((ATTN_TQ, HEAD_DIM), lambda b, g, i: (i, 0))
    k_tab = pl.BlockSpec((seq, HEAD_DIM), lambda b, g, i: (0, 0))
    gain = pl.BlockSpec((1, HEAD_DIM), lambda b, g, i: (0, 0))

    n_steps = batch * N_KV_HEADS * nq
    step = lambda b, g, i: (b * N_KV_HEADS + g) * nq + i
    up_rows, down_rows = w_up.shape[0] // n_steps, w_down.shape[0] // n_steps
    assert up_rows * n_steps == w_up.shape[0] and up_rows % BF16_SUBLANES == 0
    assert down_rows * n_steps == w_down.shape[0] and down_rows % BF16_SUBLANES == 0
    up_blk = pl.BlockSpec((up_rows, w_up.shape[1]), lambda b, g, i: (step(b, g, i), 0))
    down_blk = pl.BlockSpec((down_rows, w_down.shape[1]), lambda b, g, i: (step(b, g, i), 0))
    return pl.pallas_call(
        _attn_kernel,
        grid=(batch, N_KV_HEADS, nq),
        in_specs=[
            pl.BlockSpec((ATTN_TQ, gw), lambda b, g, i: (b * nq + i, g)),
            pl.BlockSpec((seq, HEAD_DIM), lambda b, g, i: (b, k_blk0 + g)),
            pl.BlockSpec((seq, HEAD_DIM), lambda b, g, i: (b, v_blk0 + g)),
            q_tab, q_tab, q_tab, k_tab, k_tab, k_tab, gain, gain, up_blk, down_blk,
        ],
        out_specs=[pl.BlockSpec((ATTN_TQ, gw), lambda b, g, i: (b * nq + i, g)), up_blk, down_blk],
        out_shape=[jax.ShapeDtypeStruct((t, ATTN_WIDTH), BF16),
                   jax.ShapeDtypeStruct(w_up.shape, BF16), jax.ShapeDtypeStruct(w_down.shape, BF16)],
        scratch_shapes=[pltpu.VMEM((seq, HEAD_DIM), BF16), pltpu.VMEM((seq, 2 * HEAD_DIM), BF16)],
        compiler_params=_params("arbitrary", "arbitrary", "arbitrary"),
        name="attention",
    )(proj, proj, proj, cos, sin_lo, sin_hi, cos, sin_lo, sin_hi, qg, kg, w_up, w_down)


def _fourier_kernel(flo_ref, fhi_ref, w_ref, cc_ref, sc_ref, ca_ref, sa_ref, cr_ref, sr_ref, o_ref,
                    ab_ref, za_scr, zb_scr, z_stage, c_scr, ms_scr):
    m = pl.program_id(1)
    half = za_scr.shape[0] // 2
    rows = pl.ds(pl.multiple_of(m * FOURIER_TM, FOURIER_TM), FOURIER_TM)

    @pl.when((pl.program_id(0) == 0) & (m == 0))
    def _():
        for g in range(FOURIER_GROUPS):
            w = w_ref[g]
            a = jnp.dot(cc_ref[...], w, preferred_element_type=F32, precision=lax.Precision.HIGHEST)
            b = jnp.dot(sc_ref[...], w, preferred_element_type=F32, precision=lax.Precision.HIGHEST)
            ab_ref[g, :, :FOURIER_GROUP_DIM] = a.astype(BF16)
            ab_ref[g, :, FOURIER_GROUP_DIM:] = b.astype(BF16)

    @pl.when(pl.program_id(0) == 0)
    def _():
        cr, sr = cr_ref[...], sr_ref[...]
        per_block = FOURIER_TM // DFT_SUB
        for i in range(per_block):
            ca = ca_ref[pl.ds(m * per_block + i, 1), :]
            sa = sa_ref[pl.ds(m * per_block + i, 1), :]
            sub = pl.ds(pl.multiple_of(m * FOURIER_TM + i * DFT_SUB, DFT_SUB), DFT_SUB)
            c_scr[sub, :] = (ca * cr - sa * sr).astype(BF16)
            ms_scr[sub, :] = (-(sa * cr + ca * sr)).astype(BF16)

    @pl.when(m == 0)
    def _():
        groups_per_ref = FOURIER_GROUPS // 2
        for g in range(FOURIER_GROUPS):
            src = flo_ref if g < groups_per_ref else fhi_ref
            lo = (g % groups_per_ref) * FOURIER_GROUP_DIM
            z = jnp.dot(src[:, lo:lo + FOURIER_GROUP_DIM], ab_ref[g], preferred_element_type=F32)
            z_stage[0] = z[:, :FOURIER_GROUP_DIM]
            z_stage[1] = z[:, FOURIER_GROUP_DIM:]
            dst = slice(g * FOURIER_GROUP_DIM, (g + 1) * FOURIER_GROUP_DIM)
            for parity in range(2):
                part = slice(parity * half, (parity + 1) * half)
                za_scr[part, dst] = z_stage[0, pl.ds(parity, half, stride=2), :].astype(BF16)
                zb_scr[part, dst] = z_stage[1, pl.ds(parity, half, stride=2), :].astype(BF16)

    def half_sum(cols):
        return (jnp.dot(c_scr[rows, cols], za_scr[cols, :], preferred_element_type=F32)
                + jnp.dot(ms_scr[rows, cols], zb_scr[cols, :], preferred_element_type=F32))

    even, odd = half_sum(slice(0, half)), half_sum(slice(half, 2 * half))
    o_ref[0] = (even + odd).astype(o_ref.dtype)
    o_ref[1] = (even - odd).astype(o_ref.dtype)


def _fourier(proj, w_fmix, batch, seq):
    t = proj.shape[0]
    cc, sc = _dft_tables(FOURIER_GROUP_DIM, 1.0 / math.sqrt(seq * FOURIER_GROUP_DIM))
    half = seq // 2
    nm = half // FOURIER_TM
    half_w = FOURIER_WIDTH // 2
    f_blk0 = (ATTN_WIDTH + 2 * KV_WIDTH) // half_w
    cols = np.concatenate([np.arange(0, seq, 2), np.arange(1, seq, 2)])
    ca, sa = _dft_cos_sin(np.arange(half // DFT_SUB) * DFT_SUB, cols, seq)
    cr, sr = _dft_cos_sin(np.arange(DFT_SUB), cols, seq)
    whole = lambda arr: pl.BlockSpec(arr.shape, lambda b, m: (0,) * arr.ndim)
    out = pl.pallas_call(
        _fourier_kernel,
        grid=(batch, nm),
        in_specs=[
            pl.BlockSpec((seq, half_w), lambda b, m: (b, f_blk0)),
            pl.BlockSpec((seq, half_w), lambda b, m: (b, f_blk0 + 1)),
            whole(w_fmix), whole(cc), whole(sc), whole(ca), whole(sa), whole(cr), whole(sr),
        ],
        out_specs=pl.BlockSpec((None, 2, FOURIER_TM, FOURIER_WIDTH), lambda b, m: (b, 0, m, 0)),
        out_shape=jax.ShapeDtypeStruct((batch, 2, half, FOURIER_WIDTH), BF16),
        scratch_shapes=[pltpu.VMEM((FOURIER_GROUPS, FOURIER_GROUP_DIM, 2 * FOURIER_GROUP_DIM), BF16),
                        pltpu.VMEM((seq, FOURIER_WIDTH), BF16), pltpu.VMEM((seq, FOURIER_WIDTH), BF16),
                        pltpu.VMEM((2, seq, FOURIER_GROUP_DIM), F32),
                        pltpu.VMEM((half, seq), BF16), pltpu.VMEM((half, seq), BF16)],
        compiler_params=_params("arbitrary", "arbitrary"),
        name="fourier",
    )(proj, proj, w_fmix, cc, sc, ca, sa, cr, sr)
    return out.reshape(t, FOURIER_WIDTH)


def _out_proj_kernel(a_ref, f_ref, x_ref, mg_ref, w_ref, g2_ref, h_ref, u_ref, w_scr, *, n_tiles):
    s = pl.program_id(0)

    def inv_rms(ref):
        v = ref[...].astype(F32)
        return lax.rsqrt(jnp.mean(v * v, axis=-1, keepdims=True) + EPS)

    def column_tile(t, cols, ra, rf):
        h_ref[:, cols] = (x_ref[:, cols]
                          + ra * jnp.dot(a_ref[...], w_scr[t, :ATTN_WIDTH, :], preferred_element_type=F32)
                          + rf * jnp.dot(f_ref[...], w_scr[t, ATTN_WIDTH:, :], preferred_element_type=F32))

    def finish():
        u_ref[...] = (_rms(h_ref[...]) * g2_ref[...]).astype(u_ref.dtype)

    @pl.when(s < n_tiles)
    def _():
        w_scr[s] = (w_ref[...] * mg_ref[...]).astype(BF16)
        column_tile(s, pl.ds(pl.multiple_of(s * OUT_TN, OUT_TN), OUT_TN), inv_rms(a_ref), inv_rms(f_ref))

    @pl.when(s == n_tiles - 1)
    def _():
        finish()

    @pl.when(s >= n_tiles)
    def _():
        ra, rf = inv_rms(a_ref), inv_rms(f_ref)
        for t in range(n_tiles):
            column_tile(t, slice(t * OUT_TN, (t + 1) * OUT_TN), ra, rf)
        finish()


def _out_proj(a, f, x2, ag, fg, w, g2):
    t, d = x2.shape
    n_tiles = d // OUT_TN
    row_block = lambda s: (jnp.maximum(s - (n_tiles - 1), 0), 0)
    row = lambda width: pl.BlockSpec((OUT_TM, width), row_block)
    mix_gain = jnp.concatenate([ag, fg], axis=1).reshape(w.shape[0], 1)
    return pl.pallas_call(
        functools.partial(_out_proj_kernel, n_tiles=n_tiles),
        grid=(n_tiles + t // OUT_TM - 1,),
        in_specs=[
            row(ATTN_WIDTH), row(FOURIER_WIDTH), row(d),
            pl.BlockSpec((w.shape[0], 1), lambda s: (0, 0)),
            pl.BlockSpec((w.shape[0], OUT_TN), lambda s: (0, jnp.minimum(s, n_tiles - 1))),
            pl.BlockSpec((1, d), lambda s: (0, 0)),
        ],
        out_specs=[row(d), row(d)],
        out_shape=[jax.ShapeDtypeStruct((t, d), F32), jax.ShapeDtypeStruct((t, d), BF16)],
        scratch_shapes=[pltpu.VMEM((n_tiles, w.shape[0], OUT_TN), BF16)],
        compiler_params=_params("arbitrary"),
        name="out_proj",
    )(a, f, x2, mix_gain, w, g2)


def _ffn_kernel(u_hbm, wg_ref, wv_ref, cwg_ref, cwv_ref, cbg_ref, cbv_ref, wd_ref, h_ref, g_ref,
                y_hbm, acc_ref, gate_scr, val_scr, u_ref, y_buf, u_sem, y_sem):
    b, j = pl.program_id(0), pl.program_id(1)
    n_seq, n_ff = pl.num_programs(0), pl.num_programs(1)
    seq = u_ref.shape[0]
    n_out = seq // FFN_OUT_ROWS
    assert sum(FFN_ROW_SPLIT) == seq
    starts = [sum(FFN_ROW_SPLIT[:c]) for c in range(len(FFN_ROW_SPLIT))]

    def u_copy(seq_idx):
        return pltpu.make_async_copy(u_hbm.at[pl.ds(seq_idx * seq, seq), :], u_ref, u_sem)

    def y_copy(piece, slot):
        dst = y_hbm.at[pl.ds(b * seq + piece * FFN_Y_ROWS, FFN_Y_ROWS), :]
        return pltpu.make_async_copy(y_buf.at[slot], dst, y_sem.at[slot])

    @pl.when(j == 0)
    def _():
        @pl.when(b == 0)
        def _():
            acc_ref[...] = jnp.zeros_like(acc_ref)
            gate_scr[...] = jnp.zeros_like(gate_scr)
            val_scr[...] = jnp.zeros_like(val_scr)
            u_copy(0).start()

        u_copy(b).wait()

    wg, wv, wd = wg_ref[...], wv_ref[...], wd_ref[...]
    cwg, cwv, cbg, cbv = cwg_ref[...], cwv_ref[...], cbg_ref[...], cbv_ref[...]

    def conv3(scr, cw, cb, r0, rc):
        prev = scr[pl.ds(FFN_PAD - 1 + r0, rc), :]
        cur = scr[pl.ds(FFN_PAD + r0, rc), :]
        nxt = scr[pl.ds(FFN_PAD + 1 + r0, rc), :]
        return prev * cw[0:1, :] + cur * cw[1:2, :] + nxt * cw[2:3, :] + cb

    def gate_and_down(c):
        r0, rc = starts[c], FFN_ROW_SPLIT[c]
        gate = conv3(gate_scr, cwg, cbg, r0, rc)
        val = conv3(val_scr, cwv, cbv, r0, rc)
        act = (gate * jax.nn.sigmoid(gate) * val).astype(BF16)
        rows = pl.ds(r0, rc)
        acc_ref[rows, :] = (jnp.where(j == 0, 0.0, acc_ref[rows, :])
                            + jnp.dot(act, wd, preferred_element_type=F32))

    for c, (r0, rc) in enumerate(zip(starts, FFN_ROW_SPLIT)):
        u = u_ref[pl.ds(r0, rc), :]
        gate_scr[pl.ds(FFN_PAD + r0, rc), :] = jnp.dot(u, wg, preferred_element_type=F32)
        val_scr[pl.ds(FFN_PAD + r0, rc), :] = jnp.dot(u, wv, preferred_element_type=F32)
        if c >= 1:
            gate_and_down(c - 1)
    gate_and_down(len(FFN_ROW_SPLIT) - 1)

    rows = pl.ds(pl.multiple_of(jnp.minimum(j, n_out - 1) * FFN_OUT_ROWS, FFN_OUT_ROWS), FFN_OUT_ROWS)
    acc_ref[rows, :] += jnp.where(j < n_out, h_ref[...], 0.0)

    @pl.when(j == n_ff - 1)
    def _():
        @pl.when(b + 1 < n_seq)
        def _():
            u_copy(b + 1).start()

        n_pieces = seq // FFN_Y_ROWS
        for piece in range(n_pieces):
            slot = piece % 2
            if piece >= 2:
                y_copy(piece - 2, slot).wait()
            rows = slice(piece * FFN_Y_ROWS, (piece + 1) * FFN_Y_ROWS)
            y_buf[slot] = _rms(acc_ref[rows, :]) * g_ref[...]
            y_copy(piece, slot).start()
        for piece in range(max(n_pieces - 2, 0), n_pieces):
            y_copy(piece, piece % 2).wait()


def _ffn(u2, w_up, conv_w, conv_b, w_down, h1, g, batch, seq):
    t, d = u2.shape
    d_ff = w_down.shape[0]
    n_ff = d_ff // FFN_TN
    n_out = seq // FFN_OUT_ROWS
    taps = conv_w.shape[0]
    gate_col = lambda b, j: (0, j)
    val_col = lambda b, j: (0, n_ff + j)
    res_row = lambda b, j: (b * n_out + jnp.minimum(j, n_out - 1), 0)
    assert n_out <= n_ff
    return pl.pallas_call(
        _ffn_kernel,
        grid=(batch, n_ff),
        in_specs=[
            pl.BlockSpec(memory_space=pl.ANY),
            pl.BlockSpec((d, FFN_TN), gate_col),
            pl.BlockSpec((d, FFN_TN), val_col),
            pl.BlockSpec((taps, FFN_TN), gate_col),
            pl.BlockSpec((taps, FFN_TN), val_col),
            pl.BlockSpec((1, FFN_TN), gate_col),
            pl.BlockSpec((1, FFN_TN), val_col),
            pl.BlockSpec((FFN_TN, d), lambda b, j: (j, 0)),
            pl.BlockSpec((FFN_OUT_ROWS, d), res_row),
            pl.BlockSpec((1, d), lambda b, j: (0, 0)),
        ],
        out_specs=pl.BlockSpec(memory_space=pl.ANY),
        out_shape=jax.ShapeDtypeStruct((t, d), F32),
        scratch_shapes=[
            pltpu.VMEM((seq, d), F32),
            pltpu.VMEM((seq + 2 * FFN_PAD, FFN_TN), F32),
            pltpu.VMEM((seq + 2 * FFN_PAD, FFN_TN), F32),
            pltpu.VMEM((seq, d), BF16),
            pltpu.VMEM((2, FFN_Y_ROWS, d), F32),
            pltpu.SemaphoreType.DMA(()),
            pltpu.SemaphoreType.DMA((2,)),
        ],
        compiler_params=pltpu.CompilerParams(dimension_semantics=("arbitrary", "arbitrary"),
                                             vmem_limit_bytes=FFN_VMEM_LIMIT_BYTES),
        name="ffn",
    )(u2, w_up, w_up, conv_w, conv_w, conv_b, conv_b, w_down, h1, g)


def _rope_tables(seq):
    t = np.arange(seq)
    inv_freq = ROPE_THETA ** (-np.arange(ROPE_PAIRS, dtype=np.float64) / ROPE_PAIRS)

    def axis_tables(pos):
        ang = pos[:, None] * inv_freq[None, :]
        ang = np.concatenate([ang, ang], axis=-1)
        return np.cos(ang), np.sin(ang)

    cos_r, sin_r = axis_tables(t // GRID_W)
    cos_c, sin_c = axis_tables(t % GRID_W)
    cos = np.concatenate([cos_r, cos_c], axis=-1)
    sin = np.concatenate([sin_r, sin_c], axis=-1)
    first_half = (np.arange(HEAD_DIM) % (2 * ROPE_PAIRS)) < ROPE_PAIRS
    sin_lo = np.where(first_half[None, :], -sin, 0.0)
    sin_hi = np.where(first_half[None, :], 0.0, sin)
    return tuple(jnp.asarray(tab, F32) for tab in (cos, sin_lo, sin_hi))


def _dft_cos_sin(rows, cols, n, scale=1.0):
    ang = ((rows[:, None] * cols[None, :]) % n) * (2.0 * math.pi / n)
    return jnp.asarray(np.cos(ang) * scale, F32), jnp.asarray(np.sin(ang) * scale, F32)


def _dft_tables(n, scale):
    return _dft_cos_sin(np.arange(n), np.arange(n), n, scale)


def kernel(x, norm1_g, w_in, q_norm_g, k_norm_g, w_fmix, attn_out_g, fourier_out_g, w_out,
           norm2_g, w_up, conv_w, conv_b, w_down, final_g):
    batch, seq, d = x.shape
    depth = w_in.shape[0]
    rope = _rope_tables(seq)

    assert depth == 1
    h = x.reshape(batch * seq, d)
    for l in range(depth):
        proj = _in_proj(h, norm1_g[l][None], w_in[l])
        attn, w_up_bf16, w_down_bf16 = _attention(proj, rope, q_norm_g[l][None], k_norm_g[l][None],
                                                  w_up[l], w_down[l], batch, seq)
        four = _fourier(proj, w_fmix[l], batch, seq)
        h1, u2 = _out_proj(attn, four, h, attn_out_g[l][None], fourier_out_g[l][None],
                           w_out[l], norm2_g[l][None])
        h = _ffn(u2, w_up_bf16, conv_w[l], conv_b[l][None], w_down_bf16, h1, final_g[None], batch, seq)
    return h.reshape(batch, seq, d)
```

```python
import functools
import math

import jax
import jax.numpy as jnp
import numpy as np
from jax import lax
from jax.experimental import pallas as pl
from jax.experimental.pallas import tpu as pltpu

F32 = jnp.float32
BF16 = jnp.bfloat16

HEAD_DIM = 128
N_Q_HEADS = 8
N_KV_HEADS = 2
Q_PER_KV = N_Q_HEADS // N_KV_HEADS
ATTN_WIDTH = N_Q_HEADS * HEAD_DIM
KV_WIDTH = N_KV_HEADS * HEAD_DIM
FOURIER_GROUPS = 8
FOURIER_GROUP_DIM = 128
FOURIER_WIDTH = FOURIER_GROUPS * FOURIER_GROUP_DIM
GRID_W = 64
ROPE_THETA = 10000.0
ROPE_PAIRS = HEAD_DIM // 4
EPS = 1e-6

VMEM_LIMIT_BYTES = 56 * 1024 * 1024

IN_TM, IN_TN, IN_ROWS = 1024, 512, 512
ATTN_TQ, ATTN_ROWS = 1024, 256
BF16_SUBLANES = 16
FOURIER_TM = 512
DFT_SUB = 64
OUT_TM, OUT_TN = 512, 512
FFN_TN = 512
FFN_ROW_SPLIT = (1280, 768)
FFN_PAD = 8
FFN_OUT_ROWS = 256
FFN_Y_ROWS = 256
FFN_VMEM_LIMIT_BYTES = 60 * 1024 * 1024


def _params(*sem):
    return pltpu.CompilerParams(dimension_semantics=sem, vmem_limit_bytes=VMEM_LIMIT_BYTES)


def _rms(x):
    return x * lax.rsqrt(jnp.mean(x * x, axis=-1, keepdims=True) + EPS)


def _in_proj_kernel(x_ref, g_ref, w_ref, o_ref, u_scr, w_scr, *, n_tiles):
    s = pl.program_id(0)

    def normed(rows):
        return (_rms(x_ref[rows, :]) * g_ref[...]).astype(BF16)

    @pl.when(s == 0)
    def _():
        u_scr[...] = normed(slice(None))

    @pl.when(s < n_tiles)
    def _():
        w_scr[s] = w_ref[...].astype(BF16)
        cols = pl.ds(pl.multiple_of(s * IN_TN, IN_TN), IN_TN)
        o_ref[:, cols] = jnp.dot(u_scr[...], w_scr[s], preferred_element_type=F32).astype(o_ref.dtype)

    @pl.when(s >= n_tiles)
    def _():
        for c in range(IN_TM // IN_ROWS):
            rows = slice(c * IN_ROWS, (c + 1) * IN_ROWS)
            u = normed(rows)
            for t in range(n_tiles):
                o_ref[rows, t * IN_TN:(t + 1) * IN_TN] = jnp.dot(
                    u, w_scr[t], preferred_element_type=F32).astype(o_ref.dtype)


def _in_proj(x2, g, w):
    t, d = x2.shape
    n = w.shape[1]
    n_tiles = n // IN_TN
    row_block = lambda s: (jnp.maximum(s - (n_tiles - 1), 0), 0)
    return pl.pallas_call(
        functools.partial(_in_proj_kernel, n_tiles=n_tiles),
        grid=(n_tiles + t // IN_TM - 1,),
        in_specs=[
            pl.BlockSpec((IN_TM, d), row_block),
            pl.BlockSpec((1, d), lambda s: (0, 0)),
            pl.BlockSpec((d, IN_TN), lambda s: (0, jnp.minimum(s, n_tiles - 1))),
        ],
        out_specs=pl.BlockSpec((IN_TM, n), row_block),
        out_shape=jax.ShapeDtypeStruct((t, n), BF16),
        scratch_shapes=[pltpu.VMEM((IN_TM, d), BF16), pltpu.VMEM((n_tiles, d, IN_TN), BF16)],
        compiler_params=_params("arbitrary"),
        name="in_proj",
    )(x2, g, w)


def _rope(x, cos, sin_lo, sin_hi):
    return (x * cos
            + pltpu.roll(x, HEAD_DIM - ROPE_PAIRS, axis=1) * sin_lo
            + pltpu.roll(x, ROPE_PAIRS, axis=1) * sin_hi)


def _attn_kernel(q_ref, k_ref, v_ref, cq_ref, slq_ref, shq_ref, ck_ref, slk_ref, shk_ref,
                 qg_ref, kg_ref, wu_ref, wd_ref, o_ref, wu_out, wd_out, k_scr, v_scr):
    @pl.when(pl.program_id(2) == 0)
    def _():
        k = _rms(k_ref[...].astype(F32)) * kg_ref[...]
        k_scr[...] = _rope(k, ck_ref[...], slk_ref[...], shk_ref[...]).astype(BF16)
        v_scr[:, :HEAD_DIM] = v_ref[...]
        v_scr[:, HEAD_DIM:] = jnp.ones((v_ref.shape[0], HEAD_DIM), BF16)

    scale = math.log2(math.e) / math.sqrt(HEAD_DIM)
    for r in range(ATTN_TQ // ATTN_ROWS):
        rows = slice(r * ATTN_ROWS, (r + 1) * ATTN_ROWS)
        for h in range(Q_PER_KV):
            sl = slice(h * HEAD_DIM, (h + 1) * HEAD_DIM)
            q = _rms(q_ref[rows, sl].astype(F32)) * qg_ref[...]
            q = (_rope(q, cq_ref[rows, :], slq_ref[rows, :], shq_ref[rows, :]) * scale).astype(BF16)
            s = lax.dot_general(q, k_scr[...], (((1,), (1,)), ((), ())), preferred_element_type=F32)
            p = jnp.exp2(s - jnp.max(s, axis=-1, keepdims=True))
            o = jnp.dot(p.astype(BF16), v_scr[...], preferred_element_type=F32)
            o_ref[rows, sl] = (o[:, :HEAD_DIM] / o[:, HEAD_DIM:]).astype(o_ref.dtype)

    wu_out[...] = wu_ref[...].astype(BF16)
    wd_out[...] = wd_ref[...].astype(BF16)


def _attention(proj, tabs, qg, kg, w_up, w_down, batch, seq):
    t = proj.shape[0]
    nq = seq // ATTN_TQ
    gw = Q_PER_KV * HEAD_DIM
    k_blk0 = ATTN_WIDTH // HEAD_DIM
    v_blk0 = (ATTN_WIDTH + KV_WIDTH) // HEAD_DIM
    cos, sin_lo, sin_hi = tabs
    q_tab = pl.BlockSpec((ATTN_TQ, HEAD_DIM), lambda b, g, i: (i, 0))
    k_tab = pl.BlockSpec((seq, HEAD_DIM), lambda b, g, i: (0, 0))
    gain = pl.BlockSpec((1, HEAD_DIM), lambda b, g, i: (0, 0))

    n_steps = batch * N_KV_HEADS * nq
    step = lambda b, g, i: (b * N_KV_HEADS + g) * nq + i
    up_rows, down_rows = w_up.shape[0] // n_steps, w_down.shape[0] // n_steps
    assert up_rows * n_steps == w_up.shape[0] and up_rows % BF16_SUBLANES == 0
    assert down_rows * n_steps == w_down.shape[0] and down_rows % BF16_SUBLANES == 0
    up_blk = pl.BlockSpec((up_rows, w_up.shape[1]), lambda b, g, i: (step(b, g, i), 0))
    down_blk = pl.BlockSpec((down_rows, w_down.shape[1]), lambda b, g, i: (step(b, g, i), 0))
    return pl.pallas_call(
        _attn_kernel,
        grid=(batch, N_KV_HEADS, nq),
        in_specs=[
            pl.BlockSpec((ATTN_TQ, gw), lambda b, g, i: (b * nq + i, g)),
            pl.BlockSpec((seq, HEAD_DIM), lambda b, g, i: (b, k_blk0 + g)),
            pl.BlockSpec((seq, HEAD_DIM), lambda b, g, i: (b, v_blk0 + g)),
            q_tab, q_tab, q_tab, k_tab, k_tab, k_tab, gain, gain, up_blk, down_blk,
        ],
        out_specs=[pl.BlockSpec((ATTN_TQ, gw), lambda b, g, i: (b * nq + i, g)), up_blk, down_blk],
        out_shape=[jax.ShapeDtypeStruct((t, ATTN_WIDTH), BF16),
                   jax.ShapeDtypeStruct(w_up.shape, BF16), jax.ShapeDtypeStruct(w_down.shape, BF16)],
        scratch_shapes=[pltpu.VMEM((seq, HEAD_DIM), BF16), pltpu.VMEM((seq, 2 * HEAD_DIM), BF16)],
        compiler_params=_params("arbitrary", "arbitrary", "arbitrary"),
        name="attention",
    )(proj, proj, proj, cos, sin_lo, sin_hi, cos, sin_lo, sin_hi, qg, kg, w_up, w_down)


def _fourier_kernel(flo_ref, fhi_ref, w_ref, cc_ref, sc_ref, ca_ref, sa_ref, cr_ref, sr_ref, o_ref,
                    ab_ref, za_scr, zb_scr, z_stage, c_scr, ms_scr):
    m = pl.program_id(1)
    half = za_scr.shape[0] // 2
    rows = pl.ds(pl.multiple_of(m * FOURIER_TM, FOURIER_TM), FOURIER_TM)

    @pl.when((pl.program_id(0) == 0) & (m == 0))
    def _():
        for g in range(FOURIER_GROUPS):
            w = w_ref[g]
            a = jnp.dot(cc_ref[...], w, preferred_element_type=F32, precision=lax.Precision.HIGHEST)
            b = jnp.dot(sc_ref[...], w, preferred_element_type=F32, precision=lax.Precision.HIGHEST)
            ab_ref[g, :, :FOURIER_GROUP_DIM] = a.astype(BF16)
            ab_ref[g, :, FOURIER_GROUP_DIM:] = b.astype(BF16)

    @pl.when(pl.program_id(0) == 0)
    def _():
        cr, sr = cr_ref[...], sr_ref[...]
        per_block = FOURIER_TM // DFT_SUB
        for i in range(per_block):
            ca = ca_ref[pl.ds(m * per_block + i, 1), :]
            sa = sa_ref[pl.ds(m * per_block + i, 1), :]
            sub = pl.ds(pl.multiple_of(m * FOURIER_TM + i * DFT_SUB, DFT_SUB), DFT_SUB)
            c_scr[sub, :] = (ca * cr - sa * sr).astype(BF16)
            ms_scr[sub, :] = (-(sa * cr + ca * sr)).astype(BF16)

    @pl.when(m == 0)
    def _():
        groups_per_ref = FOURIER_GROUPS // 2
        for g in range(FOURIER_GROUPS):
            src = flo_ref if g < groups_per_ref else fhi_ref
            lo = (g % groups_per_ref) * FOURIER_GROUP_DIM
            z = jnp.dot(src[:, lo:lo + FOURIER_GROUP_DIM], ab_ref[g], preferred_element_type=F32)
            z_stage[0] = z[:, :FOURIER_GROUP_DIM]
            z_stage[1] = z[:, FOURIER_GROUP_DIM:]
            dst = slice(g * FOURIER_GROUP_DIM, (g + 1) * FOURIER_GROUP_DIM)
            for parity in range(2):
                part = slice(parity * half, (parity + 1) * half)
                za_scr[part, dst] = z_stage[0, pl.ds(parity, half, stride=2), :].astype(BF16)
                zb_scr[part, dst] = z_stage[1, pl.ds(parity, half, stride=2), :].astype(BF16)

    def half_sum(cols):
        return (jnp.dot(c_scr[rows, cols], za_scr[cols, :], preferred_element_type=F32)
                + jnp.dot(ms_scr[rows, cols], zb_scr[cols, :], preferred_element_type=F32))

    even, odd = half_sum(slice(0, half)), half_sum(slice(half, 2 * half))
    o_ref[0] = (even + odd).astype(o_ref.dtype)
    o_ref[1] = (even - odd).astype(o_ref.dtype)


def _fourier(proj, w_fmix, batch, seq):
    t = proj.shape[0]
    cc, sc = _dft_tables(FOURIER_GROUP_DIM, 1.0 / math.sqrt(seq * FOURIER_GROUP_DIM))
    half = seq // 2
    nm = half // FOURIER_TM
    half_w = FOURIER_WIDTH // 2
    f_blk0 = (ATTN_WIDTH + 2 * KV_WIDTH) // half_w
    cols = np.concatenate([np.arange(0, seq, 2), np.arange(1, seq, 2)])
    ca, sa = _dft_cos_sin(np.arange(half // DFT_SUB) * DFT_SUB, cols, seq)
    cr, sr = _dft_cos_sin(np.arange(DFT_SUB), cols, seq)
    whole = lambda arr: pl.BlockSpec(arr.shape, lambda b, m: (0,) * arr.ndim)
    out = pl.pallas_call(
        _fourier_kernel,
        grid=(batch, nm),
        in_specs=[
            pl.BlockSpec((seq, half_w), lambda b, m: (b, f_blk0)),
            pl.BlockSpec((seq, half_w), lambda b, m: (b, f_blk0 + 1)),
            whole(w_fmix), whole(cc), whole(sc), whole(ca), whole(sa), whole(cr), whole(sr),
        ],
        out_specs=pl.BlockSpec((None, 2, FOURIER_TM, FOURIER_WIDTH), lambda b, m: (b, 0, m, 0)),
        out_shape=jax.ShapeDtypeStruct((batch, 2, half, FOURIER_WIDTH), BF16),
        scratch_shapes=[pltpu.VMEM((FOURIER_GROUPS, FOURIER_GROUP_DIM, 2 * FOURIER_GROUP_DIM), BF16),
                        pltpu.VMEM((seq, FOURIER_WIDTH), BF16), pltpu.VMEM((seq, FOURIER_WIDTH), BF16),
                        pltpu.VMEM((2, seq, FOURIER_GROUP_DIM), F32),
                        pltpu.VMEM((half, seq), BF16), pltpu.VMEM((half, seq), BF16)],
        compiler_params=_params("arbitrary", "arbitrary"),
        name="fourier",
    )(proj, proj, w_fmix, cc, sc, ca, sa, cr, sr)
    return out.reshape(t, FOURIER_WIDTH)


def _out_proj_kernel(a_ref, f_ref, x_ref, mg_ref, w_ref, g2_ref, h_ref, u_ref, w_scr, *, n_tiles):
    s = pl.program_id(0)

    def inv_rms(ref):
        v = ref[...].astype(F32)
        return lax.rsqrt(jnp.mean(v * v, axis=-1, keepdims=True) + EPS)

    def column_tile(t, cols, ra, rf):
        h_ref[:, cols] = (x_ref[:, cols]
                          + ra * jnp.dot(a_ref[...], w_scr[t, :ATTN_WIDTH, :], preferred_element_type=F32)
                          + rf * jnp.dot(f_ref[...], w_scr[t, ATTN_WIDTH:, :], preferred_element_type=F32))

    def finish():
        u_ref[...] = (_rms(h_ref[...]) * g2_ref[...]).astype(u_ref.dtype)

    @pl.when(s < n_tiles)
    def _():
        w_scr[s] = (w_ref[...] * mg_ref[...]).astype(BF16)
        column_tile(s, pl.ds(pl.multiple_of(s * OUT_TN, OUT_TN), OUT_TN), inv_rms(a_ref), inv_rms(f_ref))

    @pl.when(s == n_tiles - 1)
    def _():
        finish()

    @pl.when(s >= n_tiles)
    def _():
        ra, rf = inv_rms(a_ref), inv_rms(f_ref)
        for t in range(n_tiles):
            column_tile(t, slice(t * OUT_TN, (t + 1) * OUT_TN), ra, rf)
        finish()


def _out_proj(a, f, x2, ag, fg, w, g2):
    t, d = x2.shape
    n_tiles = d // OUT_TN
    row_block = lambda s: (jnp.maximum(s - (n_tiles - 1), 0), 0)
    row = lambda width: pl.BlockSpec((OUT_TM, width), row_block)
    mix_gain = jnp.concatenate([ag, fg], axis=1).reshape(w.shape[0], 1)
    return pl.pallas_call(
        functools.partial(_out_proj_kernel, n_tiles=n_tiles),
        grid=(n_tiles + t // OUT_TM - 1,),
        in_specs=[
            row(ATTN_WIDTH), row(FOURIER_WIDTH), row(d),
            pl.BlockSpec((w.shape[0], 1), lambda s: (0, 0)),
            pl.BlockSpec((w.shape[0], OUT_TN), lambda s: (0, jnp.minimum(s, n_tiles - 1))),
            pl.BlockSpec((1, d), lambda s: (0, 0)),
        ],
        out_specs=[row(d), row(d)],
        out_shape=[jax.ShapeDtypeStruct((t, d), F32), jax.ShapeDtypeStruct((t, d), BF16)],
        scratch_shapes=[pltpu.VMEM((n_tiles, w.shape[0], OUT_TN), BF16)],
        compiler_params=_params("arbitrary"),
        name="out_proj",
    )(a, f, x2, mix_gain, w, g2)


def _ffn_kernel(u_hbm, wg_ref, wv_ref, cwg_ref, cwv_ref, cbg_ref, cbv_ref, wd_ref, h_ref, g_ref,
                y_hbm, acc_ref, gate_scr, val_scr, u_ref, y_buf, u_sem, y_sem):
    b, j = pl.program_id(0), pl.program_id(1)
    n_seq, n_ff = pl.num_programs(0), pl.num_programs(1)
    seq = u_ref.shape[0]
    n_out = seq // FFN_OUT_ROWS
    assert sum(FFN_ROW_SPLIT) == seq
    starts = [sum(FFN_ROW_SPLIT[:c]) for c in range(len(FFN_ROW_SPLIT))]

    def u_copy(seq_idx):
        return pltpu.make_async_copy(u_hbm.at[pl.ds(seq_idx * seq, seq), :], u_ref, u_sem)

    def y_copy(piece, slot):
        dst = y_hbm.at[pl.ds(b * seq + piece * FFN_Y_ROWS, FFN_Y_ROWS), :]
        return pltpu.make_async_copy(y_buf.at[slot], dst, y_sem.at[slot])

    @pl.when(j == 0)
    def _():
        @pl.when(b == 0)
        def _():
            acc_ref[...] = jnp.zeros_like(acc_ref)
            gate_scr[...] = jnp.zeros_like(gate_scr)
            val_scr[...] = jnp.zeros_like(val_scr)
            u_copy(0).start()

        u_copy(b).wait()

    wg, wv, wd = wg_ref[...], wv_ref[...], wd_ref[...]
    cwg, cwv, cbg, cbv = cwg_ref[...], cwv_ref[...], cbg_ref[...], cbv_ref[...]

    def conv3(scr, cw, cb, r0, rc):
        prev = scr[pl.ds(FFN_PAD - 1 + r0, rc), :]
        cur = scr[pl.ds(FFN_PAD + r0, rc), :]
        nxt = scr[pl.ds(FFN_PAD + 1 + r0, rc), :]
        return prev * cw[0:1, :] + cur * cw[1:2, :] + nxt * cw[2:3, :] + cb

    def gate_and_down(c):
        r0, rc = starts[c], FFN_ROW_SPLIT[c]
        gate = conv3(gate_scr, cwg, cbg, r0, rc)
        val = conv3(val_scr, cwv, cbv, r0, rc)
        act = (gate * jax.nn.sigmoid(gate) * val).astype(BF16)
        rows = pl.ds(r0, rc)
        acc_ref[rows, :] = (jnp.where(j == 0, 0.0, acc_ref[rows, :])
                            + jnp.dot(act, wd, preferred_element_type=F32))

    for c, (r0, rc) in enumerate(zip(starts, FFN_ROW_SPLIT)):
        u = u_ref[pl.ds(r0, rc), :]
        gate_scr[pl.ds(FFN_PAD + r0, rc), :] = jnp.dot(u, wg, preferred_element_type=F32)
        val_scr[pl.ds(FFN_PAD + r0, rc), :] = jnp.dot(u, wv, preferred_element_type=F32)
        if c >= 1:
            gate_and_down(c - 1)
    gate_and_down(len(FFN_ROW_SPLIT) - 1)

    rows = pl.ds(pl.multiple_of(jnp.minimum(j, n_out - 1) * FFN_OUT_ROWS, FFN_OUT_ROWS), FFN_OUT_ROWS)
    acc_ref[rows, :] += jnp.where(j < n_out, h_ref[...], 0.0)

    @pl.when(j == n_ff - 1)
    def _():
        @pl.when(b + 1 < n_seq)
        def _():
            u_copy(b + 1).start()

        n_pieces = seq // FFN_Y_ROWS
        for piece in range(n_pieces):
            slot = piece % 2
            if piece >= 2:
                y_copy(piece - 2, slot).wait()
            rows = slice(piece * FFN_Y_ROWS, (piece + 1) * FFN_Y_ROWS)
            y_buf[slot] = _rms(acc_ref[rows, :]) * g_ref[...]
            y_copy(piece, slot).start()
        for piece in range(max(n_pieces - 2, 0), n_pieces):
            y_copy(piece, piece % 2).wait()


def _ffn(u2, w_up, conv_w, conv_b, w_down, h1, g, batch, seq):
    t, d = u2.shape
    d_ff = w_down.shape[0]
    n_ff = d_ff // FFN_TN
    n_out = seq // FFN_OUT_ROWS
    taps = conv_w.shape[0]
    gate_col = lambda b, j: (0, j)
    val_col = lambda b, j: (0, n_ff + j)
    res_row = lambda b, j: (b * n_out + jnp.minimum(j, n_out - 1), 0)
    assert n_out <= n_ff
    return pl.pallas_call(
        _ffn_kernel,
        grid=(batch, n_ff),
        in_specs=[
            pl.BlockSpec(memory_space=pl.ANY),
            pl.BlockSpec((d, FFN_TN), gate_col),
            pl.BlockSpec((d, FFN_TN), val_col),
            pl.BlockSpec((taps, FFN_TN), gate_col),
            pl.BlockSpec((taps, FFN_TN), val_col),
            pl.BlockSpec((1, FFN_TN), gate_col),
            pl.BlockSpec((1, FFN_TN), val_col),
            pl.BlockSpec((FFN_TN, d), lambda b, j: (j, 0)),
            pl.BlockSpec((FFN_OUT_ROWS, d), res_row),
            pl.BlockSpec((1, d), lambda b, j: (0, 0)),
        ],
        out_specs=pl.BlockSpec(memory_space=pl.ANY),
        out_shape=jax.ShapeDtypeStruct((t, d), F32),
        scratch_shapes=[
            pltpu.VMEM((seq, d), F32),
            pltpu.VMEM((seq + 2 * FFN_PAD, FFN_TN), F32),
            pltpu.VMEM((seq + 2 * FFN_PAD, FFN_TN), F32),
            pltpu.VMEM((seq, d), BF16),
            pltpu.VMEM((2, FFN_Y_ROWS, d), F32),
            pltpu.SemaphoreType.DMA(()),
            pltpu.SemaphoreType.DMA((2,)),
        ],
        compiler_params=pltpu.CompilerParams(dimension_semantics=("arbitrary", "arbitrary"),
                                             vmem_limit_bytes=FFN_VMEM_LIMIT_BYTES),
        name="ffn",
    )(u2, w_up, w_up, conv_w, conv_w, conv_b, conv_b, w_down, h1, g)


def _rope_tables(seq):
    t = np.arange(seq)
    inv_freq = ROPE_THETA ** (-np.arange(ROPE_PAIRS, dtype=np.float64) / ROPE_PAIRS)

    def axis_tables(pos):
        ang = pos[:, None] * inv_freq[None, :]
        ang = np.concatenate([ang, ang], axis=-1)
        return np.cos(ang), np.sin(ang)

    cos_r, sin_r = axis_tables(t // GRID_W)
    cos_c, sin_c = axis_tables(t % GRID_W)
    cos = np.concatenate([cos_r, cos_c], axis=-1)
    sin = np.concatenate([sin_r, sin_c], axis=-1)
    first_half = (np.arange(HEAD_DIM) % (2 * ROPE_PAIRS)) < ROPE_PAIRS
    sin_lo = np.where(first_half[None, :], -sin, 0.0)
    sin_hi = np.where(first_half[None, :], 0.0, sin)
    return tuple(jnp.asarray(tab, F32) for tab in (cos, sin_lo, sin_hi))


def _dft_cos_sin(rows, cols, n, scale=1.0):
    ang = ((rows[:, None] * cols[None, :]) % n) * (2.0 * math.pi / n)
    return jnp.asarray(np.cos(ang) * scale, F32), jnp.asarray(np.sin(ang) * scale, F32)


def _dft_tables(n, scale):
    return _dft_cos_sin(np.arange(n), np.arange(n), n, scale)


def kernel(x, norm1_g, w_in, q_norm_g, k_norm_g, w_fmix, attn_out_g, fourier_out_g, w_out,
           norm2_g, w_up, conv_w, conv_b, w_down, final_g):
    batch, seq, d = x.shape
    depth = w_in.shape[0]
    rope = _rope_tables(seq)

    assert depth == 1
    h = x.reshape(batch * seq, d)
    for l in range(depth):
        proj = _in_proj(h, norm1_g[l][None], w_in[l])
        attn, w_up_bf16, w_down_bf16 = _attention(proj, rope, q_norm_g[l][None], k_norm_g[l][None],
                                                  w_up[l], w_down[l], batch, seq)
        four = _fourier(proj, w_fmix[l], batch, seq)
        h1, u2 = _out_proj(attn, four, h, attn_out_g[l][None], fourier_out_g[l][None],
                           w_out[l], norm2_g[l][None])
        h = _ffn(u2, w_up_bf16, conv_w[l], conv_b[l][None], w_down_bf16, h1, final_g[None], batch, seq)
    return h.reshape(batch, seq, d)
```
